```python
import jax, jax.numpy as jnp
from jax import lax
import numpy as np

D_MODEL = 4096
BATCH = 1
SEQ = 8192
DEPTH = 2

N_MEM = 256
NORM_EPS = 1e-6
MIX_A = D_MODEL // 2
MIX_B = D_MODEL // 2
MLSTM_HEADS = 4
MLSTM_DV = MIX_A // MLSTM_HEADS
MLSTM_DK = MLSTM_DV // 2
MLSTM_CHUNK = 64
GATE_CAP = 15.0
HGRN_EXPAND = 128
HGRN_HEADS = MIX_B // HGRN_EXPAND
HGRN_DK = HGRN_EXPAND
HGRN_DV = MIX_B // HGRN_HEADS
HGRN_CHUNK = 64
IN_SIZES = (MLSTM_HEADS * MLSTM_DK, MLSTM_HEADS * MLSTM_DK, MIX_A, MIX_A, MLSTM_HEADS, MLSTM_HEADS, MIX_B, MIX_B, MIX_B, MIX_B)
IN_COLS = sum(IN_SIZES)
RWKV_HEAD = 64
RWKV_HEADS = D_MODEL // RWKV_HEAD
RWKV_DECAY_LORA = max(32, int(round(1.8 * D_MODEL ** 0.5 / 32)) * 32)
RWKV_AAA_LORA = max(32, int(round(1.8 * D_MODEL ** 0.5 / 32)) * 32)
RWKV_GATE_LORA = max(32, int(round(0.6 * D_MODEL ** 0.8 / 32)) * 32)
RWKV_LNX_EPS = 64e-5
XATTN_HEADS = 4
XATTN_DIM = D_MODEL // XATTN_HEADS
MLP_HIDDEN = 4 * D_MODEL
N_EVEN = (DEPTH + 1) // 2
N_ODD = DEPTH // 2

kernel_name = 'hybrid_mlstm_hgrn2_rwkv7_memxattn'


def rms_norm(x, g, eps=NORM_EPS):
    xf = x.astype(jnp.float32)
    y = xf * lax.rsqrt(jnp.mean(xf * xf, axis=-1, keepdims=True) + eps)
    return (y * g.astype(jnp.float32)).astype(x.dtype)


def soft_cap(x, cap=GATE_CAP):
    return cap * jnp.tanh(x / cap)


def _to_chunks(a, L):
    B, T = a.shape[:2]
    a = a.reshape((B, T // L, L) + a.shape[2:])
    perm = (1, 0, 3, 2) + tuple(range(4, a.ndim))
    return a.transpose(perm)


def _from_chunks(a):
    NC, B, H, L = a.shape[:4]
    perm = (1, 0, 3, 2) + tuple(range(4, a.ndim))
    return a.transpose(perm).reshape((B, NC * L, H) + a.shape[4:])


def mlstm_chunkwise(q, k, v, i_pre, f_pre):
    B, T, H, DK = q.shape
    DV = v.shape[-1]
    L = MLSTM_CHUNK
    causal = jnp.tril(jnp.ones((L, L), dtype=bool))
    q = q * (DK ** -0.5)
    log_f = jax.nn.log_sigmoid(f_pre)
    xs = (_to_chunks(q, L), _to_chunks(k, L), _to_chunks(v, L), _to_chunks(i_pre, L), _to_chunks(log_f, L))

    def step(carry, inp):
        C, n, m = carry
        qc, kc, vc, lic, lfc = inp
        b = jnp.cumsum(lfc, axis=-1)
        log_d = b[..., :, None] - b[..., None, :] + lic[..., None, :]
        log_d = jnp.where(causal, log_d, -jnp.inf)
        inter = b + m[..., None]
        m_t = jnp.maximum(inter, jnp.max(log_d, axis=-1))
        s = jnp.einsum('bhtd,bhsd->bhts', qc, kc) * jnp.exp(log_d - m_t[..., None])
        g_inter = jnp.exp(inter - m_t)
        num = jnp.einsum('bhts,bhsv->bhtv', s, vc) + g_inter[..., None] * jnp.einsum('bhtd,bhdv->bhtv', qc, C)
        den = jnp.sum(s, axis=-1) + g_inter * jnp.einsum('bhtd,bhd->bht', qc, n)
        h = num / jnp.maximum(jnp.abs(den), jnp.exp(-m_t))[..., None]
        b_last = b[..., -1]
        log_w = b_last[..., None] - b + lic
        m_new = jnp.maximum(b_last + m, jnp.max(log_w, axis=-1))
        w = jnp.exp(log_w - m_new[..., None])
        decay = jnp.exp(b_last + m - m_new)
        kw = kc * w[..., None]
        C = decay[..., None, None] * C + jnp.einsum('bhsd,bhsv->bhdv', kw, vc)
        n = decay[..., None] * n + jnp.sum(kw, axis=2)
        return (C, n, m_new), h

    init = (jnp.zeros((B, H, DK, DV), jnp.float32), jnp.zeros((B, H, DK), jnp.float32), jnp.zeros((B, H), jnp.float32))
    _, h = lax.scan(step, init, xs)
    return _from_chunks(h)


def hgrn2_chunkwise(q, k, v, log_f):
    B, T, H, DK = q.shape
    DV = v.shape[-1]
    L = HGRN_CHUNK
    causal = jnp.tril(jnp.ones((L, L), dtype=bool))[:, :, None]
    xs = (_to_chunks(q, L), _to_chunks(k, L), _to_chunks(v, L), _to_chunks(log_f, L))

    def step(S, inp):
        qc, kc, vc, lfc = inp
        A = jnp.cumsum(lfc, axis=2)
        rel = A[:, :, :, None, :] - A[:, :, None, :, :]
        rel = jnp.exp(jnp.where(causal, rel, -jnp.inf))
        s = jnp.einsum('bhtd,bhtsd,bhsd->bhts', qc, rel, kc)
        o = jnp.einsum('bhts,bhsv->bhtv', s, vc) + jnp.einsum('bhtd,bhdv->bhtv', qc * jnp.exp(A), S)
        A_last = A[:, :, -1]
        S = jnp.exp(A_last)[..., None] * S + jnp.einsum('bhsd,bhsv->bhdv', kc * jnp.exp(A_last[:, :, None] - A), vc)
        return S, o

    _, o = lax.scan(step, jnp.zeros((B, H, DK, DV), jnp.float32), xs)
    return _from_chunks(o)


def mlstm_hgrn2_mixer(h, w_in, b_i, b_f, mlstm_g, lb, hgrn_g, w_out):
    B, T, _ = h.shape
    f32 = jnp.float32
    proj = (h @ w_in).astype(f32)
    bounds = [sum(IN_SIZES[:j + 1]) for j in range(len(IN_SIZES) - 1)]
    q_a, k_a, v_a, o_a, i_a, f_a, q_b, f_b, i_b, g_b = jnp.split(proj, bounds, axis=-1)
    q = q_a.reshape(B, T, MLSTM_HEADS, MLSTM_DK)
    k = k_a.reshape(B, T, MLSTM_HEADS, MLSTM_DK)
    v = v_a.reshape(B, T, MLSTM_HEADS, MLSTM_DV)
    i_pre = soft_cap(i_a + b_i.astype(f32))
    f_pre = soft_cap(f_a + b_f.astype(f32))
    h_a = mlstm_chunkwise(q, k, v, i_pre, f_pre)
    h_a = rms_norm(h_a, mlstm_g.reshape(MLSTM_HEADS, MLSTM_DV)) * jax.nn.sigmoid(o_a).reshape(B, T, MLSTM_HEADS, MLSTM_DV)
    lb = lb.astype(f32).reshape(HGRN_HEADS, HGRN_DK)
    f = lb + (1.0 - lb) * jax.nn.sigmoid(f_b.reshape(B, T, HGRN_HEADS, HGRN_DK))
    qh = jax.nn.silu(q_b).reshape(B, T, HGRN_HEADS, HGRN_DK)
    h_b = hgrn2_chunkwise(qh, 1.0 - f, i_b.reshape(B, T, HGRN_HEADS, HGRN_DV), jnp.log(f))
    h_b = rms_norm(h_b, hgrn_g.reshape(HGRN_HEADS, HGRN_DV)) * jax.nn.silu(g_b).reshape(B, T, HGRN_HEADS, HGRN_DV)
    y = jnp.concatenate([h_a.reshape(B, T, MIX_A), h_b.reshape(B, T, MIX_B)], axis=-1)
    return y.astype(h.dtype) @ w_out


def rwkv7_scan(r, decay, k, v, kk, kka):
    B, T, H, N = r.shape

    def step(S, inp):
        r_t, w_t, k_t, v_t, kk_t, b_t = inp
        sa = jnp.einsum('bhvk,bhk->bhv', S, -kk_t)
        S = S * w_t[:, :, None, :] + sa[..., None] * b_t[:, :, None, :] + v_t[..., None] * k_t[:, :, None, :]
        return S, jnp.einsum('bhvk,bhk->bhv', S, r_t)

    xs = tuple(a.transpose(1, 0, 2, 3) for a in (r, decay, k, v, kk, kka))
    _, y = lax.scan(step, jnp.zeros((B, H, N, N), jnp.float32), xs)
    return y.transpose(1, 0, 2, 3)


def rwkv7_time_mix(h, mu, w0, w1, w2, a0, a1, a2, g1, g2, k_k, k_a, r_k, w_r, w_k, w_v, w_o, lnx_w, lnx_b):
    B, T, D = h.shape
    H, N = RWKV_HEADS, RWKV_HEAD
    f32 = jnp.float32
    x_prev = jnp.concatenate([jnp.zeros_like(h[:, :1]), h[:, :-1]], axis=1)
    xx = x_prev - h
    shift = lambda j: h + xx * mu[j]
    r = (shift(0) @ w_r).astype(f32)
    w_log = -jax.nn.softplus(-(w0 + jnp.tanh(shift(1) @ w1) @ w2).astype(f32)) - 0.5
    k = (shift(2) @ w_k).astype(f32)
    v = (shift(3) @ w_v).astype(f32)
    a = jax.nn.sigmoid((a0 + (shift(4) @ a1) @ a2).astype(f32))
    g = (jax.nn.sigmoid(shift(5) @ g1) @ g2).astype(f32)
    heads = lambda t: t.reshape(B, T, H, N)
    kk = heads(k * k_k.astype(f32))
    kk = kk / jnp.maximum(jnp.sqrt(jnp.sum(kk * kk, axis=-1, keepdims=True)), 1e-12)
    k = heads(k * (1.0 + (a - 1.0) * k_a.astype(f32)))
    r, v, a = heads(r), heads(v), heads(a)
    decay = heads(jnp.exp(-jnp.exp(w_log)))
    y = rwkv7_scan(r, decay, k, v, kk, kk * a)
    mean = jnp.mean(y, axis=-1, keepdims=True)
    var = jnp.mean(jnp.square(y - mean), axis=-1, keepdims=True)
    y = (y - mean) * lax.rsqrt(var + RWKV_LNX_EPS)
    y = y * lnx_w.astype(f32).reshape(H, N) + lnx_b.astype(f32).reshape(H, N)
    y = y + jnp.sum(r * k * r_k.astype(f32), axis=-1, keepdims=True) * v
    y = y.reshape(B, T, D) * g
    return y.astype(h.dtype) @ w_o


def mem_cross_attention(h, mem_k, mem_v, w_q, w_o):
    B, T, D = h.shape
    q = (h @ w_q).reshape(B, T, XATTN_HEADS, XATTN_DIM)
    s = jnp.einsum('bthd,bmhd->bhtm', q, mem_k).astype(jnp.float32) * (XATTN_DIM ** -0.5)
    p = jax.nn.softmax(s, axis=-1).astype(h.dtype)
    o = jnp.einsum('bhtm,bmhd->bthd', p, mem_v).reshape(B, T, D)
    return o @ w_o


def sq_relu_mlp(h, w_up, w_down):
    return jnp.square(jax.nn.relu(h @ w_up)) @ w_down


def setup_inputs(seed: int = 0) -> dict:
    key = jax.random.key(seed)
    ks = iter(jax.random.split(key, 64))
    f32 = jnp.float32
    D = D_MODEL

    def nrm(shape, fan_in, scale=1.0):
        return jax.random.normal(next(ks), shape, f32) * (scale * fan_in ** -0.5)

    def gain(shape):
        return 1.0 + 0.02 * jax.random.normal(next(ks), shape, f32)

    def small(shape, s=0.01, center=0.0):
        return center + s * jax.random.normal(next(ks), shape, f32)

    return {
        'x': jax.random.normal(next(ks), (BATCH, SEQ, D), f32),
        'mem': jax.random.normal(next(ks), (BATCH, N_MEM, D), f32),
        'norm_mix_g': gain((DEPTH, D)),
        'norm_xattn_g': gain((DEPTH, D)),
        'norm_mlp_g': gain((DEPTH, D)),
        'final_norm_g': gain((D,)),
        'mem_norm_g': gain((D,)),
        'ab_w_in': nrm((N_EVEN, D, IN_COLS), D),
        'mlstm_b_i': small((N_EVEN, MLSTM_HEADS), 0.1),
        'mlstm_b_f': small((N_EVEN, MLSTM_HEADS), 0.5, 3.0),
        'mlstm_norm_g': gain((N_EVEN, MIX_A)),
        'hgrn_lb_logits': small((N_EVEN + 1, MIX_B), 0.5),
        'hgrn_norm_g': gain((N_EVEN, MIX_B)),
        'ab_w_out': nrm((N_EVEN, D, D), D),
        'rwkv_mu': jax.random.uniform(next(ks), (N_ODD, 6, D), f32),
        'rwkv_w0': small((N_ODD, D), 0.5),
        'rwkv_w1': nrm((N_ODD, D, RWKV_DECAY_LORA), D),
        'rwkv_w2': nrm((N_ODD, RWKV_DECAY_LORA, D), RWKV_DECAY_LORA, 0.1),
        'rwkv_a0': small((N_ODD, D), 0.1),
        'rwkv_a1': nrm((N_ODD, D, RWKV_AAA_LORA), D),
        'rwkv_a2': nrm((N_ODD, RWKV_AAA_LORA, D), RWKV_AAA_LORA, 0.1),
        'rwkv_g1': nrm((N_ODD, D, RWKV_GATE_LORA), D),
        'rwkv_g2': nrm((N_ODD, RWKV_GATE_LORA, D), RWKV_GATE_LORA),
        'rwkv_k_k': small((N_ODD, D), 0.02, 0.85),
        'rwkv_k_a': gain((N_ODD, D)),
        'rwkv_r_k': small((N_ODD, RWKV_HEADS, RWKV_HEAD), 0.1),
        'rwkv_w_r': nrm((N_ODD, D, D), D),
        'rwkv_w_k': nrm((N_ODD, D, D), D),
        'rwkv_w_v': nrm((N_ODD, D, D), D),
        'rwkv_w_o': nrm((N_ODD, D, D), D),
        'rwkv_lnx_w': gain((N_ODD, D)),
        'rwkv_lnx_b': small((N_ODD, D), 0.01),
        'xattn_w_q': nrm((DEPTH, D, D), D),
        'xattn_w_o': nrm((DEPTH, D, D), D),
        'mem_w_kv': nrm((D, 2 * D), D),
        'mlp_w_up': nrm((DEPTH, D, MLP_HIDDEN), D),
        'mlp_w_down': nrm((DEPTH, MLP_HIDDEN, D), MLP_HIDDEN),
    }


def reference(x, mem, norm_mix_g, norm_xattn_g, norm_mlp_g, final_norm_g, mem_norm_g,
              ab_w_in, mlstm_b_i, mlstm_b_f, mlstm_norm_g, hgrn_lb_logits, hgrn_norm_g, ab_w_out,
              rwkv_mu, rwkv_w0, rwkv_w1, rwkv_w2, rwkv_a0, rwkv_a1, rwkv_a2, rwkv_g1, rwkv_g2,
              rwkv_k_k, rwkv_k_a, rwkv_r_k, rwkv_w_r, rwkv_w_k, rwkv_w_v, rwkv_w_o, rwkv_lnx_w, rwkv_lnx_b,
              xattn_w_q, xattn_w_o, mem_w_kv, mlp_w_up, mlp_w_down):
    B, M, D = mem.shape
    mem_kv = rms_norm(mem, mem_norm_g) @ mem_w_kv
    mem_k = mem_kv[..., :D].reshape(B, M, XATTN_HEADS, XATTN_DIM)
    mem_v = mem_kv[..., D:].reshape(B, M, XATTN_HEADS, XATTN_DIM)
    lb_all = jnp.cumsum(jax.nn.softmax(hgrn_lb_logits.astype(jnp.float32), axis=0), axis=0)
    h = x
    for layer in range(DEPTH):
        j = layer // 2
        hn = rms_norm(h, norm_mix_g[layer])
        if layer % 2 == 0:
            mix = mlstm_hgrn2_mixer(hn, ab_w_in[j], mlstm_b_i[j], mlstm_b_f[j], mlstm_norm_g[j],
                                    lb_all[j], hgrn_norm_g[j], ab_w_out[j])
        else:
            mix = rwkv7_time_mix(hn, rwkv_mu[j], rwkv_w0[j], rwkv_w1[j], rwkv_w2[j], rwkv_a0[j], rwkv_a1[j],
                                 rwkv_a2[j], rwkv_g1[j], rwkv_g2[j], rwkv_k_k[j], rwkv_k_a[j], rwkv_r_k[j],
                                 rwkv_w_r[j], rwkv_w_k[j], rwkv_w_v[j], rwkv_w_o[j], rwkv_lnx_w[j], rwkv_lnx_b[j])
        h = h + mix
        h = h + mem_cross_attention(rms_norm(h, norm_xattn_g[layer]), mem_k, mem_v, xattn_w_q[layer], xattn_w_o[layer])
        h = h + sq_relu_mlp(rms_norm(h, norm_mlp_g[layer]), mlp_w_up[layer], mlp_w_down[layer])
    return rms_norm(h, final_norm_g)
```

```python
import functools

import jax
import jax.numpy as jnp
from jax import lax
from jax.experimental import pallas as pl
from jax.experimental.pallas import tpu as pltpu

F32 = jnp.float32
BF16 = jnp.bfloat16

NORM_EPS = 1e-6
GATE_CAP = 15.0
MLSTM_HEADS = 4
MLSTM_CHUNK = 64
HGRN_HEAD_DIM = 128
HGRN_CHUNK = 16
RWKV_HEAD = 64
RWKV_CHUNK = 64
RWKV_LNX_EPS = 64e-5
XATTN_HEADS = 4
LANES = 128
VMEM_LIMIT = 56 * 1024 * 1024


def _cparams(sem):
    return pltpu.CompilerParams(dimension_semantics=sem, vmem_limit_bytes=VMEM_LIMIT)


def _sigmoid(x):
    return 1.0 / (1.0 + jnp.exp(-x))


def _log_sigmoid(x):
    return jnp.minimum(x, 0.0) - jnp.log1p(jnp.exp(-jnp.abs(x)))


def _softplus(x):
    return jnp.maximum(x, 0.0) + jnp.log1p(jnp.exp(-jnp.abs(x)))


def _mm_kernel(*refs, nk, act, has_res):
    if has_res:
        a_ref, b_ref, res_ref, o_ref = refs[:4]
        rest = refs[4:]
    else:
        a_ref, b_ref, o_ref = refs[:3]
        res_ref = None
        rest = refs[3:]

    def finish(r):
        if act == "relu2":
            r = jnp.square(jnp.maximum(r, 0.0))
        elif act == "tanh":
            r = jnp.tanh(r)
        elif act == "sigmoid":
            r = _sigmoid(r)
        if has_res:
            r = r + res_ref[...].astype(F32)
        o_ref[...] = r.astype(o_ref.dtype)

    part = jnp.dot(a_ref[...].astype(BF16), b_ref[...].astype(BF16), preferred_element_type=F32)
    if nk == 1:
        finish(part)
    else:
        acc_ref = rest[0]
        k = pl.program_id(2)

        @pl.when(k == 0)
        def _():
            acc_ref[...] = part

        @pl.when(k > 0)
        def _():
            acc_ref[...] += part

        @pl.when(k == nk - 1)
        def _():
            finish(acc_ref[...])


def _pick(dim, pref):
    t = min(dim, pref)
    while dim % t:
        t //= 2
    return t


def _mm(a, b, *, out_dtype=F32, act=None, res=None, tm=1024, tn=1024, tk=1024, name="mm"):
    M, K = a.shape
    K2, N = b.shape
    assert K == K2
    tm, tn, tk = _pick(M, tm), _pick(N, tn), _pick(K, tk)
    nk = K // tk
    in_specs = [pl.BlockSpec((tm, tk), lambda i, j, k: (i, k)),
                pl.BlockSpec((tk, tn), lambda i, j, k: (k, j))]
    args = [a, b]
    if res is not None:
        in_specs.append(pl.BlockSpec((tm, tn), lambda i, j, k: (i, j)))
        args.append(res)
    scratch = [] if nk == 1 else [pltpu.VMEM((tm, tn), F32)]
    return pl.pallas_call(
        functools.partial(_mm_kernel, nk=nk, act=act, has_res=res is not None),
        grid=(M // tm, N // tn, nk),
        in_specs=in_specs,
        out_specs=pl.BlockSpec((tm, tn), lambda i, j, k: (i, j)),
        out_shape=jax.ShapeDtypeStruct((M, N), out_dtype),
        scratch_shapes=scratch,
        compiler_params=_cparams(("parallel", "parallel", "arbitrary")),
        name=name,
    )(*args)


def _rmsnorm_kernel(x_ref, g_ref, o_ref):
    x = x_ref[...].astype(F32)
    y = x * lax.rsqrt(jnp.mean(x * x, axis=-1, keepdims=True) + NORM_EPS)
    o_ref[...] = (y * g_ref[...]).astype(o_ref.dtype)


def _rmsnorm(x, g, out_dtype, tm=256):
    M, D = x.shape
    tm = _pick(M, tm)
    return pl.pallas_call(
        _rmsnorm_kernel,
        grid=(M // tm,),
        in_specs=[pl.BlockSpec((tm, D), lambda i: (i, 0)),
                  pl.BlockSpec((1, D), lambda i: (0, 0))],
        out_specs=pl.BlockSpec((tm, D), lambda i: (i, 0)),
        out_shape=jax.ShapeDtypeStruct((M, D), out_dtype),
        compiler_params=_cparams(("parallel",)),
        name="rmsnorm",
    )(x, g.reshape(1, D).astype(F32))


def _mlstm_kernel(q_ref, k_ref, v_ref, og_ref, gates_ref, bias_ref, ng_ref, o_ref,
                  c_ref, n_ref, m_ref, *, heads):
    hd = pl.program_id(0)
    L = q_ref.shape[0]
    dk = q_ref.shape[1]

    @pl.when(pl.program_id(1) == 0)
    def _():
        c_ref[...] = jnp.zeros_like(c_ref)
        n_ref[...] = jnp.zeros_like(n_ref)
        m_ref[...] = jnp.zeros_like(m_ref)

    pre = gates_ref[...] + bias_ref[...]
    capped = GATE_CAP * jnp.tanh(pre / GATE_CAP)
    lane = lax.broadcasted_iota(jnp.int32, capped.shape, 1)
    li_col = jnp.sum(jnp.where(lane == hd, capped, 0.0), axis=1, keepdims=True)
    lf_col = jnp.sum(jnp.where(lane == hd + heads, _log_sigmoid(capped), 0.0), axis=1, keepdims=True)

    row = lax.broadcasted_iota(jnp.int32, (L, L), 0)
    col = lax.broadcasted_iota(jnp.int32, (L, L), 1)
    eye = row == col
    lower = col <= row
    lf_row = jnp.sum(jnp.where(eye, lf_col, 0.0), axis=0, keepdims=True)
    li_row = jnp.sum(jnp.where(eye, li_col, 0.0), axis=0, keepdims=True)
    b_col = jnp.sum(jnp.where(lower, lf_row, 0.0), axis=1, keepdims=True)
    b_row = jnp.sum(jnp.where(row <= col, lf_col, 0.0), axis=0, keepdims=True)
    g_row = li_row - b_row
    g_col = li_col - b_col
    m_prev = m_ref[:, 0:1]
    a_col = jnp.maximum(m_prev, jnp.max(jnp.where(lower, g_row, -jnp.inf), axis=1, keepdims=True))
    a_last = jnp.max(a_col, axis=0, keepdims=True)
    b_last = jnp.sum(lf_col, axis=0, keepdims=True)

    q = q_ref[...] * (dk ** -0.5)
    k = k_ref[...]
    v = v_ref[...].astype(BF16)
    qb = q.astype(BF16)
    s = lax.dot_general(qb, k.astype(BF16), (((1,), (1,)), ((), ())), preferred_element_type=F32)
    s = s * jnp.where(lower, jnp.exp(g_row - a_col), 0.0)
    g_inter = jnp.exp(m_prev - a_col)
    qc = jnp.dot(qb, c_ref[...].astype(BF16), preferred_element_type=F32)
    num = jnp.dot(s.astype(BF16), v, preferred_element_type=F32) + g_inter * qc
    qn = jnp.sum(q * n_ref[...], axis=1, keepdims=True)
    den = jnp.sum(s, axis=1, keepdims=True) + g_inter * qn
    hh = num / jnp.maximum(jnp.abs(den), jnp.exp(-(b_col + a_col)))

    w_col = jnp.exp(g_col - a_last)
    decay = jnp.exp(m_prev - a_last)
    kw = k * w_col
    c_ref[...] = decay * c_ref[...] + lax.dot_general(
        kw.astype(BF16), v, (((0,), (0,)), ((), ())), preferred_element_type=F32)
    n_ref[...] = decay * n_ref[...] + jnp.sum(kw, axis=0, keepdims=True)
    m_ref[...] = jnp.broadcast_to(b_last + a_last, m_ref.shape)

    y = hh * lax.rsqrt(jnp.mean(hh * hh, axis=-1, keepdims=True) + NORM_EPS)
    o_ref[...] = (y * ng_ref[...] * _sigmoid(og_ref[...])).astype(o_ref.dtype)


def _mlstm(proj, gates, bias_row, norm_g, dk, dv):
    T = proj.shape[0]
    H = MLSTM_HEADS
    L = MLSTM_CHUNK
    v0 = (2 * H * dk) // dv
    return pl.pallas_call(
        functools.partial(_mlstm_kernel, heads=H),
        grid=(H, T // L),
        in_specs=[pl.BlockSpec((L, dk), lambda h, c: (c, h)),
                  pl.BlockSpec((L, dk), lambda h, c: (c, H + h)),
                  pl.BlockSpec((L, dv), lambda h, c: (c, v0 + h)),
                  pl.BlockSpec((L, dv), lambda h, c: (c, v0 + H + h)),
                  pl.BlockSpec((L, LANES), lambda h, c: (c, 0)),
                  pl.BlockSpec((1, LANES), lambda h, c: (0, 0)),
                  pl.BlockSpec((1, dv), lambda h, c: (0, h))],
        out_specs=pl.BlockSpec((L, dv), lambda h, c: (c, h)),
        out_shape=jax.ShapeDtypeStruct((T, H * dv), BF16),
        scratch_shapes=[pltpu.VMEM((dk, dv), F32), pltpu.VMEM((1, dk), F32), pltpu.VMEM((1, LANES), F32)],
        compiler_params=_cparams(("parallel", "arbitrary")),
        name="mlstm",
    )(proj, proj, proj, proj, gates, bias_row, norm_g.reshape(1, H * dv).astype(F32))


def _hgrn_kernel(q_ref, f_ref, i_ref, g_ref, lb_ref, ng_ref, o_ref, st_ref, *, lb_row, chunk):
    TB, dk = q_ref.shape
    C = chunk

    @pl.when(pl.program_id(1) == 0)
    def _():
        st_ref[...] = jnp.zeros_like(st_ref)

    logits = lb_ref[...].astype(F32)
    e = jnp.exp(logits - jnp.max(logits, axis=0, keepdims=True))
    p = e / jnp.sum(e, axis=0, keepdims=True)
    lb = jnp.sum(p[0:lb_row + 1], axis=0, keepdims=True)
    ng = ng_ref[...]

    srow = lax.broadcasted_iota(jnp.int32, (C, dk), 0)

    def body(ci, carry):
        r0 = pl.multiple_of(ci * C, C)
        qb = q_ref[pl.ds(r0, C), :]
        fb = f_ref[pl.ds(r0, C), :]
        vb = i_ref[pl.ds(r0, C), :]
        gb = g_ref[pl.ds(r0, C), :]
        f = lb + (1.0 - lb) * _sigmoid(fb)
        kk = 1.0 - f
        lf = jnp.log(f)
        qh = qb * _sigmoid(qb)
        A = lf
        sh = 1
        while sh < C:
            A = A + jnp.where(srow >= sh, pltpu.roll(A, sh, axis=0), 0.0)
            sh *= 2
        st = st_ref[...]
        qe = (qh * jnp.exp(A)).astype(BF16)
        o_inter = lax.dot_general(qe, st.astype(BF16), (((1,), (1,)), ((), ())),
                                  preferred_element_type=F32)
        rows = []
        for t in range(C):
            rel = jnp.where(srow <= t, jnp.exp(A[t:t + 1] - A), 0.0)
            sc = jnp.sum(rel * kk * qh[t:t + 1], axis=1, keepdims=True)
            rows.append(jnp.sum(sc * vb, axis=0, keepdims=True))
        o = o_inter + jnp.concatenate(rows, axis=0)
        a_last = A[C - 1:C]
        ke = (kk * jnp.exp(a_last - A)).astype(BF16)
        st_ref[...] = st * jnp.exp(a_last) + lax.dot_general(
            vb.astype(BF16), ke, (((0,), (0,)), ((), ())), preferred_element_type=F32)
        y = o * lax.rsqrt(jnp.mean(o * o, axis=-1, keepdims=True) + NORM_EPS)
        o_ref[pl.ds(r0, C), :] = (y * ng * (gb * _sigmoid(gb))).astype(o_ref.dtype)
        return carry

    lax.fori_loop(0, TB // C, body, 0)


def _hgrn(proj, lb_logits, lb_row, norm_g, tb=256):
    T = proj.shape[0]
    W = proj.shape[1] // 4
    d = HGRN_HEAD_DIM
    H = W // d
    tb = _pick(T, tb)
    R = lb_logits.shape[0]
    return pl.pallas_call(
        functools.partial(_hgrn_kernel, lb_row=lb_row, chunk=HGRN_CHUNK),
        grid=(H, T // tb),
        in_specs=[pl.BlockSpec((tb, d), lambda h, c: (c, h)),
                  pl.BlockSpec((tb, d), lambda h, c: (c, H + h)),
                  pl.BlockSpec((tb, d), lambda h, c: (c, 2 * H + h)),
                  pl.BlockSpec((tb, d), lambda h, c: (c, 3 * H + h)),
                  pl.BlockSpec((R, d), lambda h, c: (0, h)),
                  pl.BlockSpec((1, d), lambda h, c: (0, h))],
        out_specs=pl.BlockSpec((tb, d), lambda h, c: (c, h)),
        out_shape=jax.ShapeDtypeStruct((T, W), BF16),
        scratch_shapes=[pltpu.VMEM((d, d), F32)],
        compiler_params=_cparams(("parallel", "arbitrary")),
        name="hgrn2",
    )(proj, proj, proj, proj, lb_logits.astype(F32), norm_g.reshape(1, W).astype(F32))


def _xattn_kernel(q_ref, k_ref, v_ref, o_ref):
    q = q_ref[...]
    scale = q.shape[1] ** -0.5
    s = lax.dot_general(q.astype(BF16), k_ref[...].astype(BF16), (((1,), (1,)), ((), ())),
                        preferred_element_type=F32) * scale
    e = jnp.exp(s - jnp.max(s, axis=-1, keepdims=True))
    p = e / jnp.sum(e, axis=-1, keepdims=True)
    o_ref[...] = jnp.dot(p.astype(BF16), v_ref[...].astype(BF16),
                         preferred_element_type=F32).astype(o_ref.dtype)


def _xattn(q, mem_kv, tm=512):
    T, D = q.shape
    M = mem_kv.shape[0]
    H = XATTN_HEADS
    d = D // H
    tm = _pick(T, tm)
    return pl.pallas_call(
        _xattn_kernel,
        grid=(T // tm, H),
        in_specs=[pl.BlockSpec((tm, d), lambda i, h: (i, h)),
                  pl.BlockSpec((M, d), lambda i, h: (0, h)),
                  pl.BlockSpec((M, d), lambda i, h: (0, H + h))],
        out_specs=pl.BlockSpec((tm, d), lambda i, h: (i, h)),
        out_shape=jax.ShapeDtypeStruct((T, D), BF16),
        compiler_params=_cparams(("parallel", "parallel")),
        name="xattn",
    )(q, mem_kv, mem_kv)


def _shift_kernel(x_ref, prev_ref, mu_ref, *o_refs):
    x = x_ref[...]
    n = prev_ref.shape[0]
    last = jnp.where(pl.program_id(0) == 0, 0.0, prev_ref[n - 1:n, :])
    row = lax.broadcasted_iota(jnp.int32, x.shape, 0)
    xx = jnp.where(row == 0, last, pltpu.roll(x, 1, axis=0)) - x
    for j, o_ref in enumerate(o_refs):
        o_ref[...] = (x + xx * mu_ref[j:j + 1, :]).astype(o_ref.dtype)


def _rwkv_shift(x, mu, tm=256):
    T, D = x.shape
    J = mu.shape[0]
    tm = _pick(T, tm)
    pr = 8
    spec = pl.BlockSpec((tm, D), lambda i: (i, 0))
    return pl.pallas_call(
        _shift_kernel,
        grid=(T // tm,),
        in_specs=[spec,
                  pl.BlockSpec((pr, D), lambda i: (jnp.maximum(i * (tm // pr) - 1, 0), 0)),
                  pl.BlockSpec((J, D), lambda i: (0, 0))],
        out_specs=[spec] * J,
        out_shape=[jax.ShapeDtypeStruct((T, D), BF16)] * J,
        compiler_params=_cparams(("parallel",)),
        name="rwkv_shift",
    )(x, x, mu.astype(F32))


def _split3(x):
    x1 = x.astype(BF16)
    r1 = x - x1.astype(F32)
    x2 = r1.astype(BF16)
    x3 = (r1 - x2.astype(F32)).astype(BF16)
    return x1, x2, x3


def _dot_exact_rhs(x, m):
    x1, x2, x3 = _split3(x)
    d = functools.partial(jnp.dot, preferred_element_type=F32)
    return d(x1, m) + d(x2, m) + d(x3, m)


def _head_ones(n, head, scale, dtype):
    r = lax.broadcasted_iota(jnp.int32, (n, n), 0) // head
    c = lax.broadcasted_iota(jnp.int32, (n, n), 1) // head
    return jnp.where(r == c, scale, 0.0).astype(dtype)


def _rwkv_prep_kernel(r_ref, k_ref, v_ref, lw_ref, la_ref, par_ref,
                      at_ref, rt_ref, bk_ref, tc_ref, rb_ref, akv_ref, rkv_ref, bon_ref, gl_ref, *, chunk):
    TB, W = r_ref.shape
    L = chunk
    N = RWKV_HEAD
    par = par_ref[...]
    w0, a0, k_k, k_a, r_k = (par[i:i + 1] for i in range(5))
    seg1 = _head_ones(W, N, 1.0, BF16)
    lane = lax.broadcasted_iota(jnp.int32, (L, W), 1)
    rowi = lax.broadcasted_iota(jnp.int32, (L, W), 0)
    head_a = lane < N
    srel = jnp.where(head_a, lane, lane - N)
    strict = srel < rowi
    incl = srel <= rowi
    rr = lax.broadcasted_iota(jnp.int32, (2 * L, 2 * L), 0)
    cc = lax.broadcasted_iota(jnp.int32, (2 * L, 2 * L), 1)
    eye = jnp.where(rr == cc, 1.0, 0.0)
    dnt = (((1,), (1,)), ((), ()))

    def body(ci, carry):
        r0 = pl.multiple_of(ci * L, L)
        sl = pl.ds(r0, L)
        r = r_ref[sl, :]
        k = k_ref[sl, :]
        v = v_ref[sl, :]
        w_log = -_softplus(-(w0 + lw_ref[sl, :])) - 0.5
        logw = -jnp.exp(w_log)
        a_sig = _sigmoid(a0 + la_ref[sl, :])
        kkr = k * k_k
        ss = _dot_exact_rhs(kkr * kkr, seg1)
        kk = kkr / jnp.maximum(jnp.sqrt(ss), 1e-12)
        k2 = k * (1.0 + (a_sig - 1.0) * k_a)
        bon_ref[sl, :] = _dot_exact_rhs(r * k2 * r_k, seg1) * v
        cs = logw
        sh = 1
        while sh < L:
            cs = cs + jnp.where(rowi >= sh, pltpu.roll(cs, sh, axis=0), 0.0)
            sh *= 2
        ginv = jnp.exp(-cs)
        at = -kk * jnp.exp(cs - logw)
        rt = r * jnp.exp(cs)
        bt = (kk * a_sig * ginv).astype(BF16)
        kt = (k2 * ginv).astype(BF16)
        gl_ref[ci] = jnp.exp(cs[L - 1:L])
        atb = at.astype(BF16)
        rtb = rt.astype(BF16)
        at_ref[sl, :] = atb
        rt_ref[sl, :] = rtb
        bk_ref[pl.ds(pl.multiple_of(2 * r0, 2 * L), L), :] = bt
        bk_ref[pl.ds(pl.multiple_of(2 * r0 + L, L), L), :] = kt
        zero = jnp.zeros_like(atb)
        lhs_a = jnp.concatenate([jnp.where(head_a, atb, zero), jnp.where(head_a, rtb, zero)], axis=0)
        lhs_b = jnp.concatenate([jnp.where(head_a, zero, atb), jnp.where(head_a, zero, rtb)], axis=0)
        pa = lax.dot_general(lhs_a, jnp.concatenate([bt, kt], axis=0), dnt, preferred_element_type=F32)
        pb = lax.dot_general(lhs_b, jnp.concatenate([kt, bt], axis=0), dnt, preferred_element_type=F32)
        nbd = jnp.concatenate([jnp.where(head_a & strict, pa[:L], 0.0),
                               jnp.where((~head_a) & strict, pb[:L], 0.0)], axis=0)
        pw = nbd
        tinv = eye + nbd
        steps = 1
        while steps < L // 2:
            pwb = pw.astype(BF16)
            pw = jnp.dot(pwb, pwb, preferred_element_type=F32)
            tinv = tinv + jnp.dot(tinv.astype(BF16), pw.astype(BF16), preferred_element_type=F32)
            steps *= 2
        tc_ref[sl, :] = (tinv[:L] + tinv[L:]).astype(BF16)
        rb_ref[sl, :] = jnp.where(incl, jnp.where(head_a, pa[L:], pb[L:]), 0.0).astype(BF16)
        ak_sw = jnp.where(strict, jnp.where(head_a, pb[:L], pa[:L]), 0.0)
        rk_sw = jnp.where(incl, jnp.where(head_a, pb[L:], pa[L:]), 0.0)
        vb = v.astype(BF16)
        zv = jnp.zeros_like(vb)
        v_sw = jnp.concatenate([jnp.where(head_a, zv, vb), jnp.where(head_a, vb, zv)], axis=0)
        kv = jnp.dot(jnp.concatenate([ak_sw, rk_sw], axis=0).astype(BF16), v_sw, preferred_element_type=F32)
        akv_ref[sl, :] = kv[:L]
        rkv_ref[sl, :] = kv[L:]
        return carry

    lax.fori_loop(0, TB // L, body, 0)


def _rwkv_state_kernel(at_ref, rt_ref, bk_ref, tc_ref, rb_ref, akv_ref, rkv_ref, bon_ref, gl_ref,
                       v_ref, g_ref, ln_ref, o_ref, ht_ref, *, chunk):
    TB, W = v_ref.shape
    L = chunk
    N = RWKV_HEAD

    @pl.when(pl.program_id(1) == 0)
    def _():
        ht_ref[...] = jnp.zeros_like(ht_ref)

    lane = lax.broadcasted_iota(jnp.int32, (L, W), 1)
    head_a = lane < N
    hr = lax.broadcasted_iota(jnp.int32, (W, W), 0) // N
    hc = lax.broadcasted_iota(jnp.int32, (W, W), 1) // N
    same_head = hr == hc
    seg_mean = _head_ones(W, N, 1.0 / N, BF16)
    lnw = ln_ref[0:1, :]
    lnb = ln_ref[1:2, :]
    dnt = (((1,), (1,)), ((), ()))
    dtn = (((0,), (0,)), ((), ()))

    def stack(x):
        z = jnp.zeros_like(x)
        return jnp.concatenate([jnp.where(head_a, x, z), jnp.where(head_a, z, x)], axis=0)

    def body(ci, carry):
        r0 = pl.multiple_of(ci * L, L)
        sl = pl.ds(r0, L)
        ht = ht_ref[...]
        ar = jnp.concatenate([at_ref[sl, :], rt_ref[sl, :]], axis=0)
        xr = lax.dot_general(ar, ht.astype(BF16), dnt, preferred_element_type=F32)
        x = xr[:L] + akv_ref[sl, :]
        u = jnp.dot(tc_ref[sl, :], stack(x.astype(BF16)), preferred_element_type=F32)
        ub = u.astype(BF16)
        y = xr[L:] + jnp.dot(rb_ref[sl, :], stack(ub), preferred_element_type=F32) + rkv_ref[sl, :]
        vb = v_ref[sl, :].astype(BF16)
        bk = bk_ref[pl.ds(pl.multiple_of(2 * r0, 2 * L), 2 * L), :]
        dht = lax.dot_general(jnp.concatenate([ub, vb], axis=0), bk, dtn, preferred_element_type=F32)
        ht_ref[...] = (ht + jnp.where(same_head, dht, 0.0)) * gl_ref[ci]
        mean = _dot_exact_rhs(y, seg_mean)
        yc = y - mean
        var = _dot_exact_rhs(yc * yc, seg_mean)
        yn = yc * lax.rsqrt(var + RWKV_LNX_EPS) * lnw + lnb
        o_ref[sl, :] = ((yn + bon_ref[sl, :]) * g_ref[sl, :]).astype(o_ref.dtype)
        return carry

    lax.fori_loop(0, TB // L, body, 0)


def _rwkv_core(r, k, v, lw, la, g, w0, a0, k_k, k_a, r_k, lnx_w, lnx_b, tb=512):
    T, D = r.shape
    W = 2 * RWKV_HEAD
    L = RWKV_CHUNK
    P = D // W
    tb = _pick(T, tb)
    nc = tb // L
    par = jnp.stack([w0, a0, k_k, k_a, r_k.reshape(D)]).astype(F32)
    par = jnp.concatenate([par, jnp.zeros((3, D), F32)], axis=0)
    spec = pl.BlockSpec((tb, W), lambda p, c: (c, p))
    spec2 = pl.BlockSpec((2 * tb, W), lambda p, c: (c, p))
    gspec = pl.BlockSpec((nc, 1, W), lambda p, c: (c, 0, p))
    bf = jax.ShapeDtypeStruct((T, D), BF16)
    f32 = jax.ShapeDtypeStruct((T, D), F32)
    at, rt, bk, tc, rb, akv, rkv, bon, gl = pl.pallas_call(
        functools.partial(_rwkv_prep_kernel, chunk=L),
        grid=(P, T // tb),
        in_specs=[spec] * 5 + [pl.BlockSpec((8, W), lambda p, c: (0, p))],
        out_specs=[spec, spec, spec2, spec, spec, spec, spec, spec, gspec],
        out_shape=[bf, bf, jax.ShapeDtypeStruct((2 * T, D), BF16), bf, bf, f32, f32, f32,
                   jax.ShapeDtypeStruct((T // L, 1, D), F32)],
        compiler_params=_cparams(("parallel", "parallel")),
        name="rwkv_prep",
    )(r, k, v, lw, la, par)
    ln = jnp.stack([lnx_w, lnx_b]).astype(F32)
    ln = jnp.concatenate([ln, jnp.zeros((6, D), F32)], axis=0)
    return pl.pallas_call(
        functools.partial(_rwkv_state_kernel, chunk=L),
        grid=(P, T // tb),
        in_specs=[spec, spec, spec2, spec, spec, spec, spec, spec, gspec, spec, spec,
                  pl.BlockSpec((8, W), lambda p, c: (0, p))],
        out_specs=spec,
        out_shape=bf,
        scratch_shapes=[pltpu.VMEM((W, W), F32)],
        compiler_params=_cparams(("parallel", "arbitrary")),
        name="rwkv_state",
    )(at, rt, bk, tc, rb, akv, rkv, bon, gl, v, g, ln)


def _mixer_even(h, hn, w_in, b_i, b_f, mlstm_g, lb_logits, lb_row, hgrn_g, w_out):
    D = h.shape[1]
    H = MLSTM_HEADS
    mix_a = D // 2
    dv = mix_a // H
    dk = dv // 2
    na = 2 * H * dk + 2 * mix_a
    w_a = w_in[:, :na].astype(BF16)
    w_if = jnp.zeros((D, LANES), BF16).at[:, :2 * H].set(w_in[:, na:na + 2 * H].astype(BF16))
    w_b = w_in[:, na + 2 * H:].astype(BF16)
    proj_a = _mm(hn, w_a)
    gates = _mm(hn, w_if)
    proj_b = _mm(hn, w_b)
    bias_row = jnp.zeros((1, LANES), F32).at[0, :H].set(b_i.astype(F32)).at[0, H:2 * H].set(b_f.astype(F32))
    h_a = _mlstm(proj_a, gates, bias_row, mlstm_g, dk, dv)
    h_b = _hgrn(proj_b, lb_logits, lb_row, hgrn_g)
    y = jnp.concatenate([h_a, h_b], axis=1)
    return _mm(y, w_out.astype(BF16), res=h)


def _mixer_odd(h, hn, mu, w0, w1, w2, a0, a1, a2, g1, g2, k_k, k_a, r_k, w_r, w_k, w_v, w_o, lnx_w, lnx_b):
    s_r, s_w, s_k, s_v, s_a, s_g = _rwkv_shift(hn, mu)
    bf = lambda w: w.astype(BF16)
    r = _mm(s_r, bf(w_r))
    k = _mm(s_k, bf(w_k))
    v = _mm(s_v, bf(w_v))
    lw = _mm(_mm(s_w, bf(w1), act="tanh", out_dtype=BF16), bf(w2))
    la = _mm(_mm(s_a, bf(a1), out_dtype=BF16), bf(a2))
    g = _mm(_mm(s_g, bf(g1), act="sigmoid", out_dtype=BF16), bf(g2))
    y = _rwkv_core(r, k, v, lw, la, g, w0, a0, k_k, k_a, r_k, lnx_w, lnx_b)
    return _mm(y, bf(w_o), res=h)


def kernel(x, mem, norm_mix_g, norm_xattn_g, norm_mlp_g, final_norm_g, mem_norm_g, ab_w_in, mlstm_b_i, mlstm_b_f, mlstm_norm_g, hgrn_lb_logits, hgrn_norm_g, ab_w_out, rwkv_mu, rwkv_w0, rwkv_w1, rwkv_w2, rwkv_a0, rwkv_a1, rwkv_a2, rwkv_g1, rwkv_g2, rwkv_k_k, rwkv_k_a, rwkv_r_k, rwkv_w_r, rwkv_w_k, rwkv_w_v, rwkv_w_o, rwkv_lnx_w, rwkv_lnx_b, xattn_w_q, xattn_w_o, mem_w_kv, mlp_w_up, mlp_w_down):
    B, T, D = x.shape
    depth = norm_mix_g.shape[0]
    outs = []
    for b in range(B):
        mem_kv = _mm(_rmsnorm(mem[b], mem_norm_g, BF16), mem_w_kv.astype(BF16), out_dtype=BF16)
        h = x[b]
        for layer in range(depth):
            j = layer // 2
            if layer % 2 == 0:
                hn = _rmsnorm(h, norm_mix_g[layer], BF16)
                h = _mixer_even(h, hn, ab_w_in[j], mlstm_b_i[j], mlstm_b_f[j], mlstm_norm_g[j],
                                hgrn_lb_logits, j, hgrn_norm_g[j], ab_w_out[j])
            else:
                hn = _rmsnorm(h, norm_mix_g[layer], F32)
                h = _mixer_odd(h, hn, rwkv_mu[j], rwkv_w0[j], rwkv_w1[j], rwkv_w2[j], rwkv_a0[j], rwkv_a1[j],
                               rwkv_a2[j], rwkv_g1[j], rwkv_g2[j], rwkv_k_k[j], rwkv_k_a[j], rwkv_r_k[j],
                               rwkv_w_r[j], rwkv_w_k[j], rwkv_w_v[j], rwkv_w_o[j], rwkv_lnx_w[j], rwkv_lnx_b[j])
            q = _mm(_rmsnorm(h, norm_xattn_g[layer], BF16), xattn_w_q[layer].astype(BF16), out_dtype=BF16)
            h = _mm(_xattn(q, mem_kv), xattn_w_o[layer].astype(BF16), res=h)
            up = _mm(_rmsnorm(h, norm_mlp_g[layer], BF16), mlp_w_up[layer].astype(BF16),
                     act="relu2", out_dtype=BF16)
            h = _mm(up, mlp_w_down[layer].astype(BF16), res=h)
        outs.append(_rmsnorm(h, final_norm_g, F32))
    return jnp.stack(outs)
```

```python
import functools

import jax
import jax.numpy as jnp
from jax import lax
from jax.experimental import pallas as pl
from jax.experimental.pallas import tpu as pltpu

F32 = jnp.float32
BF16 = jnp.bfloat16

NORM_EPS = 1e-6
GATE_CAP = 15.0
MLSTM_HEADS = 4
MLSTM_CHUNK = 64
HGRN_HEAD_DIM = 128
HGRN_CHUNK = 16
RWKV_HEAD = 64
RWKV_CHUNK = 64
RWKV_LNX_EPS = 64e-5
XATTN_HEADS = 4
LANES = 128
MXU_DIM = 256
STEP_LANES = 1024
VMEM_LIMIT = 56 * 1024 * 1024

NT = (((1,), (1,)), ((), ()))
TN = (((0,), (0,)), ((), ()))


def _cparams(sem):
    return pltpu.CompilerParams(dimension_semantics=sem, vmem_limit_bytes=VMEM_LIMIT)


def _sigmoid(x):
    return 1.0 / (1.0 + jnp.exp(-x))


def _log_sigmoid(x):
    return jnp.minimum(x, 0.0) - jnp.log1p(jnp.exp(-jnp.abs(x)))


def _softplus(x):
    return jnp.maximum(x, 0.0) + jnp.log1p(jnp.exp(-jnp.abs(x)))


def _pick(dim, pref):
    t = min(dim, pref)
    while dim % t:
        t //= 2
    return t


def _head_masks(rows, width, head):
    lane = lax.broadcasted_iota(jnp.int32, (rows, width), 1) // head
    return [lane == h for h in range(width // head)]


def _stack_heads(x, masks):
    z = jnp.zeros_like(x)
    return jnp.concatenate([jnp.where(m, x, z) for m in masks], axis=0)


_BDOT = {"nn": "gmk,gkn->gmn", "nt": "gmk,gnk->gmn", "tn": "gtm,gtn->gmn"}


def _bdot(xs, ys, mode):
    d = jnp.einsum(_BDOT[mode], jnp.stack(xs), jnp.stack(ys), preferred_element_type=F32)
    return [d[g] for g in range(len(xs))]


def _same_head(n, head):
    r = lax.broadcasted_iota(jnp.int32, (n, n), 0) // head
    c = lax.broadcasted_iota(jnp.int32, (n, n), 1) // head
    return r == c


def _mm_kernel(*refs, nk, act, has_res, seg):
    if has_res:
        a_ref, b_ref, res_ref, o_ref = refs[:4]
        rest = refs[4:]
    else:
        a_ref, b_ref, o_ref = refs[:3]
        res_ref = None
        rest = refs[3:]

    def finish(r):
        if act == "relu2":
            r = jnp.square(jnp.maximum(r, 0.0))
        elif act == "tanh":
            r = jnp.tanh(r)
        elif act == "sigmoid":
            r = _sigmoid(r)
        elif act == "softmax":
            scale, width = seg
            parts = []
            for j in range(r.shape[1] // width):
                s = r[:, j * width:(j + 1) * width] * scale
                e = jnp.exp(s - jnp.max(s, axis=-1, keepdims=True))
                parts.append(e / jnp.sum(e, axis=-1, keepdims=True))
            r = jnp.concatenate(parts, axis=1)
        if has_res:
            r = r + res_ref[...].astype(F32)
        o_ref[...] = r.astype(o_ref.dtype)

    part = jnp.dot(a_ref[...].astype(BF16), b_ref[...].astype(BF16), preferred_element_type=F32)
    if nk == 1:
        finish(part)
    else:
        acc_ref = rest[0]
        k = pl.program_id(2)

        @pl.when(k == 0)
        def _():
            acc_ref[...] = part

        @pl.when(k > 0)
        def _():
            acc_ref[...] += part

        @pl.when(k == nk - 1)
        def _():
            finish(acc_ref[...])


def _mm(a, b, *, out_dtype=F32, act=None, res=None, seg=None, tm=1024, tn=512, tk=4096, name="mm"):
    M, K = a.shape
    K2, N = b.shape
    assert K == K2
    tm, tn, tk = _pick(M, tm), _pick(N, tn), _pick(K, tk)
    nk = K // tk
    in_specs = [pl.BlockSpec((tm, tk), lambda i, j, k: (i, k)),
                pl.BlockSpec((tk, tn), lambda i, j, k: (k, j))]
    args = [a, b]
    if res is not None:
        in_specs.append(pl.BlockSpec((tm, tn), lambda i, j, k: (i, j)))
        args.append(res)
    scratch = [] if nk == 1 else [pltpu.VMEM((tm, tn), F32)]
    return pl.pallas_call(
        functools.partial(_mm_kernel, nk=nk, act=act, has_res=res is not None, seg=seg),
        grid=(M // tm, N // tn, nk),
        in_specs=in_specs,
        out_specs=pl.BlockSpec((tm, tn), lambda i, j, k: (i, j)),
        out_shape=jax.ShapeDtypeStruct((M, N), out_dtype),
        scratch_shapes=scratch,
        compiler_params=_cparams(("parallel", "parallel", "arbitrary")),
        name=name,
    )(*args)


def _mm_blocks_kernel(a_ref, b_ref, o_ref, *, trans_b):
    a = a_ref[...].astype(BF16)
    b = b_ref[...].astype(BF16)
    if trans_b:
        r = lax.dot_general(a, b, NT, preferred_element_type=F32)
    else:
        r = jnp.dot(a, b, preferred_element_type=F32)
    o_ref[...] = r.astype(o_ref.dtype)


def _rmsnorm_kernel(x_ref, g_ref, o_ref):
    x = x_ref[...].astype(F32)
    y = x * lax.rsqrt(jnp.mean(x * x, axis=-1, keepdims=True) + NORM_EPS)
    o_ref[...] = (y * g_ref[...]).astype(o_ref.dtype)


def _rmsnorm(x, g, out_dtype, tm=256):
    M, D = x.shape
    tm = _pick(M, tm)
    return pl.pallas_call(
        _rmsnorm_kernel,
        grid=(M // tm,),
        in_specs=[pl.BlockSpec((tm, D), lambda i: (i, 0)),
                  pl.BlockSpec((1, D), lambda i: (0, 0))],
        out_specs=pl.BlockSpec((tm, D), lambda i: (i, 0)),
        out_shape=jax.ShapeDtypeStruct((M, D), out_dtype),
        compiler_params=_cparams(("parallel",)),
        name="rmsnorm",
    )(x, g.reshape(1, D).astype(F32))


def _mlstm_kernel(q_ref, k_ref, v_ref, og_ref, gates_ref, bias_ref, ng_ref, o_ref,
                  c_ref, n_ref, m_ref, *, heads):
    hd = pl.program_id(0)
    L = q_ref.shape[0]
    dk = q_ref.shape[1]

    @pl.when(pl.program_id(1) == 0)
    def _():
        c_ref[...] = jnp.zeros_like(c_ref)
        n_ref[...] = jnp.zeros_like(n_ref)
        m_ref[...] = jnp.zeros_like(m_ref)

    pre = gates_ref[...] + bias_ref[...]
    capped = GATE_CAP * jnp.tanh(pre / GATE_CAP)
    lane = lax.broadcasted_iota(jnp.int32, capped.shape, 1)
    li_col = jnp.sum(jnp.where(lane == hd, capped, 0.0), axis=1, keepdims=True)
    lf_col = jnp.sum(jnp.where(lane == hd + heads, _log_sigmoid(capped), 0.0), axis=1, keepdims=True)

    row = lax.broadcasted_iota(jnp.int32, (L, L), 0)
    col = lax.broadcasted_iota(jnp.int32, (L, L), 1)
    eye = row == col
    lower = col <= row
    lf_row = jnp.sum(jnp.where(eye, lf_col, 0.0), axis=0, keepdims=True)
    li_row = jnp.sum(jnp.where(eye, li_col, 0.0), axis=0, keepdims=True)
    b_col = jnp.sum(jnp.where(lower, lf_row, 0.0), axis=1, keepdims=True)
    b_row = jnp.sum(jnp.where(row <= col, lf_col, 0.0), axis=0, keepdims=True)
    g_row = li_row - b_row
    g_col = li_col - b_col
    m_prev = m_ref[:, 0:1]
    a_col = jnp.maximum(m_prev, jnp.max(jnp.where(lower, g_row, -jnp.inf), axis=1, keepdims=True))
    a_last = jnp.max(a_col, axis=0, keepdims=True)
    b_last = jnp.sum(lf_col, axis=0, keepdims=True)

    q = q_ref[...] * (dk ** -0.5)
    k = k_ref[...]
    v = v_ref[...].astype(BF16)
    qb = q.astype(BF16)
    s = lax.dot_general(qb, k.astype(BF16), NT, preferred_element_type=F32)
    s = s * jnp.where(lower, jnp.exp(g_row - a_col), 0.0)
    g_inter = jnp.exp(m_prev - a_col)
    qc = jnp.dot(qb, c_ref[...].astype(BF16), preferred_element_type=F32)
    num = jnp.dot(s.astype(BF16), v, preferred_element_type=F32) + g_inter * qc
    qn = jnp.sum(q * n_ref[...], axis=1, keepdims=True)
    den = jnp.sum(s, axis=1, keepdims=True) + g_inter * qn
    hh = num / jnp.maximum(jnp.abs(den), jnp.exp(-(b_col + a_col)))

    w_col = jnp.exp(g_col - a_last)
    decay = jnp.exp(m_prev - a_last)
    kw = k * w_col
    c_ref[...] = decay * c_ref[...] + lax.dot_general(kw.astype(BF16), v, TN, preferred_element_type=F32)
    n_ref[...] = decay * n_ref[...] + jnp.sum(kw, axis=0, keepdims=True)
    m_ref[...] = jnp.broadcast_to(b_last + a_last, m_ref.shape)

    y = hh * lax.rsqrt(jnp.mean(hh * hh, axis=-1, keepdims=True) + NORM_EPS)
    o_ref[...] = (y * ng_ref[...] * _sigmoid(og_ref[...])).astype(o_ref.dtype)


def _mlstm(proj, gates, bias_row, norm_g, dk, dv):
    T = proj.shape[0]
    H = MLSTM_HEADS
    L = MLSTM_CHUNK
    v0 = (2 * H * dk) // dv
    return pl.pallas_call(
        functools.partial(_mlstm_kernel, heads=H),
        grid=(H, T // L),
        in_specs=[pl.BlockSpec((L, dk), lambda h, c: (c, h)),
                  pl.BlockSpec((L, dk), lambda h, c: (c, H + h)),
                  pl.BlockSpec((L, dv), lambda h, c: (c, v0 + h)),
                  pl.BlockSpec((L, dv), lambda h, c: (c, v0 + H + h)),
                  pl.BlockSpec((L, LANES), lambda h, c: (c, 0)),
                  pl.BlockSpec((1, LANES), lambda h, c: (0, 0)),
                  pl.BlockSpec((1, dv), lambda h, c: (0, h))],
        out_specs=pl.BlockSpec((L, dv), lambda h, c: (c, h)),
        out_shape=jax.ShapeDtypeStruct((T, H * dv), BF16),
        scratch_shapes=[pltpu.VMEM((dk, dv), F32), pltpu.VMEM((1, dk), F32), pltpu.VMEM((1, LANES), F32)],
        compiler_params=_cparams(("parallel", "arbitrary")),
        name="mlstm",
    )(proj, proj, proj, proj, gates, bias_row, norm_g.reshape(1, H * dv).astype(F32))


def _hgrn_kernel(q_ref, f_ref, i_ref, g_ref, lb_ref, ng_ref, o_ref, st_ref, *, lb_row, chunk):
    TB = q_ref.shape[0]
    dk = HGRN_HEAD_DIM
    GW = st_ref.shape[1]
    NG = q_ref.shape[1] // GW
    HG = GW // dk
    C = chunk

    @pl.when(pl.program_id(1) == 0)
    def _():
        st_ref[...] = jnp.zeros_like(st_ref)

    logits = lb_ref[...].astype(F32)
    e = jnp.exp(logits - jnp.max(logits, axis=0, keepdims=True))
    p = e / jnp.sum(e, axis=0, keepdims=True)
    lb_all = jnp.sum(p[0:lb_row + 1], axis=0, keepdims=True)
    ng_all = ng_ref[...]
    srow = lax.broadcasted_iota(jnp.int32, (C, GW), 0)
    srow_h = srow[:, :dk]
    same = _same_head(GW, dk)

    def elementwise(g, ins):
        qb, fb, vb = ins
        lb = lb_all[:, g * GW:(g + 1) * GW]
        f = lb + (1.0 - lb) * _sigmoid(fb)
        kk = 1.0 - f
        qh = qb * _sigmoid(qb)
        A = jnp.log(f)
        sh = 1
        while sh < C:
            A = A + jnp.where(srow >= sh, pltpu.roll(A, sh, axis=0), 0.0)
            sh *= 2
        a_last = A[C - 1:C]
        intra = []
        for h in range(HG):
            hs = slice(h * dk, (h + 1) * dk)
            Ah, kh, qhh, vh = A[:, hs], kk[:, hs], qh[:, hs], vb[:, hs]
            rows = []
            for t in range(C):
                rel = jnp.where(srow_h <= t, jnp.exp(Ah[t:t + 1] - Ah), 0.0)
                sc = jnp.sum(rel * kh * qhh[t:t + 1], axis=1, keepdims=True)
                rows.append(jnp.sum(sc * vh, axis=0, keepdims=True))
            intra.append(jnp.concatenate(rows, axis=0))
        qe = (qh * jnp.exp(A)).astype(BF16)
        ke = (kk * jnp.exp(a_last - A)).astype(BF16)
        return qe, ke, vb.astype(BF16), jnp.concatenate(intra, axis=1), jnp.exp(a_last)

    def body(ci, carry):
        rs = pl.ds(pl.multiple_of(ci * C, C), C)
        G = range(NG)
        gsl = [slice(g * GW, (g + 1) * GW) for g in G]
        st = [st_ref[g] for g in G]
        ew = [elementwise(g, tuple(ref[rs, gsl[g]] for ref in (q_ref, f_ref, i_ref))) for g in G]
        o_inter = _bdot([e[0] for e in ew], [s.astype(BF16) for s in st], "nt")
        upd = _bdot([e[2] for e in ew], [e[1] for e in ew], "tn")
        for g in G:
            st_ref[g] = st[g] * ew[g][4] + jnp.where(same, upd[g], 0.0)
            o = o_inter[g] + ew[g][3]
            ys = []
            for h in range(HG):
                oh = o[:, h * dk:(h + 1) * dk]
                ys.append(oh * lax.rsqrt(jnp.mean(oh * oh, axis=-1, keepdims=True) + NORM_EPS))
            gb = g_ref[rs, gsl[g]]
            o_ref[rs, gsl[g]] = (jnp.concatenate(ys, axis=1) * ng_all[:, gsl[g]]
                                 * (gb * _sigmoid(gb))).astype(o_ref.dtype)
        return carry

    lax.fori_loop(0, TB // C, body, 0)


def _hgrn(proj, lb_logits, lb_row, norm_g, tb=256):
    T = proj.shape[0]
    W = proj.shape[1] // 4
    sw = _pick(W, STEP_LANES)
    gw = _pick(sw, MXU_DIM)
    H = W // sw
    tb = _pick(T, tb)
    R = lb_logits.shape[0]
    return pl.pallas_call(
        functools.partial(_hgrn_kernel, lb_row=lb_row, chunk=HGRN_CHUNK),
        grid=(H, T // tb),
        in_specs=[pl.BlockSpec((tb, sw), lambda h, c: (c, h)),
                  pl.BlockSpec((tb, sw), lambda h, c: (c, H + h)),
                  pl.BlockSpec((tb, sw), lambda h, c: (c, 2 * H + h)),
                  pl.BlockSpec((tb, sw), lambda h, c: (c, 3 * H + h)),
                  pl.BlockSpec((R, sw), lambda h, c: (0, h)),
                  pl.BlockSpec((1, sw), lambda h, c: (0, h))],
        out_specs=pl.BlockSpec((tb, sw), lambda h, c: (c, h)),
        out_shape=jax.ShapeDtypeStruct((T, W), BF16),
        scratch_shapes=[pltpu.VMEM((sw // gw, gw, gw), F32)],
        compiler_params=_cparams(("parallel", "arbitrary")),
        name="hgrn2",
    )(proj, proj, proj, proj, lb_logits.astype(F32), norm_g.reshape(1, W).astype(F32))


def _xattn_weights(w_q, w_o, mem_kv):
    D = w_q.shape[0]
    M = mem_kv.shape[0]
    H = XATTN_HEADS
    d = D // H
    tm = _pick(D, 1024)
    wqk = pl.pallas_call(
        functools.partial(_mm_blocks_kernel, trans_b=True),
        grid=(D // tm, H),
        in_specs=[pl.BlockSpec((tm, d), lambda i, h: (i, h)),
                  pl.BlockSpec((M, d), lambda i, h: (0, h))],
        out_specs=pl.BlockSpec((tm, M), lambda i, h: (i, h)),
        out_shape=jax.ShapeDtypeStruct((D, H * M), BF16),
        compiler_params=_cparams(("parallel", "parallel")),
        name="xattn_wqk",
    )(w_q, mem_kv)
    tn = _pick(D, 1024)
    vwo = pl.pallas_call(
        functools.partial(_mm_blocks_kernel, trans_b=False),
        grid=(H, D // tn),
        in_specs=[pl.BlockSpec((M, d), lambda h, j: (0, H + h)),
                  pl.BlockSpec((d, tn), lambda h, j: (h, j))],
        out_specs=pl.BlockSpec((M, tn), lambda h, j: (h, j)),
        out_shape=jax.ShapeDtypeStruct((H * M, D), BF16),
        compiler_params=_cparams(("parallel", "parallel")),
        name="xattn_vwo",
    )(mem_kv, w_o)
    return wqk, vwo


def _shift_kernel(x_ref, prev_ref, mu_ref, *o_refs):
    x = x_ref[...]
    n = prev_ref.shape[0]
    last = jnp.where(pl.program_id(0) == 0, 0.0, prev_ref[n - 1:n, :])
    row = lax.broadcasted_iota(jnp.int32, x.shape, 0)
    xx = jnp.where(row == 0, last, pltpu.roll(x, 1, axis=0)) - x
    for j, o_ref in enumerate(o_refs):
        o_ref[...] = (x + xx * mu_ref[j:j + 1, :]).astype(o_ref.dtype)


def _rwkv_shift(x, mu, tm=256):
    T, D = x.shape
    J = mu.shape[0]
    tm = _pick(T, tm)
    pr = 8
    spec = pl.BlockSpec((tm, D), lambda i: (i, 0))
    return pl.pallas_call(
        _shift_kernel,
        grid=(T // tm,),
        in_specs=[spec,
                  pl.BlockSpec((pr, D), lambda i: (jnp.maximum(i * (tm // pr) - 1, 0), 0)),
                  pl.BlockSpec((J, D), lambda i: (0, 0))],
        out_specs=[spec] * J,
        out_shape=[jax.ShapeDtypeStruct((T, D), BF16)] * J,
        compiler_params=_cparams(("parallel",)),
        name="rwkv_shift",
    )(x, x, mu.astype(F32))


def _split3(x):
    x1 = x.astype(BF16)
    r1 = x - x1.astype(F32)
    x2 = r1.astype(BF16)
    x3 = (r1 - x2.astype(F32)).astype(BF16)
    return x1, x2, x3


def _dot_exact_rhs(x, m):
    n = x.shape[0]
    d = jnp.dot(jnp.concatenate(_split3(x), axis=0), m, preferred_element_type=F32)
    return d[:n] + d[n:2 * n] + d[2 * n:]


def _head_ones(n, head, scale, dtype):
    return jnp.where(_same_head(n, head), scale, 0.0).astype(dtype)


def _rwkv_prep_kernel(r_ref, k_ref, v_ref, lw_ref, la_ref, par_ref,
                      at_ref, rt_ref, bk_ref, tc_ref, rb_ref, akv_ref, rkv_ref, bon_ref, gl_ref, *, chunk):
    TB = r_ref.shape[0]
    L = chunk
    N = RWKV_HEAD
    GW = 4 * L
    HG = GW // N
    NG = r_ref.shape[1] // GW
    assert L == N and GW == MXU_DIM
    par = par_ref[...]
    seg1 = _head_ones(GW, N, 1.0, BF16)
    hmask = _head_masks(L, GW, N)
    lane = lax.broadcasted_iota(jnp.int32, (L, GW), 1)
    rowi = lax.broadcasted_iota(jnp.int32, (L, GW), 0)
    half = GW // 2
    lane_h = lax.broadcasted_iota(jnp.int32, (L, half), 1)
    row_h = lax.broadcasted_iota(jnp.int32, (L, half), 0)
    low = lane_h < N
    src = jnp.where(low, lane_h, lane_h - N)
    strict = src < row_h
    incl = src <= row_h
    eye_half = jnp.where(src == row_h, 1.0, 0.0)
    eye_cat = jnp.concatenate([eye_half, eye_half], axis=1)

    def elementwise(g, ins):
        gs = slice(g * GW, (g + 1) * GW)
        w0, a0, k_k, k_a, r_k = (par[i:i + 1, gs] for i in range(5))
        r, k, v, lw, la = ins
        w_log = -_softplus(-(w0 + lw)) - 0.5
        logw = -jnp.exp(w_log)
        a_sig = _sigmoid(a0 + la)
        kkr = k * k_k
        k2 = k * (1.0 + (a_sig - 1.0) * k_a)
        cs = logw
        sh = 1
        while sh < L:
            cs = cs + jnp.where(rowi >= sh, pltpu.roll(cs, sh, axis=0), 0.0)
            sh *= 2
        return r, v, logw, a_sig, kkr, k2, cs, r * k2 * r_k

    def pre_dots(ew, ss, bsum):
        r, v, logw, a_sig, kkr, k2, cs, _ = ew
        kk = kkr / jnp.maximum(jnp.sqrt(ss), 1e-12)
        ginv = jnp.exp(-cs)
        atb = (-kk * jnp.exp(cs - logw)).astype(BF16)
        rtb = (r * jnp.exp(cs)).astype(BF16)
        bt = (kk * a_sig * ginv).astype(BF16)
        kt = (k2 * ginv).astype(BF16)
        zero = jnp.zeros_like(atb)
        lhs = jnp.concatenate([jnp.where(m, x, zero) for m in hmask for x in (atb, rtb)], axis=0)
        return atb, rtb, bt, kt, bsum * v, jnp.exp(cs[L - 1:L]), lhs, jnp.concatenate([bt, kt, kt, bt], axis=0)

    def cat(parts, even_sel, odd_sel, mask):
        cols = [jnp.where(mask, jnp.where(low, even_sel(parts[2 * c]), odd_sel(parts[2 * c + 1])), 0.0)
                for c in range(HG // 2)]
        return jnp.concatenate(cols, axis=1)

    lo = lambda x: x[:, :half]
    hi = lambda x: x[:, half:]

    def body(ci, carry):
        r0 = pl.multiple_of(ci * L, L)
        sl = pl.ds(r0, L)
        G = range(NG)
        ews = [elementwise(g, tuple(ref[sl, g * GW:(g + 1) * GW]
                                    for ref in (r_ref, k_ref, v_ref, lw_ref, la_ref))) for g in G]
        sums = _dot_exact_rhs(jnp.concatenate([x for ew in ews for x in (ew[4] * ew[4], ew[7])], axis=0), seg1)
        pre = [pre_dots(ews[g], sums[2 * g * L:(2 * g + 1) * L], sums[(2 * g + 1) * L:(2 * g + 2) * L]) for g in G]
        P = _bdot([p[6] for p in pre], [p[7] for p in pre], "nt")
        pat = [[P[g][2 * h * L:(2 * h + 1) * L] for h in range(HG)] for g in G]
        prt = [[P[g][(2 * h + 1) * L:(2 * h + 2) * L] for h in range(HG)] for g in G]
        ncat = [cat(pat[g], lo, hi, strict) for g in G]
        nb = [n.astype(BF16) for n in ncat]
        pw = _bdot(nb, [_stack_heads(n, hmask) for n in nb], "nn")
        tinv = [eye_cat + n for n in ncat]
        power = 2
        while power < L:
            pwb = [x.astype(BF16) for x in pw]
            pst = [_stack_heads(x, hmask) for x in pwb]
            if 2 * power < L:
                d = _bdot([jnp.concatenate([tinv[g].astype(BF16), pwb[g]], axis=0) for g in G], pst, "nn")
                tinv = [tinv[g] + d[g][:L] for g in G]
                pw = [d[g][L:] for g in G]
            else:
                d = _bdot([t.astype(BF16) for t in tinv], pst, "nn")
                tinv = [tinv[g] + d[g] for g in G]
            power *= 2
        akrk = [jnp.concatenate([cat(pat[g], hi, lo, strict), cat(prt[g], hi, lo, incl)], axis=0).astype(BF16)
                for g in G]
        kv = _bdot(akrk, [_stack_heads(ews[g][1].astype(BF16), hmask) for g in G], "nn")
        for g in G:
            gs = slice(g * GW, (g + 1) * GW)
            atb, rtb, bt, kt, bon, gl = pre[g][:6]
            at_ref[sl, gs] = atb
            rt_ref[sl, gs] = rtb
            bk_ref[pl.ds(pl.multiple_of(2 * r0, 2 * L), L), gs] = bt
            bk_ref[pl.ds(pl.multiple_of(2 * r0 + L, L), L), gs] = kt
            tc_ref[sl, gs] = tinv[g].astype(BF16)
            rb_ref[sl, gs] = cat(prt[g], lo, hi, incl).astype(BF16)
            akv_ref[sl, gs] = kv[g][:L]
            rkv_ref[sl, gs] = kv[g][L:]
            bon_ref[sl, gs] = bon
            gl_ref[ci, :, gs] = gl
        return carry

    lax.fori_loop(0, TB // L, body, 0)


def _rwkv_state_kernel(at_ref, rt_ref, bk_ref, tc_ref, rb_ref, akv_ref, rkv_ref, gl_ref, v_ref,
                       y_ref, ht_ref, *, chunk):
    TB = v_ref.shape[0]
    L = chunk
    N = RWKV_HEAD
    GW = ht_ref.shape[1]
    NG = v_ref.shape[1] // GW

    @pl.when(pl.program_id(1) == 0)
    def _():
        ht_ref[...] = jnp.zeros_like(ht_ref)

    hmask = _head_masks(L, GW, N)
    same = _same_head(GW, N)

    def body(ci, carry):
        r0 = pl.multiple_of(ci * L, L)
        sl = pl.ds(r0, L)
        sl2 = pl.ds(pl.multiple_of(2 * r0, 2 * L), 2 * L)
        G = range(NG)
        gsl = [slice(g * GW, (g + 1) * GW) for g in G]
        ht = [ht_ref[g] for g in G]
        xr = _bdot([jnp.concatenate([at_ref[sl, gs], rt_ref[sl, gs]], axis=0) for gs in gsl],
                   [h.astype(BF16) for h in ht], "nt")
        xb = [(xr[g][:L] + akv_ref[sl, gsl[g]]).astype(BF16) for g in G]
        u = _bdot([tc_ref[sl, gs] for gs in gsl], [_stack_heads(x, hmask) for x in xb], "nn")
        ub = [x.astype(BF16) for x in u]
        yd = _bdot([rb_ref[sl, gs] for gs in gsl], [_stack_heads(x, hmask) for x in ub], "nn")
        dht = _bdot([jnp.concatenate([ub[g], v_ref[sl, gsl[g]].astype(BF16)], axis=0) for g in G],
                    [bk_ref[sl2, gs] for gs in gsl], "tn")
        for g in G:
            ht_ref[g] = (ht[g] + jnp.where(same, dht[g], 0.0)) * gl_ref[ci, :, gsl[g]]
            y_ref[sl, gsl[g]] = xr[g][L:] + yd[g] + rkv_ref[sl, gsl[g]]
        return carry

    lax.fori_loop(0, TB // L, body, 0)


def _rwkv_out_kernel(y_ref, bon_ref, g_ref, ln_ref, o_ref):
    N = RWKV_HEAD
    GW = MXU_DIM
    seg_mean = _head_ones(GW, N, 1.0 / N, BF16)
    for g in range(y_ref.shape[1] // GW):
        gs = slice(g * GW, (g + 1) * GW)
        y = y_ref[:, gs]
        yc = y - _dot_exact_rhs(y, seg_mean)
        var = _dot_exact_rhs(yc * yc, seg_mean)
        yn = yc * lax.rsqrt(var + RWKV_LNX_EPS) * ln_ref[0:1, gs] + ln_ref[1:2, gs]
        o_ref[:, gs] = ((yn + bon_ref[:, gs]) * g_ref[:, gs]).astype(o_ref.dtype)


def _rwkv_core(r, k, v, lw, la, g, w0, a0, k_k, k_a, r_k, lnx_w, lnx_b, tb=256):
    T, D = r.shape
    L = RWKV_CHUNK
    gw = MXU_DIM
    W = _pick(D, STEP_LANES)
    assert W % gw == 0
    P = D // W
    tb = _pick(T, tb)
    nc = tb // L
    par = jnp.stack([w0, a0, k_k, k_a, r_k.reshape(D)]).astype(F32)
    par = jnp.concatenate([par, jnp.zeros((3, D), F32)], axis=0)
    spec = pl.BlockSpec((tb, W), lambda p, c: (c, p))
    spec2 = pl.BlockSpec((2 * tb, W), lambda p, c: (c, p))
    gspec = pl.BlockSpec((nc, 1, W), lambda p, c: (c, 0, p))
    bf = jax.ShapeDtypeStruct((T, D), BF16)
    f32 = jax.ShapeDtypeStruct((T, D), F32)
    at, rt, bk, tc, rb, akv, rkv, bon, gl = pl.pallas_call(
        functools.partial(_rwkv_prep_kernel, chunk=L),
        grid=(P, T // tb),
        in_specs=[spec] * 5 + [pl.BlockSpec((8, W), lambda p, c: (0, p))],
        out_specs=[spec, spec, spec2, spec, spec, spec, spec, spec, gspec],
        out_shape=[bf, bf, jax.ShapeDtypeStruct((2 * T, D), BF16), bf, bf, f32, f32, f32,
                   jax.ShapeDtypeStruct((T // L, 1, D), F32)],
        compiler_params=_cparams(("parallel", "parallel")),
        name="rwkv_prep",
    )(r, k, v, lw, la, par)
    y = pl.pallas_call(
        functools.partial(_rwkv_state_kernel, chunk=L),
        grid=(P, T // tb),
        in_specs=[spec, spec, spec2, spec, spec, spec, spec, gspec, spec],
        out_specs=spec,
        out_shape=f32,
        scratch_shapes=[pltpu.VMEM((W // gw, gw, gw), F32)],
        compiler_params=_cparams(("parallel", "arbitrary")),
        name="rwkv_state",
    )(at, rt, bk, tc, rb, akv, rkv, gl, v)
    ln = jnp.stack([lnx_w, lnx_b]).astype(F32)
    ln = jnp.concatenate([ln, jnp.zeros((6, D), F32)], axis=0)
    return pl.pallas_call(
        _rwkv_out_kernel,
        grid=(P, T // tb),
        in_specs=[spec, spec, spec, pl.BlockSpec((8, W), lambda p, c: (0, p))],
        out_specs=spec,
        out_shape=bf,
        compiler_params=_cparams(("parallel", "parallel")),
        name="rwkv_out",
    )(y, bon, g, ln)


def _mixer_even(h, hn, w_in, b_i, b_f, mlstm_g, lb_logits, lb_row, hgrn_g, w_out):
    D = h.shape[1]
    H = MLSTM_HEADS
    mix_a = D // 2
    dv = mix_a // H
    dk = dv // 2
    na = 2 * H * dk + 2 * mix_a
    w_a = w_in[:, :na].astype(BF16)
    w_if = jnp.zeros((D, LANES), BF16).at[:, :2 * H].set(w_in[:, na:na + 2 * H].astype(BF16))
    w_b = w_in[:, na + 2 * H:].astype(BF16)
    proj_a = _mm(hn, w_a, name="mm_mlstm_in")
    gates = _mm(hn, w_if, name="mm_gates")
    proj_b = _mm(hn, w_b, name="mm_hgrn_in")
    bias_row = jnp.zeros((1, LANES), F32).at[0, :H].set(b_i.astype(F32)).at[0, H:2 * H].set(b_f.astype(F32))
    h_a = _mlstm(proj_a, gates, bias_row, mlstm_g, dk, dv)
    h_b = _hgrn(proj_b, lb_logits, lb_row, hgrn_g)
    y = jnp.concatenate([h_a, h_b], axis=1)
    return _mm(y, w_out.astype(BF16), res=h, name="mm_mix_out")


def _mixer_odd(h, hn, mu, w0, w1, w2, a0, a1, a2, g1, g2, k_k, k_a, r_k, w_r, w_k, w_v, w_o, lnx_w, lnx_b):
    s_r, s_w, s_k, s_v, s_a, s_g = _rwkv_shift(hn, mu)
    bf = lambda w: w.astype(BF16)
    r = _mm(s_r, bf(w_r), name="mm_rwkv_r")
    k = _mm(s_k, bf(w_k), name="mm_rwkv_k")
    v = _mm(s_v, bf(w_v), name="mm_rwkv_v")
    lw = _mm(_mm(s_w, bf(w1), act="tanh", out_dtype=BF16, name="mm_lora_w1"), bf(w2), name="mm_lora_w2")
    la = _mm(_mm(s_a, bf(a1), out_dtype=BF16, name="mm_lora_a1"), bf(a2), name="mm_lora_a2")
    g = _mm(_mm(s_g, bf(g1), act="sigmoid", out_dtype=BF16, name="mm_lora_g1"), bf(g2), name="mm_lora_g2")
    y = _rwkv_core(r, k, v, lw, la, g, w0, a0, k_k, k_a, r_k, lnx_w, lnx_b)
    return _mm(y, bf(w_o), res=h, name="mm_rwkv_out")


def kernel(x, mem, norm_mix_g, norm_xattn_g, norm_mlp_g, final_norm_g, mem_norm_g, ab_w_in, mlstm_b_i, mlstm_b_f, mlstm_norm_g, hgrn_lb_logits, hgrn_norm_g, ab_w_out, rwkv_mu, rwkv_w0, rwkv_w1, rwkv_w2, rwkv_a0, rwkv_a1, rwkv_a2, rwkv_g1, rwkv_g2, rwkv_k_k, rwkv_k_a, rwkv_r_k, rwkv_w_r, rwkv_w_k, rwkv_w_v, rwkv_w_o, rwkv_lnx_w, rwkv_lnx_b, xattn_w_q, xattn_w_o, mem_w_kv, mlp_w_up, mlp_w_down):
    B, T, D = x.shape
    depth = norm_mix_g.shape[0]
    M = mem.shape[1]
    xscale = (D // XATTN_HEADS) ** -0.5
    outs = []
    for b in range(B):
        mem_kv = _mm(_rmsnorm(mem[b], mem_norm_g, BF16), mem_w_kv.astype(BF16), out_dtype=BF16, name="mm_mem_kv")
        h = x[b]
        for layer in range(depth):
            j = layer // 2
            if layer % 2 == 0:
                hn = _rmsnorm(h, norm_mix_g[layer], BF16)
                h = _mixer_even(h, hn, ab_w_in[j], mlstm_b_i[j], mlstm_b_f[j], mlstm_norm_g[j],
                                hgrn_lb_logits, j, hgrn_norm_g[j], ab_w_out[j])
            else:
                hn = _rmsnorm(h, norm_mix_g[layer], F32)
                h = _mixer_odd(h, hn, rwkv_mu[j], rwkv_w0[j], rwkv_w1[j], rwkv_w2[j], rwkv_a0[j], rwkv_a1[j],
                               rwkv_a2[j], rwkv_g1[j], rwkv_g2[j], rwkv_k_k[j], rwkv_k_a[j], rwkv_r_k[j],
                               rwkv_w_r[j], rwkv_w_k[j], rwkv_w_v[j], rwkv_w_o[j], rwkv_lnx_w[j], rwkv_lnx_b[j])
            wqk, vwo = _xattn_weights(xattn_w_q[layer], xattn_w_o[layer], mem_kv)
            p = _mm(_rmsnorm(h, norm_xattn_g[layer], BF16), wqk, act="softmax", seg=(xscale, M),
                    out_dtype=BF16, tn=XATTN_HEADS * M, name="mm_xattn_scores")
            h = _mm(p, vwo, res=h, name="mm_xattn_out")
            up = _mm(_rmsnorm(h, norm_mlp_g[layer], BF16), mlp_w_up[layer].astype(BF16),
                     act="relu2", out_dtype=BF16, name="mm_mlp_up")
            h = _mm(up, mlp_w_down[layer].astype(BF16), res=h, name="mm_mlp_down")
        outs.append(_rmsnorm(h, final_norm_g, F32))
    return jnp.stack(outs)
```

```python
import functools

import jax
import jax.numpy as jnp
from jax import lax
from jax.experimental import pallas as pl
from jax.experimental.pallas import tpu as pltpu

F32 = jnp.float32
BF16 = jnp.bfloat16

NORM_EPS = 1e-6
GATE_CAP = 15.0
MLSTM_HEADS = 4
MLSTM_CHUNK = 64
HGRN_HEAD_DIM = 128
HGRN_CHUNK = 16
RWKV_HEAD = 64
RWKV_CHUNK = 64
RWKV_LNX_EPS = 64e-5
XATTN_HEADS = 4
LANES = 128
MXU_DIM = 256
STEP_LANES = 1024
VMEM_LIMIT = 56 * 1024 * 1024

NT = (((1,), (1,)), ((), ()))
TN = (((0,), (0,)), ((), ()))


def _cparams(sem):
    return pltpu.CompilerParams(dimension_semantics=sem, vmem_limit_bytes=VMEM_LIMIT)


def _sigmoid(x):
    return 1.0 / (1.0 + jnp.exp(-x))


def _log_sigmoid(x):
    return jnp.minimum(x, 0.0) - jnp.log1p(jnp.exp(-jnp.abs(x)))


def _softplus(x):
    return jnp.maximum(x, 0.0) + jnp.log1p(jnp.exp(-jnp.abs(x)))


def _pick(dim, pref):
    t = min(dim, pref)
    while dim % t:
        t //= 2
    return t


def _head_masks(rows, width, head):
    lane = lax.broadcasted_iota(jnp.int32, (rows, width), 1) // head
    return [lane == h for h in range(width // head)]


def _stack_heads(x, masks):
    z = jnp.zeros_like(x)
    return jnp.concatenate([jnp.where(m, x, z) for m in masks], axis=0)


_BDOT = {"nn": "gmk,gkn->gmn", "nt": "gmk,gnk->gmn", "tn": "gtm,gtn->gmn"}


def _bdot(xs, ys, mode):
    d = jnp.einsum(_BDOT[mode], jnp.stack(xs), jnp.stack(ys), preferred_element_type=F32)
    return [d[g] for g in range(len(xs))]


def _same_head(n, head):
    r = lax.broadcasted_iota(jnp.int32, (n, n), 0) // head
    c = lax.broadcasted_iota(jnp.int32, (n, n), 1) // head
    return r == c


def _mm_kernel(*refs, nk, act, has_res, seg):
    if has_res:
        a_ref, b_ref, res_ref, o_ref = refs[:4]
        rest = refs[4:]
    else:
        a_ref, b_ref, o_ref = refs[:3]
        res_ref = None
        rest = refs[3:]

    def finish(r):
        if act == "relu2":
            r = jnp.square(jnp.maximum(r, 0.0))
        elif act == "tanh":
            r = jnp.tanh(r)
        elif act == "sigmoid":
            r = _sigmoid(r)
        elif act == "softmax":
            scale, width = seg
            parts = []
            for j in range(r.shape[1] // width):
                s = r[:, j * width:(j + 1) * width] * scale
                e = jnp.exp(s - jnp.max(s, axis=-1, keepdims=True))
                parts.append(e / jnp.sum(e, axis=-1, keepdims=True))
            r = jnp.concatenate(parts, axis=1)
        if has_res:
            r = r + res_ref[...].astype(F32)
        o_ref[...] = r.astype(o_ref.dtype)

    part = jnp.dot(a_ref[...].astype(BF16), b_ref[...].astype(BF16), preferred_element_type=F32)
    if nk == 1:
        finish(part)
    else:
        acc_ref = rest[0]
        k = pl.program_id(2)

        @pl.when(k == 0)
        def _():
            acc_ref[...] = part

        @pl.when(k > 0)
        def _():
            acc_ref[...] += part

        @pl.when(k == nk - 1)
        def _():
            finish(acc_ref[...])


def _mm(a, b, *, out_dtype=F32, act=None, res=None, seg=None, tm=1024, tn=512, tk=4096, name="mm",
        layer=None, cols=None):
    M, K = a.shape
    K2, nb = b.shape[-2:]
    assert K == K2 and (b.ndim == 3) == (layer is not None)
    c0, N = cols if cols is not None else (0, nb)
    tm, tn, tk = _pick(M, tm), _pick(N, tn), _pick(K, tk)
    assert c0 % tn == 0
    j0 = c0 // tn
    nk = K // tk
    if layer is None:
        b_spec = pl.BlockSpec((tk, tn), lambda i, j, k: (k, j0 + j))
    else:
        b_spec = pl.BlockSpec((None, tk, tn), lambda i, j, k: (layer, k, j0 + j))
    in_specs = [pl.BlockSpec((tm, tk), lambda i, j, k: (i, k)), b_spec]
    args = [a, b]
    if res is not None:
        in_specs.append(pl.BlockSpec((tm, tn), lambda i, j, k: (i, j)))
        args.append(res)
    scratch = [] if nk == 1 else [pltpu.VMEM((tm, tn), F32)]
    return pl.pallas_call(
        functools.partial(_mm_kernel, nk=nk, act=act, has_res=res is not None, seg=seg),
        grid=(M // tm, N // tn, nk),
        in_specs=in_specs,
        out_specs=pl.BlockSpec((tm, tn), lambda i, j, k: (i, j)),
        out_shape=jax.ShapeDtypeStruct((M, N), out_dtype),
        scratch_shapes=scratch,
        compiler_params=_cparams(("parallel", "parallel", "arbitrary")),
        name=name,
    )(*args)


def _mm_blocks_kernel(a_ref, b_ref, o_ref, *, trans_b):
    a = a_ref[...].astype(BF16)
    b = b_ref[...].astype(BF16)
    if trans_b:
        r = lax.dot_general(a, b, NT, preferred_element_type=F32)
    else:
        r = jnp.dot(a, b, preferred_element_type=F32)
    o_ref[...] = r.astype(o_ref.dtype)


def _rmsnorm_kernel(x_ref, g_ref, o_ref):
    x = x_ref[...].astype(F32)
    y = x * lax.rsqrt(jnp.mean(x * x, axis=-1, keepdims=True) + NORM_EPS)
    o_ref[...] = (y * g_ref[...]).astype(o_ref.dtype)


def _rmsnorm(x, g, out_dtype, tm=256):
    M, D = x.shape
    tm = _pick(M, tm)
    return pl.pallas_call(
        _rmsnorm_kernel,
        grid=(M // tm,),
        in_specs=[pl.BlockSpec((tm, D), lambda i: (i, 0)),
                  pl.BlockSpec((1, D), lambda i: (0, 0))],
        out_specs=pl.BlockSpec((tm, D), lambda i: (i, 0)),
        out_shape=jax.ShapeDtypeStruct((M, D), out_dtype),
        compiler_params=_cparams(("parallel",)),
        name="rmsnorm",
    )(x, g.reshape(1, D).astype(F32))


def _mlstm_kernel(q_ref, k_ref, v_ref, og_ref, gates_ref, bias_ref, ng_ref, o_ref,
                  c_ref, n_ref, m_ref, *, heads):
    H = heads
    L = q_ref.shape[0]
    dk = q_ref.shape[1] // H
    dv = v_ref.shape[1] // H

    @pl.when(pl.program_id(0) == 0)
    def _():
        c_ref[...] = jnp.zeros_like(c_ref)
        n_ref[...] = jnp.zeros_like(n_ref)
        m_ref[...] = jnp.zeros_like(m_ref)

    pre = gates_ref[...] + bias_ref[...]
    capped = GATE_CAP * jnp.tanh(pre / GATE_CAP)
    lsig = _log_sigmoid(capped)
    lane = lax.broadcasted_iota(jnp.int32, capped.shape, 1)
    row = lax.broadcasted_iota(jnp.int32, (L, L), 0)
    col = lax.broadcasted_iota(jnp.int32, (L, L), 1)
    eye = row == col
    lower = col <= row

    def gate_terms(h):
        li_col = jnp.sum(jnp.where(lane == h, capped, 0.0), axis=1, keepdims=True)
        lf_col = jnp.sum(jnp.where(lane == h + H, lsig, 0.0), axis=1, keepdims=True)
        lf_row = jnp.sum(jnp.where(eye, lf_col, 0.0), axis=0, keepdims=True)
        li_row = jnp.sum(jnp.where(eye, li_col, 0.0), axis=0, keepdims=True)
        b_col = jnp.sum(jnp.where(lower, lf_row, 0.0), axis=1, keepdims=True)
        b_row = jnp.sum(jnp.where(row <= col, lf_col, 0.0), axis=0, keepdims=True)
        g_row = li_row - b_row
        g_col = li_col - b_col
        m_prev = m_ref[h][:, 0:1]
        a_col = jnp.maximum(m_prev, jnp.max(jnp.where(lower, g_row, -jnp.inf), axis=1, keepdims=True))
        a_last = jnp.max(a_col, axis=0, keepdims=True)
        b_last = jnp.sum(lf_col, axis=0, keepdims=True)
        return dict(dmat=jnp.where(lower, jnp.exp(g_row - a_col), 0.0), g_inter=jnp.exp(m_prev - a_col),
                    floor=jnp.exp(-(b_col + a_col)), w_col=jnp.exp(g_col - a_last),
                    decay=jnp.exp(m_prev - a_last), m_new=b_last + a_last)

    hs = range(H)
    gt = [gate_terms(h) for h in hs]
    q = [q_ref[:, h * dk:(h + 1) * dk] * (dk ** -0.5) for h in hs]
    k = [k_ref[:, h * dk:(h + 1) * dk] for h in hs]
    vb = [v_ref[:, h * dv:(h + 1) * dv].astype(BF16) for h in hs]
    qb = [x.astype(BF16) for x in q]
    cst = [c_ref[h] for h in hs]
    s = _bdot(qb, [x.astype(BF16) for x in k], "nt")
    s = [s[h] * gt[h]["dmat"] for h in hs]
    qc = _bdot(qb, [c.astype(BF16) for c in cst], "nn")
    sv = _bdot([x.astype(BF16) for x in s], vb, "nn")
    kw = [k[h] * gt[h]["w_col"] for h in hs]
    upd = _bdot([x.astype(BF16) for x in kw], vb, "tn")
    for h in hs:
        g = gt[h]
        num = sv[h] + g["g_inter"] * qc[h]
        qn = jnp.sum(q[h] * n_ref[h], axis=1, keepdims=True)
        den = jnp.sum(s[h], axis=1, keepdims=True) + g["g_inter"] * qn
        hh = num / jnp.maximum(jnp.abs(den), g["floor"])
        c_ref[h] = g["decay"] * cst[h] + upd[h]
        n_ref[h] = g["decay"] * n_ref[h] + jnp.sum(kw[h], axis=0, keepdims=True)
        m_ref[h] = jnp.broadcast_to(g["m_new"], m_ref.shape[1:])
        y = hh * lax.rsqrt(jnp.mean(hh * hh, axis=-1, keepdims=True) + NORM_EPS)
        cs = slice(h * dv, (h + 1) * dv)
        o_ref[:, cs] = (y * ng_ref[:, cs] * _sigmoid(og_ref[:, cs])).astype(o_ref.dtype)


def _mlstm(proj, gates, bias_row, norm_g, dk, dv, out_width):
    T = proj.shape[0]
    H = MLSTM_HEADS
    L = MLSTM_CHUNK
    wk, wv = H * dk, H * dv
    assert (2 * wk) % wv == 0
    v0 = (2 * wk) // wv
    return pl.pallas_call(
        functools.partial(_mlstm_kernel, heads=H),
        grid=(T // L,),
        in_specs=[pl.BlockSpec((L, wk), lambda c: (c, 0)),
                  pl.BlockSpec((L, wk), lambda c: (c, 1)),
                  pl.BlockSpec((L, wv), lambda c: (c, v0)),
                  pl.BlockSpec((L, wv), lambda c: (c, v0 + 1)),
                  pl.BlockSpec((L, LANES), lambda c: (c, 0)),
                  pl.BlockSpec((1, LANES), lambda c: (0, 0)),
                  pl.BlockSpec((1, wv), lambda c: (0, 0))],
        out_specs=pl.BlockSpec((L, wv), lambda c: (c, 0)),
        out_shape=jax.ShapeDtypeStruct((T, out_width), BF16),
        scratch_shapes=[pltpu.VMEM((H, dk, dv), F32), pltpu.VMEM((H, 1, dk), F32), pltpu.VMEM((H, 1, LANES), F32)],
        compiler_params=_cparams(("arbitrary",)),
        name="mlstm",
    )(proj, proj, proj, proj, gates, bias_row, norm_g.reshape(1, wv).astype(F32))


def _hgrn_kernel(q_ref, f_ref, i_ref, g_ref, lb_ref, ng_ref, y_hbm_ref, o_ref, st_ref, *, lb_row, chunk):
    TB = q_ref.shape[0]
    dk = HGRN_HEAD_DIM
    GW = st_ref.shape[1]
    NG = q_ref.shape[1] // GW
    HG = GW // dk
    C = chunk

    @pl.when(pl.program_id(1) == 0)
    def _():
        st_ref[...] = jnp.zeros_like(st_ref)

    logits = lb_ref[...].astype(F32)
    e = jnp.exp(logits - jnp.max(logits, axis=0, keepdims=True))
    p = e / jnp.sum(e, axis=0, keepdims=True)
    lb_all = jnp.sum(p[0:lb_row + 1], axis=0, keepdims=True)
    ng_all = ng_ref[...]
    srow = lax.broadcasted_iota(jnp.int32, (C, GW), 0)
    srow_h = srow[:, :dk]
    same = _same_head(GW, dk)

    def elementwise(g, ins):
        qb, fb, vb = ins
        lb = lb_all[:, g * GW:(g + 1) * GW]
        f = lb + (1.0 - lb) * _sigmoid(fb)
        kk = 1.0 - f
        qh = qb * _sigmoid(qb)
        A = jnp.log(f)
        sh = 1
        while sh < C:
            A = A + jnp.where(srow >= sh, pltpu.roll(A, sh, axis=0), 0.0)
            sh *= 2
        a_last = A[C - 1:C]
        intra = []
        for h in range(HG):
            hs = slice(h * dk, (h + 1) * dk)
            Ah, kh, qhh, vh = A[:, hs], kk[:, hs], qh[:, hs], vb[:, hs]
            rows = []
            for t in range(C):
                rel = jnp.where(srow_h <= t, jnp.exp(Ah[t:t + 1] - Ah), 0.0)
                sc = jnp.sum(rel * kh * qhh[t:t + 1], axis=1, keepdims=True)
                rows.append(jnp.sum(sc * vh, axis=0, keepdims=True))
            intra.append(jnp.concatenate(rows, axis=0))
        qe = (qh * jnp.exp(A)).astype(BF16)
        ke = (kk * jnp.exp(a_last - A)).astype(BF16)
        return qe, ke, vb.astype(BF16), jnp.concatenate(intra, axis=1), jnp.exp(a_last)

    def body(ci, carry):
        rs = pl.ds(pl.multiple_of(ci * C, C), C)
        G = range(NG)
        gsl = [slice(g * GW, (g + 1) * GW) for g in G]
        st = [st_ref[g] for g in G]
        ew = [elementwise(g, tuple(ref[rs, gsl[g]] for ref in (q_ref, f_ref, i_ref))) for g in G]
        o_inter = _bdot([e[0] for e in ew], [s.astype(BF16) for s in st], "nt")
        upd = _bdot([e[2] for e in ew], [e[1] for e in ew], "tn")
        for g in G:
            st_ref[g] = st[g] * ew[g][4] + jnp.where(same, upd[g], 0.0)
            o = o_inter[g] + ew[g][3]
            ys = []
            for h in range(HG):
                oh = o[:, h * dk:(h + 1) * dk]
                ys.append(oh * lax.rsqrt(jnp.mean(oh * oh, axis=-1, keepdims=True) + NORM_EPS))
            gb = g_ref[rs, gsl[g]]
            o_ref[rs, gsl[g]] = (jnp.concatenate(ys, axis=1) * ng_all[:, gsl[g]]
                                 * (gb * _sigmoid(gb))).astype(o_ref.dtype)
        return carry

    lax.fori_loop(0, TB // C, body, 0)


def _hgrn(proj, lb_logits, lb_row, norm_g, y, tb=256):
    T = proj.shape[0]
    W = proj.shape[1] // 4
    sw = _pick(W, STEP_LANES)
    gw = _pick(sw, MXU_DIM)
    H = W // sw
    tb = _pick(T, tb)
    R = lb_logits.shape[0]
    assert (y.shape[1] - W) % sw == 0
    off = (y.shape[1] - W) // sw
    return pl.pallas_call(
        functools.partial(_hgrn_kernel, lb_row=lb_row, chunk=HGRN_CHUNK),
        grid=(H, T // tb),
        in_specs=[pl.BlockSpec((tb, sw), lambda h, c: (c, h)),
                  pl.BlockSpec((tb, sw), lambda h, c: (c, H + h)),
                  pl.BlockSpec((tb, sw), lambda h, c: (c, 2 * H + h)),
                  pl.BlockSpec((tb, sw), lambda h, c: (c, 3 * H + h)),
                  pl.BlockSpec((R, sw), lambda h, c: (0, h)),
                  pl.BlockSpec((1, sw), lambda h, c: (0, h)),
                  pl.BlockSpec(memory_space=pl.ANY)],
        out_specs=pl.BlockSpec((tb, sw), lambda h, c: (c, off + h)),
        out_shape=jax.ShapeDtypeStruct(y.shape, y.dtype),
        input_output_aliases={6: 0},
        scratch_shapes=[pltpu.VMEM((sw // gw, gw, gw), F32)],
        compiler_params=_cparams(("parallel", "arbitrary")),
        name="hgrn2",
    )(proj, proj, proj, proj, lb_logits.astype(F32), norm_g.reshape(1, W).astype(F32), y)


def _xattn_weights(w_q, w_o, layer, mem_kv):
    D = w_q.shape[1]
    M = mem_kv.shape[0]
    H = XATTN_HEADS
    d = D // H
    tm = _pick(D, 1024)
    wqk = pl.pallas_call(
        functools.partial(_mm_blocks_kernel, trans_b=True),
        grid=(D // tm, H),
        in_specs=[pl.BlockSpec((None, tm, d), lambda i, h: (layer, i, h)),
                  pl.BlockSpec((M, d), lambda i, h: (0, h))],
        out_specs=pl.BlockSpec((tm, M), lambda i, h: (i, h)),
        out_shape=jax.ShapeDtypeStruct((D, H * M), BF16),
        compiler_params=_cparams(("parallel", "parallel")),
        name="xattn_wqk",
    )(w_q, mem_kv)
    tn = _pick(D, 1024)
    vwo = pl.pallas_call(
        functools.partial(_mm_blocks_kernel, trans_b=False),
        grid=(H, D // tn),
        in_specs=[pl.BlockSpec((M, d), lambda h, j: (0, H + h)),
                  pl.BlockSpec((None, d, tn), lambda h, j: (layer, h, j))],
        out_specs=pl.BlockSpec((M, tn), lambda h, j: (h, j)),
        out_shape=jax.ShapeDtypeStruct((H * M, D), BF16),
        compiler_params=_cparams(("parallel", "parallel")),
        name="xattn_vwo",
    )(mem_kv, w_o)
    return wqk, vwo


def _shift_kernel(h_ref, prev_ref, g_ref, mu_ref, *o_refs):
    def norm(h):
        return h * lax.rsqrt(jnp.mean(h * h, axis=-1, keepdims=True) + NORM_EPS) * g_ref[...]

    x = norm(h_ref[...])
    n = prev_ref.shape[0]
    last = jnp.where(pl.program_id(0) == 0, 0.0, norm(prev_ref[...])[n - 1:n, :])
    row = lax.broadcasted_iota(jnp.int32, x.shape, 0)
    xx = jnp.where(row == 0, last, pltpu.roll(x, 1, axis=0)) - x
    for j, o_ref in enumerate(o_refs):
        o_ref[...] = (x + xx * mu_ref[j:j + 1, :]).astype(o_ref.dtype)


def _rwkv_shift(h, g, mu, tm=256):
    T, D = h.shape
    J = mu.shape[0]
    tm = _pick(T, tm)
    pr = 8
    spec = pl.BlockSpec((tm, D), lambda i: (i, 0))
    return pl.pallas_call(
        _shift_kernel,
        grid=(T // tm,),
        in_specs=[spec,
                  pl.BlockSpec((pr, D), lambda i: (jnp.maximum(i * (tm // pr) - 1, 0), 0)),
                  pl.BlockSpec((1, D), lambda i: (0, 0)),
                  pl.BlockSpec((J, D), lambda i: (0, 0))],
        out_specs=[spec] * J,
        out_shape=[jax.ShapeDtypeStruct((T, D), BF16)] * J,
        compiler_params=_cparams(("parallel",)),
        name="rwkv_shift",
    )(h, h, g.reshape(1, D).astype(F32), mu.astype(F32))


def _split3(x):
    x1 = x.astype(BF16)
    r1 = x - x1.astype(F32)
    x2 = r1.astype(BF16)
    x3 = (r1 - x2.astype(F32)).astype(BF16)
    return x1, x2, x3


def _dot_exact_rhs(x, m):
    n = x.shape[0]
    d = jnp.dot(jnp.concatenate(_split3(x), axis=0), m, preferred_element_type=F32)
    return d[:n] + d[n:2 * n] + d[2 * n:]


def _head_ones(n, head, scale, dtype):
    return jnp.where(_same_head(n, head), scale, 0.0).astype(dtype)


def _rwkv_prep_kernel(r_ref, k_ref, v_ref, lw_ref, la_ref, par_ref,
                      at_ref, rt_ref, bk_ref, tc_ref, rb_ref, akv_ref, rkv_ref, bon_ref, gl_ref, *, chunk):
    TB = r_ref.shape[0]
    L = chunk
    N = RWKV_HEAD
    GW = 4 * L
    HG = GW // N
    NG = r_ref.shape[1] // GW
    assert L == N and GW == MXU_DIM
    par = par_ref[...]
    seg1 = _head_ones(GW, N, 1.0, BF16)
    hmask = _head_masks(L, GW, N)
    lane = lax.broadcasted_iota(jnp.int32, (L, GW), 1)
    rowi = lax.broadcasted_iota(jnp.int32, (L, GW), 0)
    half = GW // 2
    lane_h = lax.broadcasted_iota(jnp.int32, (L, half), 1)
    row_h = lax.broadcasted_iota(jnp.int32, (L, half), 0)
    low = lane_h < N
    src = jnp.where(low, lane_h, lane_h - N)
    strict = src < row_h
    incl = src <= row_h
    eye_half = jnp.where(src == row_h, 1.0, 0.0)
    eye_cat = jnp.concatenate([eye_half, eye_half], axis=1)

    def elementwise(g, ins):
        gs = slice(g * GW, (g + 1) * GW)
        w0, a0, k_k, k_a, r_k = (par[i:i + 1, gs] for i in range(5))
        r, k, v, lw, la = ins
        w_log = -_softplus(-(w0 + lw)) - 0.5
        logw = -jnp.exp(w_log)
        a_sig = _sigmoid(a0 + la)
        kkr = k * k_k
        k2 = k * (1.0 + (a_sig - 1.0) * k_a)
        cs = logw
        sh = 1
        while sh < L:
            cs = cs + jnp.where(rowi >= sh, pltpu.roll(cs, sh, axis=0), 0.0)
            sh *= 2
        return r, v, logw, a_sig, kkr, k2, cs, r * k2 * r_k

    def pre_dots(ew, ss, bsum):
        r, v, logw, a_sig, kkr, k2, cs, _ = ew
        kk = kkr / jnp.maximum(jnp.sqrt(ss), 1e-12)
        ginv = jnp.exp(-cs)
        atb = (-kk * jnp.exp(cs - logw)).astype(BF16)
        rtb = (r * jnp.exp(cs)).astype(BF16)
        bt = (kk * a_sig * ginv).astype(BF16)
        kt = (k2 * ginv).astype(BF16)
        zero = jnp.zeros_like(atb)
        lhs = jnp.concatenate([jnp.where(m, x, zero) for m in hmask for x in (atb, rtb)], axis=0)
        return atb, rtb, bt, kt, bsum * v, jnp.exp(cs[L - 1:L]), lhs, jnp.concatenate([bt, kt, kt, bt], axis=0)

    def cat(parts, even_sel, odd_sel, mask):
        cols = [jnp.where(mask, jnp.where(low, even_sel(parts[2 * c]), odd_sel(parts[2 * c + 1])), 0.0)
                for c in range(HG // 2)]
        return jnp.concatenate(cols, axis=1)

    lo = lambda x: x[:, :half]
    hi = lambda x: x[:, half:]

    def body(ci, carry):
        r0 = pl.multiple_of(ci * L, L)
        sl = pl.ds(r0, L)
        G = range(NG)
        ews = [elementwise(g, tuple(ref[sl, g * GW:(g + 1) * GW]
                                    for ref in (r_ref, k_ref, v_ref, lw_ref, la_ref))) for g in G]
        sums = _dot_exact_rhs(jnp.concatenate([x for ew in ews for x in (ew[4] * ew[4], ew[7])], axis=0), seg1)
        pre = [pre_dots(ews[g], sums[2 * g * L:(2 * g + 1) * L], sums[(2 * g + 1) * L:(2 * g + 2) * L]) for g in G]
        P = _bdot([p[6] for p in pre], [p[7] for p in pre], "nt")
        pat = [[P[g][2 * h * L:(2 * h + 1) * L] for h in range(HG)] for g in G]
        prt = [[P[g][(2 * h + 1) * L:(2 * h + 2) * L] for h in range(HG)] for g in G]
        ncat = [cat(pat[g], lo, hi, strict) for g in G]
        nb = [n.astype(BF16) for n in ncat]
        pw = _bdot(nb, [_stack_heads(n, hmask) for n in nb], "nn")
        tinv = [eye_cat + n for n in ncat]
        power = 2
        while power < L:
            pwb = [x.astype(BF16) for x in pw]
            pst = [_stack_heads(x, hmask) for x in pwb]
            if 2 * power < L:
                d = _bdot([jnp.concatenate([tinv[g].astype(BF16), pwb[g]], axis=0) for g in G], pst, "nn")
                tinv = [tinv[g] + d[g][:L] for g in G]
                pw = [d[g][L:] for g in G]
            else:
                d = _bdot([t.astype(BF16) for t in tinv], pst, "nn")
                tinv = [tinv[g] + d[g] for g in G]
            power *= 2
        akrk = [jnp.concatenate([cat(pat[g], hi, lo, strict), cat(prt[g], hi, lo, incl)], axis=0).astype(BF16)
                for g in G]
        kv = _bdot(akrk, [_stack_heads(ews[g][1].astype(BF16), hmask) for g in G], "nn")
        for g in G:
            gs = slice(g * GW, (g + 1) * GW)
            atb, rtb, bt, kt, bon, gl = pre[g][:6]
            at_ref[sl, gs] = atb
            rt_ref[sl, gs] = rtb
            bk_ref[pl.ds(pl.multiple_of(2 * r0, 2 * L), L), gs] = bt
            bk_ref[pl.ds(pl.multiple_of(2 * r0 + L, L), L), gs] = kt
            tc_ref[sl, gs] = tinv[g].astype(BF16)
            rb_ref[sl, gs] = cat(prt[g], lo, hi, incl).astype(BF16)
            akv_ref[sl, gs] = kv[g][:L]
            rkv_ref[sl, gs] = kv[g][L:]
            bon_ref[sl, gs] = bon
            gl_ref[ci, :, gs] = gl
        return carry

    lax.fori_loop(0, TB // L, body, 0)


def _rwkv_state_kernel(at_ref, rt_ref, bk_ref, tc_ref, rb_ref, akv_ref, rkv_ref, gl_ref, v_ref,
                       y_ref, ht_ref, *, chunk):
    TB = v_ref.shape[0]
    L = chunk
    N = RWKV_HEAD
    GW = ht_ref.shape[1]
    NG = v_ref.shape[1] // GW

    @pl.when(pl.program_id(1) == 0)
    def _():
        ht_ref[...] = jnp.zeros_like(ht_ref)

    hmask = _head_masks(L, GW, N)
    same = _same_head(GW, N)

    def body(ci, carry):
        r0 = pl.multiple_of(ci * L, L)
        sl = pl.ds(r0, L)
        sl2 = pl.ds(pl.multiple_of(2 * r0, 2 * L), 2 * L)
        G = range(NG)
        gsl = [slice(g * GW, (g + 1) * GW) for g in G]
        ht = [ht_ref[g] for g in G]
        xr = _bdot([jnp.concatenate([at_ref[sl, gs], rt_ref[sl, gs]], axis=0) for gs in gsl],
                   [h.astype(BF16) for h in ht], "nt")
        xb = [(xr[g][:L] + akv_ref[sl, gsl[g]]).astype(BF16) for g in G]
        u = _bdot([tc_ref[sl, gs] for gs in gsl], [_stack_heads(x, hmask) for x in xb], "nn")
        ub = [x.astype(BF16) for x in u]
        yd = _bdot([rb_ref[sl, gs] for gs in gsl], [_stack_heads(x, hmask) for x in ub], "nn")
        dht = _bdot([jnp.concatenate([ub[g], v_ref[sl, gsl[g]].astype(BF16)], axis=0) for g in G],
                    [bk_ref[sl2, gs] for gs in gsl], "tn")
        for g in G:
            ht_ref[g] = (ht[g] + jnp.where(same, dht[g], 0.0)) * gl_ref[ci, :, gsl[g]]
            y_ref[sl, gsl[g]] = xr[g][L:] + yd[g] + rkv_ref[sl, gsl[g]]
        return carry

    lax.fori_loop(0, TB // L, body, 0)


def _rwkv_out_kernel(y_ref, bon_ref, g_ref, ln_ref, o_ref):
    N = RWKV_HEAD
    GW = MXU_DIM
    seg_mean = _head_ones(GW, N, 1.0 / N, BF16)
    for g in range(y_ref.shape[1] // GW):
        gs = slice(g * GW, (g + 1) * GW)
        y = y_ref[:, gs]
        yc = y - _dot_exact_rhs(y, seg_mean)
        var = _dot_exact_rhs(yc * yc, seg_mean)
        yn = yc * lax.rsqrt(var + RWKV_LNX_EPS) * ln_ref[0:1, gs] + ln_ref[1:2, gs]
        o_ref[:, gs] = ((yn + bon_ref[:, gs]) * g_ref[:, gs]).astype(o_ref.dtype)


def _rwkv_core(r, k, v, lw, la, g, w0, a0, k_k, k_a, r_k, lnx_w, lnx_b, tb=256):
    T, D = r.shape
    L = RWKV_CHUNK
    gw = MXU_DIM
    W = _pick(D, STEP_LANES)
    assert W % gw == 0
    P = D // W
    tb = _pick(T, tb)
    nc = tb // L
    par = jnp.stack([w0, a0, k_k, k_a, r_k.reshape(D)]).astype(F32)
    par = jnp.concatenate([par, jnp.zeros((3, D), F32)], axis=0)
    spec = pl.BlockSpec((tb, W), lambda p, c: (c, p))
    spec2 = pl.BlockSpec((2 * tb, W), lambda p, c: (c, p))
    gspec = pl.BlockSpec((nc, 1, W), lambda p, c: (c, 0, p))
    bf = jax.ShapeDtypeStruct((T, D), BF16)
    f32 = jax.ShapeDtypeStruct((T, D), F32)
    at, rt, bk, tc, rb, akv, rkv, bon, gl = pl.pallas_call(
        functools.partial(_rwkv_prep_kernel, chunk=L),
        grid=(P, T // tb),
        in_specs=[spec] * 5 + [pl.BlockSpec((8, W), lambda p, c: (0, p))],
        out_specs=[spec, spec, spec2, spec, spec, spec, spec, spec, gspec],
        out_shape=[bf, bf, jax.ShapeDtypeStruct((2 * T, D), BF16), bf, bf, f32, f32, f32,
                   jax.ShapeDtypeStruct((T // L, 1, D), F32)],
        compiler_params=_cparams(("parallel", "parallel")),
        name="rwkv_prep",
    )(r, k, v, lw, la, par)
    y = pl.pallas_call(
        functools.partial(_rwkv_state_kernel, chunk=L),
        grid=(P, T // tb),
        in_specs=[spec, spec, spec2, spec, spec, spec, spec, gspec, spec],
        out_specs=spec,
        out_shape=f32,
        scratch_shapes=[pltpu.VMEM((W // gw, gw, gw), F32)],
        compiler_params=_cparams(("parallel", "arbitrary")),
        name="rwkv_state",
    )(at, rt, bk, tc, rb, akv, rkv, gl, v)
    ln = jnp.stack([lnx_w, lnx_b]).astype(F32)
    ln = jnp.concatenate([ln, jnp.zeros((6, D), F32)], axis=0)
    return pl.pallas_call(
        _rwkv_out_kernel,
        grid=(P, T // tb),
        in_specs=[spec, spec, spec, pl.BlockSpec((8, W), lambda p, c: (0, p))],
        out_specs=spec,
        out_shape=bf,
        compiler_params=_cparams(("parallel", "parallel")),
        name="rwkv_out",
    )(y, bon, g, ln)


def _mixer_even(h, norm_g, w_in, j, b_i, b_f, mlstm_g, lb_logits, hgrn_g, w_out):
    D = h.shape[1]
    H = MLSTM_HEADS
    mix_a = D // 2
    dv = mix_a // H
    dk = dv // 2
    na = 2 * H * dk + 2 * mix_a
    hn = _rmsnorm(h, norm_g, BF16)
    w_if = jnp.zeros((D, LANES), F32).at[:, :2 * H].set(w_in[j, :, na:na + 2 * H])
    w_b = w_in[j, :, na + 2 * H:]
    proj_a = _mm(hn, w_in, layer=j, cols=(0, na), name="mm_mlstm_in")
    gates = _mm(hn, w_if, name="mm_gates")
    proj_b = _mm(hn, w_b, name="mm_hgrn_in")
    bias_row = jnp.zeros((1, LANES), F32).at[0, :H].set(b_i.astype(F32)).at[0, H:2 * H].set(b_f.astype(F32))
    y = _mlstm(proj_a, gates, bias_row, mlstm_g, dk, dv, D)
    y = _hgrn(proj_b, lb_logits, j, hgrn_g, y)
    return _mm(y, w_out, layer=j, res=h, name="mm_mix_out")


def _mixer_odd(h, norm_g, j, mu, w0, w1, w2, a0, a1, a2, g1, g2, k_k, k_a, r_k, w_r, w_k, w_v, w_o, lnx_w, lnx_b):
    s_r, s_w, s_k, s_v, s_a, s_g = _rwkv_shift(h, norm_g, mu[j])
    r = _mm(s_r, w_r, layer=j, name="mm_rwkv_r")
    k = _mm(s_k, w_k, layer=j, name="mm_rwkv_k")
    v = _mm(s_v, w_v, layer=j, name="mm_rwkv_v")
    lw = _mm(_mm(s_w, w1, layer=j, act="tanh", out_dtype=BF16, name="mm_lora_w1"), w2, layer=j, name="mm_lora_w2")
    la = _mm(_mm(s_a, a1, layer=j, out_dtype=BF16, name="mm_lora_a1"), a2, layer=j, name="mm_lora_a2")
    g = _mm(_mm(s_g, g1, layer=j, act="sigmoid", out_dtype=BF16, name="mm_lora_g1"), g2, layer=j, name="mm_lora_g2")
    y = _rwkv_core(r, k, v, lw, la, g, w0[j], a0[j], k_k[j], k_a[j], r_k[j], lnx_w[j], lnx_b[j])
    return _mm(y, w_o, layer=j, res=h, name="mm_rwkv_out")


def kernel(x, mem, norm_mix_g, norm_xattn_g, norm_mlp_g, final_norm_g, mem_norm_g, ab_w_in, mlstm_b_i, mlstm_b_f, mlstm_norm_g, hgrn_lb_logits, hgrn_norm_g, ab_w_out, rwkv_mu, rwkv_w0, rwkv_w1, rwkv_w2, rwkv_a0, rwkv_a1, rwkv_a2, rwkv_g1, rwkv_g2, rwkv_k_k, rwkv_k_a, rwkv_r_k, rwkv_w_r, rwkv_w_k, rwkv_w_v, rwkv_w_o, rwkv_lnx_w, rwkv_lnx_b, xattn_w_q, xattn_w_o, mem_w_kv, mlp_w_up, mlp_w_down):
    B, T, D = x.shape
    depth = norm_mix_g.shape[0]
    M = mem.shape[1]
    xscale = (D // XATTN_HEADS) ** -0.5
    outs = []
    for b in range(B):
        mem_kv = _mm(_rmsnorm(mem[b], mem_norm_g, BF16), mem_w_kv, out_dtype=BF16, name="mm_mem_kv")
        h = x[b]
        for layer in range(depth):
            j = layer // 2
            if layer % 2 == 0:
                h = _mixer_even(h, norm_mix_g[layer], ab_w_in, j, mlstm_b_i[j], mlstm_b_f[j], mlstm_norm_g[j],
                                hgrn_lb_logits, hgrn_norm_g[j], ab_w_out)
            else:
                h = _mixer_odd(h, norm_mix_g[layer], j, rwkv_mu, rwkv_w0, rwkv_w1, rwkv_w2, rwkv_a0, rwkv_a1,
                               rwkv_a2, rwkv_g1, rwkv_g2, rwkv_k_k, rwkv_k_a, rwkv_r_k,
                               rwkv_w_r, rwkv_w_k, rwkv_w_v, rwkv_w_o, rwkv_lnx_w, rwkv_lnx_b)
            wqk, vwo = _xattn_weights(xattn_w_q, xattn_w_o, layer, mem_kv)
            p = _mm(_rmsnorm(h, norm_xattn_g[layer], BF16), wqk, act="softmax", seg=(xscale, M),
                    out_dtype=BF16, tn=XATTN_HEADS * M, name="mm_xattn_scores")
            h = _mm(p, vwo, res=h, name="mm_xattn_out")
            up = _mm(_rmsnorm(h, norm_mlp_g[layer], BF16), mlp_w_up, layer=layer,
                     act="relu2", out_dtype=BF16, name="mm_mlp_up")
            h = _mm(up, mlp_w_down, layer=layer, res=h, name="mm_mlp_down")
        outs.append(_rmsnorm(h, final_norm_g, F32))
    return jnp.stack(outs)
```

```python
import functools
import math

import jax
import jax.numpy as jnp
from jax import lax
from jax.experimental import pallas as pl
from jax.experimental.pallas import tpu as pltpu

F32 = jnp.float32
BF16 = jnp.bfloat16

NORM_EPS = 1e-6
GATE_CAP = 15.0
MLSTM_HEADS = 4
MLSTM_CHUNK = 64
HGRN_HEAD_DIM = 128
HGRN_CHUNK = 16
RWKV_HEAD = 64
RWKV_CHUNK = 64
RWKV_LNX_EPS = 64e-5
RWKV_DECAY_SCALE = math.exp(-0.5)
XATTN_HEADS = 4
LANES = 128
MXU_DIM = 256
STEP_LANES = 2048
VMEM_LIMIT = 56 * 1024 * 1024

NT = (((1,), (1,)), ((), ()))
TN = (((0,), (0,)), ((), ()))


def _cparams(sem):
    return pltpu.CompilerParams(dimension_semantics=sem, vmem_limit_bytes=VMEM_LIMIT)


def _sigmoid(x):
    return 1.0 / (1.0 + jnp.exp(-x))


def _log_sigmoid(x):
    return jnp.minimum(x, 0.0) - jnp.log1p(jnp.exp(-jnp.abs(x)))


def _pick(dim, pref):
    t = min(dim, pref)
    while dim % t:
        t //= 2
    return t


def _head_masks(rows, width, head):
    lane = lax.broadcasted_iota(jnp.int32, (rows, width), 1) // head
    return [lane == h for h in range(width // head)]


def _stack_heads(x, masks):
    z = jnp.zeros_like(x)
    return jnp.concatenate([jnp.where(m, x, z) for m in masks], axis=0)


_BDOT = {"nn": "gmk,gkn->gmn", "nt": "gmk,gnk->gmn", "tn": "gtm,gtn->gmn"}


def _bdot(xs, ys, mode):
    d = jnp.einsum(_BDOT[mode], jnp.stack(xs), jnp.stack(ys), preferred_element_type=F32)
    return [d[g] for g in range(len(xs))]


def _same_head(n, head):
    r = lax.broadcasted_iota(jnp.int32, (n, n), 0) // head
    c = lax.broadcasted_iota(jnp.int32, (n, n), 1) // head
    return r == c


def _mm_kernel(*refs, nk, act, has_res, seg):
    if has_res:
        a_ref, b_ref, res_ref, o_ref = refs[:4]
        rest = refs[4:]
    else:
        a_ref, b_ref, o_ref = refs[:3]
        res_ref = None
        rest = refs[3:]

    def finish(r):
        if act == "relu2":
            r = jnp.square(jnp.maximum(r, 0.0))
        elif act == "tanh":
            r = jnp.tanh(r)
        elif act == "sigmoid":
            r = _sigmoid(r)
        elif act == "softmax":
            scale, width = seg
            parts = []
            for j in range(r.shape[1] // width):
                s = r[:, j * width:(j + 1) * width] * scale
                e = jnp.exp(s - jnp.max(s, axis=-1, keepdims=True))
                parts.append(e / jnp.sum(e, axis=-1, keepdims=True))
            r = jnp.concatenate(parts, axis=1)
        if has_res:
            r = r + res_ref[...].astype(F32)
        o_ref[...] = r.astype(o_ref.dtype)

    part = jnp.dot(a_ref[...].astype(BF16), b_ref[...].astype(BF16), preferred_element_type=F32)
    if nk == 1:
        finish(part)
    else:
        acc_ref = rest[0]
        k = pl.program_id(2)

        @pl.when(k == 0)
        def _():
            acc_ref[...] = part

        @pl.when(k > 0)
        def _():
            acc_ref[...] += part

        @pl.when(k == nk - 1)
        def _():
            finish(acc_ref[...])


def _mm(a, b, *, out_dtype=F32, act=None, res=None, seg=None, tm=1024, tn=512, tk=4096, name="mm",
        layer=None, cols=None, krange=None):
    M, Ka = a.shape
    K2, nb = b.shape[-2:]
    assert Ka == K2 and (b.ndim == 3) == (layer is not None)
    c0, N = cols if cols is not None else (0, nb)
    kf, K = krange if krange is not None else (0, Ka)
    tm, tn, tk = _pick(M, tm), _pick(N, tn), _pick(K, tk)
    assert c0 % tn == 0 and kf % tk == 0
    j0 = c0 // tn
    k0 = kf // tk
    nk = K // tk
    if layer is None:
        b_spec = pl.BlockSpec((tk, tn), lambda i, j, k: (k0 + k, j0 + j))
    else:
        b_spec = pl.BlockSpec((None, tk, tn), lambda i, j, k: (layer, k0 + k, j0 + j))
    in_specs = [pl.BlockSpec((tm, tk), lambda i, j, k: (i, k0 + k)), b_spec]
    args = [a, b]
    if res is not None:
        in_specs.append(pl.BlockSpec((tm, tn), lambda i, j, k: (i, j)))
        args.append(res)
    scratch = [] if nk == 1 else [pltpu.VMEM((tm, tn), F32)]
    return pl.pallas_call(
        functools.partial(_mm_kernel, nk=nk, act=act, has_res=res is not None, seg=seg),
        grid=(M // tm, N // tn, nk),
        in_specs=in_specs,
        out_specs=pl.BlockSpec((tm, tn), lambda i, j, k: (i, j)),
        out_shape=jax.ShapeDtypeStruct((M, N), out_dtype),
        scratch_shapes=scratch,
        compiler_params=_cparams(("parallel", "parallel", "arbitrary")),
        name=name,
    )(*args)


def _mm_blocks_kernel(a_ref, b_ref, o_ref, *, trans_b):
    a = a_ref[...].astype(BF16)
    b = b_ref[...].astype(BF16)
    if trans_b:
        r = lax.dot_general(a, b, NT, preferred_element_type=F32)
    else:
        r = jnp.dot(a, b, preferred_element_type=F32)
    o_ref[...] = r.astype(o_ref.dtype)


def _rmsnorm_kernel(x_ref, g_ref, o_ref):
    x = x_ref[...].astype(F32)
    y = x * lax.rsqrt(jnp.mean(x * x, axis=-1, keepdims=True) + NORM_EPS)
    o_ref[...] = (y * g_ref[...]).astype(o_ref.dtype)


def _rmsnorm(x, g, out_dtype, tm=256):
    M, D = x.shape
    tm = _pick(M, tm)
    return pl.pallas_call(
        _rmsnorm_kernel,
        grid=(M // tm,),
        in_specs=[pl.BlockSpec((tm, D), lambda i: (i, 0)),
                  pl.BlockSpec((1, D), lambda i: (0, 0))],
        out_specs=pl.BlockSpec((tm, D), lambda i: (i, 0)),
        out_shape=jax.ShapeDtypeStruct((M, D), out_dtype),
        compiler_params=_cparams(("parallel",)),
        name="rmsnorm",
    )(x, g.reshape(1, D).astype(F32))


def _mlstm_kernel(q_ref, k_ref, v_ref, og_ref, gates_ref, bias_ref, ng_ref, o_ref,
                  c_ref, n_ref, m_ref, *, heads):
    H = heads
    L = q_ref.shape[0]
    dk = q_ref.shape[1] // H
    dv = v_ref.shape[1] // H

    @pl.when(pl.program_id(0) == 0)
    def _():
        c_ref[...] = jnp.zeros_like(c_ref)
        n_ref[...] = jnp.zeros_like(n_ref)
        m_ref[...] = jnp.zeros_like(m_ref)

    pre = gates_ref[...] + bias_ref[...]
    capped = GATE_CAP * jnp.tanh(pre / GATE_CAP)
    lsig = _log_sigmoid(capped)
    lane = lax.broadcasted_iota(jnp.int32, capped.shape, 1)
    row = lax.broadcasted_iota(jnp.int32, (L, L), 0)
    col = lax.broadcasted_iota(jnp.int32, (L, L), 1)
    eye = row == col
    lower = col <= row

    def gate_terms(h):
        li_col = jnp.sum(jnp.where(lane == h, capped, 0.0), axis=1, keepdims=True)
        lf_col = jnp.sum(jnp.where(lane == h + H, lsig, 0.0), axis=1, keepdims=True)
        lf_row = jnp.sum(jnp.where(eye, lf_col, 0.0), axis=0, keepdims=True)
        li_row = jnp.sum(jnp.where(eye, li_col, 0.0), axis=0, keepdims=True)
        b_col = jnp.sum(jnp.where(lower, lf_row, 0.0), axis=1, keepdims=True)
        b_row = jnp.sum(jnp.where(row <= col, lf_col, 0.0), axis=0, keepdims=True)
        g_row = li_row - b_row
        g_col = li_col - b_col
        m_prev = m_ref[h][:, 0:1]
        a_col = jnp.maximum(m_prev, jnp.max(jnp.where(lower, g_row, -jnp.inf), axis=1, keepdims=True))
        a_last = jnp.max(a_col, axis=0, keepdims=True)
        b_last = jnp.sum(lf_col, axis=0, keepdims=True)
        return dict(dmat=jnp.where(lower, jnp.exp(g_row - a_col), 0.0), g_inter=jnp.exp(m_prev - a_col),
                    floor=jnp.exp(-(b_col + a_col)), w_col=jnp.exp(g_col - a_last),
                    decay=jnp.exp(m_prev - a_last), m_new=b_last + a_last)

    hs = range(H)
    gt = [gate_terms(h) for h in hs]
    q = [q_ref[:, h * dk:(h + 1) * dk] * (dk ** -0.5) for h in hs]
    k = [k_ref[:, h * dk:(h + 1) * dk] for h in hs]
    vb = [v_ref[:, h * dv:(h + 1) * dv].astype(BF16) for h in hs]
    qb = [x.astype(BF16) for x in q]
    cst = [c_ref[h] for h in hs]
    s = _bdot(qb, [x.astype(BF16) for x in k], "nt")
    s = [s[h] * gt[h]["dmat"] for h in hs]
    qc = _bdot(qb, [c.astype(BF16) for c in cst], "nn")
    sv = _bdot([x.astype(BF16) for x in s], vb, "nn")
    kw = [k[h] * gt[h]["w_col"] for h in hs]
    upd = _bdot([x.astype(BF16) for x in kw], vb, "tn")
    for h in hs:
        g = gt[h]
        num = sv[h] + g["g_inter"] * qc[h]
        qn = jnp.sum(q[h] * n_ref[h], axis=1, keepdims=True)
        den = jnp.sum(s[h], axis=1, keepdims=True) + g["g_inter"] * qn
        hh = num / jnp.maximum(jnp.abs(den), g["floor"])
        c_ref[h] = g["decay"] * cst[h] + upd[h]
        n_ref[h] = g["decay"] * n_ref[h] + jnp.sum(kw[h], axis=0, keepdims=True)
        m_ref[h] = jnp.broadcast_to(g["m_new"], m_ref.shape[1:])
        y = hh * lax.rsqrt(jnp.mean(hh * hh, axis=-1, keepdims=True) + NORM_EPS)
        cs = slice(h * dv, (h + 1) * dv)
        o_ref[:, cs] = (y * ng_ref[:, cs] * _sigmoid(og_ref[:, cs])).astype(o_ref.dtype)


def _mlstm(proj, gates, bias_row, norm_g, dk, dv, out_width):
    T = proj.shape[0]
    H = MLSTM_HEADS
    L = MLSTM_CHUNK
    wk, wv = H * dk, H * dv
    assert (2 * wk) % wv == 0
    v0 = (2 * wk) // wv
    return pl.pallas_call(
        functools.partial(_mlstm_kernel, heads=H),
        grid=(T // L,),
        in_specs=[pl.BlockSpec((L, wk), lambda c: (c, 0)),
                  pl.BlockSpec((L, wk), lambda c: (c, 1)),
                  pl.BlockSpec((L, wv), lambda c: (c, v0)),
                  pl.BlockSpec((L, wv), lambda c: (c, v0 + 1)),
                  pl.BlockSpec((L, LANES), lambda c: (c, 0)),
                  pl.BlockSpec((1, LANES), lambda c: (0, 0)),
                  pl.BlockSpec((1, wv), lambda c: (0, 0))],
        out_specs=pl.BlockSpec((L, wv), lambda c: (c, 0)),
        out_shape=jax.ShapeDtypeStruct((T, out_width), BF16),
        scratch_shapes=[pltpu.VMEM((H, dk, dv), F32), pltpu.VMEM((H, 1, dk), F32), pltpu.VMEM((H, 1, LANES), F32)],
        compiler_params=_cparams(("arbitrary",)),
        name="mlstm",
    )(proj, proj, proj, proj, gates, bias_row, norm_g.reshape(1, wv).astype(F32))


def _hgrn_kernel(q_ref, f_ref, i_ref, g_ref, lb_ref, ng_ref, y_hbm_ref, o_ref, st_ref, *, lb_row, chunk):
    TB = q_ref.shape[0]
    dk = HGRN_HEAD_DIM
    GW = st_ref.shape[1]
    NG = q_ref.shape[1] // GW
    HG = GW // dk
    C = chunk

    @pl.when(pl.program_id(1) == 0)
    def _():
        st_ref[...] = jnp.zeros_like(st_ref)

    logits = lb_ref[...].astype(F32)
    e = jnp.exp(logits - jnp.max(logits, axis=0, keepdims=True))
    p = e / jnp.sum(e, axis=0, keepdims=True)
    lb_all = jnp.sum(p[0:lb_row + 1], axis=0, keepdims=True)
    ng_all = ng_ref[...]
    srow = lax.broadcasted_iota(jnp.int32, (C, GW), 0)
    srow_h = srow[:, :dk]
    same = _same_head(GW, dk)

    def elementwise(g, ins):
        qb, fb, vb = ins
        lb = lb_all[:, g * GW:(g + 1) * GW]
        f = lb + (1.0 - lb) * _sigmoid(fb)
        kk = 1.0 - f
        qh = qb * _sigmoid(qb)
        A = jnp.log(f)
        sh = 1
        while sh < C:
            A = A + jnp.where(srow >= sh, pltpu.roll(A, sh, axis=0), 0.0)
            sh *= 2
        a_last = A[C - 1:C]
        intra = []
        for h in range(HG):
            hs = slice(h * dk, (h + 1) * dk)
            Ah, kh, qhh, vh = A[:, hs], kk[:, hs], qh[:, hs], vb[:, hs]
            rows = []
            for t in range(C):
                rel = jnp.where(srow_h <= t, jnp.exp(Ah[t:t + 1] - Ah), 0.0)
                sc = jnp.sum(rel * kh * qhh[t:t + 1], axis=1, keepdims=True)
                rows.append(jnp.sum(sc * vh, axis=0, keepdims=True))
            intra.append(jnp.concatenate(rows, axis=0))
        qe = (qh * jnp.exp(A)).astype(BF16)
        ke = (kk * jnp.exp(a_last - A)).astype(BF16)
        return qe, ke, vb.astype(BF16), jnp.concatenate(intra, axis=1), jnp.exp(a_last)

    def body(ci, carry):
        rs = pl.ds(pl.multiple_of(ci * C, C), C)
        G = range(NG)
        gsl = [slice(g * GW, (g + 1) * GW) for g in G]
        st = [st_ref[g] for g in G]
        ew = [elementwise(g, tuple(ref[rs, gsl[g]] for ref in (q_ref, f_ref, i_ref))) for g in G]
        o_inter = _bdot([e[0] for e in ew], [s.astype(BF16) for s in st], "nt")
        upd = _bdot([e[2] for e in ew], [e[1] for e in ew], "tn")
        for g in G:
            st_ref[g] = st[g] * ew[g][4] + (upd[g] if HG == 1 else jnp.where(same, upd[g], 0.0))
            o = o_inter[g] + ew[g][3]
            ys = []
            for h in range(HG):
                oh = o[:, h * dk:(h + 1) * dk]
                ys.append(oh * lax.rsqrt(jnp.mean(oh * oh, axis=-1, keepdims=True) + NORM_EPS))
            gb = g_ref[rs, gsl[g]]
            o_ref[rs, gsl[g]] = (jnp.concatenate(ys, axis=1) * ng_all[:, gsl[g]]
                                 * (gb * _sigmoid(gb))).astype(o_ref.dtype)
        return carry

    lax.fori_loop(0, TB // C, body, 0)


def _hgrn(proj, lb_logits, lb_row, norm_g, y, tb=256):
    T = proj.shape[0]
    W = proj.shape[1] // 4
    sw = _pick(W, STEP_LANES)
    gw = HGRN_HEAD_DIM
    H = W // sw
    tb = _pick(T, tb)
    R = lb_logits.shape[0]
    assert (y.shape[1] - W) % sw == 0
    off = (y.shape[1] - W) // sw
    return pl.pallas_call(
        functools.partial(_hgrn_kernel, lb_row=lb_row, chunk=HGRN_CHUNK),
        grid=(H, T // tb),
        in_specs=[pl.BlockSpec((tb, sw), lambda h, c: (c, h)),
                  pl.BlockSpec((tb, sw), lambda h, c: (c, H + h)),
                  pl.BlockSpec((tb, sw), lambda h, c: (c, 2 * H + h)),
                  pl.BlockSpec((tb, sw), lambda h, c: (c, 3 * H + h)),
                  pl.BlockSpec((R, sw), lambda h, c: (0, h)),
                  pl.BlockSpec((1, sw), lambda h, c: (0, h)),
                  pl.BlockSpec(memory_space=pl.ANY)],
        out_specs=pl.BlockSpec((tb, sw), lambda h, c: (c, off + h)),
        out_shape=jax.ShapeDtypeStruct(y.shape, y.dtype),
        input_output_aliases={6: 0},
        scratch_shapes=[pltpu.VMEM((sw // gw, gw, gw), F32)],
        compiler_params=_cparams(("parallel", "arbitrary")),
        name="hgrn2",
    )(proj, proj, proj, proj, lb_logits.astype(F32), norm_g.reshape(1, W).astype(F32), y)


def _xattn_weights(w_q, w_o, layer, mem_kv):
    D = w_q.shape[1]
    M = mem_kv.shape[0]
    H = XATTN_HEADS
    d = D // H
    tm = _pick(D, 1024)
    wqk = pl.pallas_call(
        functools.partial(_mm_blocks_kernel, trans_b=True),
        grid=(D // tm, H),
        in_specs=[pl.BlockSpec((None, tm, d), lambda i, h: (layer, i, h)),
                  pl.BlockSpec((M, d), lambda i, h: (0, h))],
        out_specs=pl.BlockSpec((tm, M), lambda i, h: (i, h)),
        out_shape=jax.ShapeDtypeStruct((D, H * M), BF16),
        compiler_params=_cparams(("parallel", "parallel")),
        name="xattn_wqk",
    )(w_q, mem_kv)
    tn = _pick(D, 1024)
    vwo = pl.pallas_call(
        functools.partial(_mm_blocks_kernel, trans_b=False),
        grid=(H, D // tn),
        in_specs=[pl.BlockSpec((M, d), lambda h, j: (0, H + h)),
                  pl.BlockSpec((None, d, tn), lambda h, j: (layer, h, j))],
        out_specs=pl.BlockSpec((M, tn), lambda h, j: (h, j)),
        out_shape=jax.ShapeDtypeStruct((H * M, D), BF16),
        compiler_params=_cparams(("parallel", "parallel")),
        name="xattn_vwo",
    )(mem_kv, w_o)
    return wqk, vwo


def _shift_kernel(h_ref, prev_ref, g_ref, mu_ref, *o_refs):
    def norm(h):
        return h * lax.rsqrt(jnp.mean(h * h, axis=-1, keepdims=True) + NORM_EPS) * g_ref[...]

    x = norm(h_ref[...])
    n = prev_ref.shape[0]
    last = jnp.where(pl.program_id(0) == 0, 0.0, norm(prev_ref[...])[n - 1:n, :])
    row = lax.broadcasted_iota(jnp.int32, x.shape, 0)
    xx = jnp.where(row == 0, last, pltpu.roll(x, 1, axis=0)) - x
    for j, o_ref in enumerate(o_refs):
        o_ref[...] = (x + xx * mu_ref[j:j + 1, :]).astype(o_ref.dtype)


def _rwkv_shift(h, g, mu, tm=256):
    T, D = h.shape
    J = mu.shape[0]
    tm = _pick(T, tm)
    pr = 8
    spec = pl.BlockSpec((tm, D), lambda i: (i, 0))
    return pl.pallas_call(
        _shift_kernel,
        grid=(T // tm,),
        in_specs=[spec,
                  pl.BlockSpec((pr, D), lambda i: (jnp.maximum(i * (tm // pr) - 1, 0), 0)),
                  pl.BlockSpec((1, D), lambda i: (0, 0)),
                  pl.BlockSpec((J, D), lambda i: (0, 0))],
        out_specs=[spec] * J,
        out_shape=[jax.ShapeDtypeStruct((T, D), BF16)] * J,
        compiler_params=_cparams(("parallel",)),
        name="rwkv_shift",
    )(h, h, g.reshape(1, D).astype(F32), mu.astype(F32))


def _split2(x):
    x1 = x.astype(BF16)
    return x1, (x - x1.astype(F32)).astype(BF16)


def _dot_exact_rhs(x, m):
    n = x.shape[0]
    d = jnp.dot(jnp.concatenate(_split2(x), axis=0), m, preferred_element_type=F32)
    return d[:n] + d[n:]


def _head_ones(n, head, scale, dtype):
    return jnp.where(_same_head(n, head), scale, 0.0).astype(dtype)


def _rwkv_prep_kernel(r_ref, k_ref, v_ref, lw_ref, la_ref, par_ref,
                      at_ref, rt_ref, bk_ref, tc_ref, rb_ref, akv_ref, rkv_ref, bon_ref, gl_ref, *, chunk):
    TB = r_ref.shape[0]
    L = chunk
    N = RWKV_HEAD
    GW = 4 * L
    HG = GW // N
    NG = r_ref.shape[1] // GW
    assert L == N and GW == MXU_DIM
    par = par_ref[...]
    seg1 = _head_ones(GW, N, 1.0, BF16)
    hmask = _head_masks(L, GW, N)
    lane = lax.broadcasted_iota(jnp.int32, (L, GW), 1)
    rowi = lax.broadcasted_iota(jnp.int32, (L, GW), 0)
    half = GW // 2
    lane_h = lax.broadcasted_iota(jnp.int32, (L, half), 1)
    row_h = lax.broadcasted_iota(jnp.int32, (L, half), 0)
    low = lane_h < N
    src = jnp.where(low, lane_h, lane_h - N)
    strict = src < row_h
    incl = src <= row_h
    eye_half = jnp.where(src == row_h, 1.0, 0.0)
    eye_cat = jnp.concatenate([eye_half, eye_half], axis=1)

    def elementwise(g, ins):
        gs = slice(g * GW, (g + 1) * GW)
        w0, a0, k_k, k_a, r_k = (par[i:i + 1, gs] for i in range(5))
        r, k, v, lw, la = ins
        logw = -RWKV_DECAY_SCALE * _sigmoid(w0 + lw)
        a_sig = _sigmoid(a0 + la)
        kkr = k * k_k
        k2 = k * (1.0 + (a_sig - 1.0) * k_a)
        cs = logw
        sh = 1
        while sh < L:
            cs = cs + jnp.where(rowi >= sh, pltpu.roll(cs, sh, axis=0), 0.0)
            sh *= 2
        return r, v, logw, a_sig, kkr, k2, cs, r * k2 * r_k

    def pre_dots(ew, ss, bsum):
        r, v, logw, a_sig, kkr, k2, cs, _ = ew
        kk = kkr / jnp.maximum(jnp.sqrt(ss), 1e-12)
        ginv = jnp.exp(-cs)
        atb = (-kk * jnp.exp(cs - logw)).astype(BF16)
        rtb = (r * jnp.exp(cs)).astype(BF16)
        bt = (kk * a_sig * ginv).astype(BF16)
        kt = (k2 * ginv).astype(BF16)
        zero = jnp.zeros_like(atb)
        lhs = jnp.concatenate([jnp.where(m, x, zero) for m in hmask for x in (atb, rtb)], axis=0)
        return atb, rtb, bt, kt, bsum * v, jnp.exp(cs[L - 1:L]), lhs, jnp.concatenate([bt, kt, kt, bt], axis=0)

    def cat(parts, even_sel, odd_sel, mask):
        cols = [jnp.where(mask, jnp.where(low, even_sel(parts[2 * c]), odd_sel(parts[2 * c + 1])), 0.0)
                for c in range(HG // 2)]
        return jnp.concatenate(cols, axis=1)

    lo = lambda x: x[:, :half]
    hi = lambda x: x[:, half:]

    def body(ci, carry):
        r0 = pl.multiple_of(ci * L, L)
        sl = pl.ds(r0, L)
        G = range(NG)
        ews = [elementwise(g, tuple(ref[sl, g * GW:(g + 1) * GW]
                                    for ref in (r_ref, k_ref, v_ref, lw_ref, la_ref))) for g in G]
        sums = _dot_exact_rhs(jnp.concatenate([x for ew in ews for x in (ew[4] * ew[4], ew[7])], axis=0), seg1)
        pre = [pre_dots(ews[g], sums[2 * g * L:(2 * g + 1) * L], sums[(2 * g + 1) * L:(2 * g + 2) * L]) for g in G]
        P = _bdot([p[6] for p in pre], [p[7] for p in pre], "nt")
        pat = [[P[g][2 * h * L:(2 * h + 1) * L] for h in range(HG)] for g in G]
        prt = [[P[g][(2 * h + 1) * L:(2 * h + 2) * L] for h in range(HG)] for g in G]
        ncat = [cat(pat[g], lo, hi, strict) for g in G]
        nb = [n.astype(BF16) for n in ncat]
        pw = _bdot(nb, [_stack_heads(n, hmask) for n in nb], "nn")
        tinv = [eye_cat + n for n in ncat]
        power = 2
        while power < L:
            pwb = [x.astype(BF16) for x in pw]
            pst = [_stack_heads(x, hmask) for x in pwb]
            if 2 * power < L:
                d = _bdot([jnp.concatenate([tinv[g].astype(BF16), pwb[g]], axis=0) for g in G], pst, "nn")
                tinv = [tinv[g] + d[g][:L] for g in G]
                pw = [d[g][L:] for g in G]
            else:
                d = _bdot([t.astype(BF16) for t in tinv], pst, "nn")
                tinv = [tinv[g] + d[g] for g in G]
            power *= 2
        akrk = [jnp.concatenate([cat(pat[g], hi, lo, strict), cat(prt[g], hi, lo, incl)], axis=0).astype(BF16)
                for g in G]
        kv = _bdot(akrk, [_stack_heads(ews[g][1].astype(BF16), hmask) for g in G], "nn")
        for g in G:
            gs = slice(g * GW, (g + 1) * GW)
            atb, rtb, bt, kt, bon, gl = pre[g][:6]
            at_ref[sl, gs] = atb
            rt_ref[sl, gs] = rtb
            bk_ref[pl.ds(pl.multiple_of(2 * r0, 2 * L), L), gs] = bt
            bk_ref[pl.ds(pl.multiple_of(2 * r0 + L, L), L), gs] = kt
            tc_ref[sl, gs] = tinv[g].astype(BF16)
            rb_ref[sl, gs] = cat(prt[g], lo, hi, incl).astype(BF16)
            akv_ref[sl, gs] = kv[g][:L]
            rkv_ref[sl, gs] = kv[g][L:]
            bon_ref[sl, gs] = bon
            gl_ref[ci, :, gs] = gl
        return carry

    lax.fori_loop(0, TB // L, body, 0)


def _rwkv_state_kernel(at_ref, rt_ref, bk_ref, tc_ref, rb_ref, akv_ref, rkv_ref, gl_ref, v_ref,
                       y_ref, ht_ref, *, chunk):
    TB = v_ref.shape[0]
    L = chunk
    N = RWKV_HEAD
    GW = ht_ref.shape[1]
    NG = v_ref.shape[1] // GW

    @pl.when(pl.program_id(1) == 0)
    def _():
        ht_ref[...] = jnp.zeros_like(ht_ref)

    hmask = _head_masks(L, GW, N)
    same = _same_head(GW, N)

    def body(ci, carry):
        r0 = pl.multiple_of(ci * L, L)
        sl = pl.ds(r0, L)
        sl2 = pl.ds(pl.multiple_of(2 * r0, 2 * L), 2 * L)
        G = range(NG)
        gsl = [slice(g * GW, (g + 1) * GW) for g in G]
        ht = [ht_ref[g] for g in G]
        xr = _bdot([jnp.concatenate([at_ref[sl, gs], rt_ref[sl, gs]], axis=0) for gs in gsl],
                   [h.astype(BF16) for h in ht], "nt")
        xb = [(xr[g][:L] + akv_ref[sl, gsl[g]]).astype(BF16) for g in G]
        u = _bdot([tc_ref[sl, gs] for gs in gsl], [_stack_heads(x, hmask) for x in xb], "nn")
        ub = [x.astype(BF16) for x in u]
        yd = _bdot([rb_ref[sl, gs] for gs in gsl], [_stack_heads(x, hmask) for x in ub], "nn")
        dht = _bdot([jnp.concatenate([ub[g], v_ref[sl, gsl[g]].astype(BF16)], axis=0) for g in G],
                    [bk_ref[sl2, gs] for gs in gsl], "tn")
        for g in G:
            ht_ref[g] = (ht[g] + jnp.where(same, dht[g], 0.0)) * gl_ref[ci, :, gsl[g]]
            y_ref[sl, gsl[g]] = xr[g][L:] + yd[g] + rkv_ref[sl, gsl[g]]
        return carry

    lax.fori_loop(0, TB // L, body, 0)


def _rwkv_out_kernel(y_ref, bon_ref, g_ref, ln_ref, o_ref):
    N = RWKV_HEAD
    GW = MXU_DIM
    seg_mean = _head_ones(GW, N, 1.0 / N, BF16)
    for g in range(y_ref.shape[1] // GW):
        gs = slice(g * GW, (g + 1) * GW)
        y = y_ref[:, gs]
        yc = y - _dot_exact_rhs(y, seg_mean)
        var = _dot_exact_rhs(yc * yc, seg_mean)
        yn = yc * lax.rsqrt(var + RWKV_LNX_EPS) * ln_ref[0:1, gs] + ln_ref[1:2, gs]
        o_ref[:, gs] = ((yn + bon_ref[:, gs]) * g_ref[:, gs]).astype(o_ref.dtype)


def _rwkv_core(r, k, v, lw, la, g, w0, a0, k_k, k_a, r_k, lnx_w, lnx_b, tb=128):
    T, D = r.shape
    L = RWKV_CHUNK
    gw = MXU_DIM
    W = _pick(D, STEP_LANES)
    assert W % gw == 0
    P = D // W
    tb = _pick(T, tb)
    nc = tb // L
    par = jnp.stack([w0, a0, k_k, k_a, r_k.reshape(D)]).astype(F32)
    par = jnp.concatenate([par, jnp.zeros((3, D), F32)], axis=0)
    spec = pl.BlockSpec((tb, W), lambda p, c: (c, p))
    spec2 = pl.BlockSpec((2 * tb, W), lambda p, c: (c, p))
    gspec = pl.BlockSpec((nc, 1, W), lambda p, c: (c, 0, p))
    bf = jax.ShapeDtypeStruct((T, D), BF16)
    f32 = jax.ShapeDtypeStruct((T, D), F32)
    at, rt, bk, tc, rb, akv, rkv, bon, gl = pl.pallas_call(
        functools.partial(_rwkv_prep_kernel, chunk=L),
        grid=(P, T // tb),
        in_specs=[spec] * 5 + [pl.BlockSpec((8, W), lambda p, c: (0, p))],
        out_specs=[spec, spec, spec2, spec, spec, spec, spec, spec, gspec],
        out_shape=[bf, bf, jax.ShapeDtypeStruct((2 * T, D), BF16), bf, bf, f32, f32, f32,
                   jax.ShapeDtypeStruct((T // L, 1, D), F32)],
        compiler_params=_cparams(("parallel", "parallel")),
        name="rwkv_prep",
    )(r, k, v, lw, la, par)
    y = pl.pallas_call(
        functools.partial(_rwkv_state_kernel, chunk=L),
        grid=(P, T // tb),
        in_specs=[spec, spec, spec2, spec, spec, spec, spec, gspec, spec],
        out_specs=spec,
        out_shape=f32,
        scratch_shapes=[pltpu.VMEM((W // gw, gw, gw), F32)],
        compiler_params=_cparams(("parallel", "arbitrary")),
        name="rwkv_state",
    )(at, rt, bk, tc, rb, akv, rkv, gl, v)
    ln = jnp.stack([lnx_w, lnx_b]).astype(F32)
    ln = jnp.concatenate([ln, jnp.zeros((6, D), F32)], axis=0)
    return pl.pallas_call(
        _rwkv_out_kernel,
        grid=(P, T // tb),
        in_specs=[spec, spec, spec, pl.BlockSpec((8, W), lambda p, c: (0, p))],
        out_specs=spec,
        out_shape=bf,
        compiler_params=_cparams(("parallel", "parallel")),
        name="rwkv_out",
    )(y, bon, g, ln)


def _mixer_even(h, norm_g, w_in, j, b_i, b_f, mlstm_g, lb_logits, hgrn_g, w_out):
    D = h.shape[1]
    H = MLSTM_HEADS
    mix_a = D // 2
    dv = mix_a // H
    dk = dv // 2
    na = 2 * H * dk + 2 * mix_a
    hn = _rmsnorm(h, norm_g, BF16)
    w_if = jnp.zeros((D, LANES), F32).at[:, :2 * H].set(w_in[j, :, na:na + 2 * H])
    w_b = w_in[j, :, na + 2 * H:]
    proj_a = _mm(hn, w_in, layer=j, cols=(0, na), name="mm_mlstm_in")
    gates = _mm(hn, w_if, name="mm_gates")
    proj_b = _mm(hn, w_b, name="mm_hgrn_in")
    bias_row = jnp.zeros((1, LANES), F32).at[0, :H].set(b_i.astype(F32)).at[0, H:2 * H].set(b_f.astype(F32))
    y = _mlstm(proj_a, gates, bias_row, mlstm_g, dk, dv, D)
    y = _hgrn(proj_b, lb_logits, j, hgrn_g, y)
    return _mm(y, w_out, layer=j, res=h, name="mm_mix_out")


def _mixer_odd(h, norm_g, j, mu, w0, w1, w2, a0, a1, a2, g1, g2, k_k, k_a, r_k, w_r, w_k, w_v, w_o, lnx_w, lnx_b):
    s_r, s_w, s_k, s_v, s_a, s_g = _rwkv_shift(h, norm_g, mu[j])
    r = _mm(s_r, w_r, layer=j, name="mm_rwkv_r")
    k = _mm(s_k, w_k, layer=j, name="mm_rwkv_k")
    v = _mm(s_v, w_v, layer=j, name="mm_rwkv_v")
    lw = _mm(_mm(s_w, w1, layer=j, act="tanh", out_dtype=BF16, name="mm_lora_w1"), w2, layer=j, name="mm_lora_w2")
    la = _mm(_mm(s_a, a1, layer=j, out_dtype=BF16, name="mm_lora_a1"), a2, layer=j, name="mm_lora_a2")
    g = _mm(_mm(s_g, g1, layer=j, act="sigmoid", out_dtype=BF16, name="mm_lora_g1"), g2, layer=j, name="mm_lora_g2")
    y = _rwkv_core(r, k, v, lw, la, g, w0[j], a0[j], k_k[j], k_a[j], r_k[j], lnx_w[j], lnx_b[j])
    return _mm(y, w_o, layer=j, res=h, name="mm_rwkv_out")


def kernel(x, mem, norm_mix_g, norm_xattn_g, norm_mlp_g, final_norm_g, mem_norm_g, ab_w_in, mlstm_b_i, mlstm_b_f, mlstm_norm_g, hgrn_lb_logits, hgrn_norm_g, ab_w_out, rwkv_mu, rwkv_w0, rwkv_w1, rwkv_w2, rwkv_a0, rwkv_a1, rwkv_a2, rwkv_g1, rwkv_g2, rwkv_k_k, rwkv_k_a, rwkv_r_k, rwkv_w_r, rwkv_w_k, rwkv_w_v, rwkv_w_o, rwkv_lnx_w, rwkv_lnx_b, xattn_w_q, xattn_w_o, mem_w_kv, mlp_w_up, mlp_w_down):
    B, T, D = x.shape
    depth = norm_mix_g.shape[0]
    M = mem.shape[1]
    xscale = (D // XATTN_HEADS) ** -0.5
    outs = []
    for b in range(B):
        mem_kv = _mm(_rmsnorm(mem[b], mem_norm_g, BF16), mem_w_kv, out_dtype=BF16, name="mm_mem_kv")
        h = x[b]
        for layer in range(depth):
            j = layer // 2
            if layer % 2 == 0:
                h = _mixer_even(h, norm_mix_g[layer], ab_w_in, j, mlstm_b_i[j], mlstm_b_f[j], mlstm_norm_g[j],
                                hgrn_lb_logits, hgrn_norm_g[j], ab_w_out)
            else:
                h = _mixer_odd(h, norm_mix_g[layer], j, rwkv_mu, rwkv_w0, rwkv_w1, rwkv_w2, rwkv_a0, rwkv_a1,
                               rwkv_a2, rwkv_g1, rwkv_g2, rwkv_k_k, rwkv_k_a, rwkv_r_k,
                               rwkv_w_r, rwkv_w_k, rwkv_w_v, rwkv_w_o, rwkv_lnx_w, rwkv_lnx_b)
            wqk, vwo = _xattn_weights(xattn_w_q, xattn_w_o, layer, mem_kv)
            p = _mm(_rmsnorm(h, norm_xattn_g[layer], BF16), wqk, act="softmax", seg=(xscale, M),
                    out_dtype=BF16, tn=XATTN_HEADS * M, name="mm_xattn_scores")
            h = _mm(p, vwo, res=h, name="mm_xattn_out")
            up = _mm(_rmsnorm(h, norm_mlp_g[layer], BF16), mlp_w_up, layer=layer,
                     act="relu2", out_dtype=BF16, name="mm_mlp_up")
            for k0 in range(0, up.shape[1], D):
                h = _mm(up, mlp_w_down, layer=layer, res=h, krange=(k0, D), name="mm_mlp_down")
        outs.append(_rmsnorm(h, final_norm_g, F32))
    return jnp.stack(outs)
```

```python
import functools
import math

import jax
import jax.numpy as jnp
from jax import lax
from jax.experimental import pallas as pl
from jax.experimental.pallas import tpu as pltpu

F32 = jnp.float32
BF16 = jnp.bfloat16

NORM_EPS = 1e-6
GATE_CAP = 15.0
MLSTM_HEADS = 4
MLSTM_CHUNK = 64
HGRN_HEAD_DIM = 128
HGRN_CHUNK = 16
RWKV_HEAD = 64
RWKV_CHUNK = 64
RWKV_LNX_EPS = 64e-5
RWKV_DECAY_SCALE = math.exp(-0.5)
LOG2E = math.log2(math.e)
XATTN_HEADS = 4
LANES = 128
MXU_DIM = 256
STEP_LANES = 2048
VMEM_LIMIT = 56 * 1024 * 1024

NT = (((1,), (1,)), ((), ()))
TN = (((0,), (0,)), ((), ()))


def _cparams(sem):
    return pltpu.CompilerParams(dimension_semantics=sem, vmem_limit_bytes=VMEM_LIMIT)


def _sigmoid(x):
    return 1.0 / (1.0 + jnp.exp(-x))


def _log_sigmoid(x):
    return jnp.minimum(x, 0.0) - jnp.log1p(jnp.exp(-jnp.abs(x)))


def _pick(dim, pref):
    t = min(dim, pref)
    while dim % t:
        t //= 2
    return t


def _head_masks(rows, width, head):
    lane = lax.broadcasted_iota(jnp.int32, (rows, width), 1) // head
    return [lane == h for h in range(width // head)]


def _stack_heads(x, masks):
    z = jnp.zeros_like(x)
    return jnp.concatenate([jnp.where(m, x, z) for m in masks], axis=0)


_BDOT = {"nn": "gmk,gkn->gmn", "nt": "gmk,gnk->gmn", "tn": "gtm,gtn->gmn"}


def _bdot(xs, ys, mode):
    d = jnp.einsum(_BDOT[mode], jnp.stack(xs), jnp.stack(ys), preferred_element_type=F32)
    return [d[g] for g in range(len(xs))]


def _same_head(n, head):
    r = lax.broadcasted_iota(jnp.int32, (n, n), 0) // head
    c = lax.broadcasted_iota(jnp.int32, (n, n), 1) // head
    return r == c


def _mm_kernel(*refs, nk, act, has_res, seg):
    if has_res:
        a_ref, b_ref, res_ref, o_ref = refs[:4]
        rest = refs[4:]
    else:
        a_ref, b_ref, o_ref = refs[:3]
        res_ref = None
        rest = refs[3:]

    def finish(r):
        if act == "relu2":
            r = jnp.square(jnp.maximum(r, 0.0))
        elif act == "tanh":
            r = jnp.tanh(r)
        elif act == "sigmoid":
            r = _sigmoid(r)
        elif act == "softmax":
            scale, width = seg
            parts = []
            for j in range(r.shape[1] // width):
                s = r[:, j * width:(j + 1) * width] * scale
                e = jnp.exp(s - jnp.max(s, axis=-1, keepdims=True))
                parts.append(e / jnp.sum(e, axis=-1, keepdims=True))
            r = jnp.concatenate(parts, axis=1)
        if has_res:
            r = r + res_ref[...].astype(F32)
        o_ref[...] = r.astype(o_ref.dtype)

    part = jnp.dot(a_ref[...].astype(BF16), b_ref[...].astype(BF16), preferred_element_type=F32)
    if nk == 1:
        finish(part)
    else:
        acc_ref = rest[0]
        k = pl.program_id(2)

        @pl.when(k == 0)
        def _():
            acc_ref[...] = part

        @pl.when(k > 0)
        def _():
            acc_ref[...] += part

        @pl.when(k == nk - 1)
        def _():
            finish(acc_ref[...])


def _mm(a, b, *, out_dtype=F32, act=None, res=None, seg=None, tm=1024, tn=512, tk=4096, name="mm",
        layer=None, cols=None, krange=None):
    M, Ka = a.shape
    K2, nb = b.shape[-2:]
    assert Ka == K2 and (b.ndim == 3) == (layer is not None)
    c0, N = cols if cols is not None else (0, nb)
    kf, K = krange if krange is not None else (0, Ka)
    tm, tn, tk = _pick(M, tm), _pick(N, tn), _pick(K, tk)
    assert c0 % tn == 0 and kf % tk == 0
    j0 = c0 // tn
    k0 = kf // tk
    nk = K // tk
    if layer is None:
        b_spec = pl.BlockSpec((tk, tn), lambda i, j, k: (k0 + k, j0 + j))
    else:
        b_spec = pl.BlockSpec((None, tk, tn), lambda i, j, k: (layer, k0 + k, j0 + j))
    in_specs = [pl.BlockSpec((tm, tk), lambda i, j, k: (i, k0 + k)), b_spec]
    args = [a, b]
    if res is not None:
        in_specs.append(pl.BlockSpec((tm, tn), lambda i, j, k: (i, j)))
        args.append(res)
    scratch = [] if nk == 1 else [pltpu.VMEM((tm, tn), F32)]
    return pl.pallas_call(
        functools.partial(_mm_kernel, nk=nk, act=act, has_res=res is not None, seg=seg),
        grid=(M // tm, N // tn, nk),
        in_specs=in_specs,
        out_specs=pl.BlockSpec((tm, tn), lambda i, j, k: (i, j)),
        out_shape=jax.ShapeDtypeStruct((M, N), out_dtype),
        scratch_shapes=scratch,
        compiler_params=_cparams(("parallel", "parallel", "arbitrary")),
        name=name,
    )(*args)


def _mm_blocks_kernel(a_ref, b_ref, o_ref, *, trans_b):
    a = a_ref[...].astype(BF16)
    b = b_ref[...].astype(BF16)
    if trans_b:
        r = lax.dot_general(a, b, NT, preferred_element_type=F32)
    else:
        r = jnp.dot(a, b, preferred_element_type=F32)
    o_ref[...] = r.astype(o_ref.dtype)


def _rmsnorm_kernel(x_ref, g_ref, o_ref):
    x = x_ref[...].astype(F32)
    y = x * lax.rsqrt(jnp.mean(x * x, axis=-1, keepdims=True) + NORM_EPS)
    o_ref[...] = (y * g_ref[...]).astype(o_ref.dtype)


def _rmsnorm(x, g, out_dtype, tm=512):
    M, D = x.shape
    tm = _pick(M, tm)
    return pl.pallas_call(
        _rmsnorm_kernel,
        grid=(M // tm,),
        in_specs=[pl.BlockSpec((tm, D), lambda i: (i, 0)),
                  pl.BlockSpec((1, D), lambda i: (0, 0))],
        out_specs=pl.BlockSpec((tm, D), lambda i: (i, 0)),
        out_shape=jax.ShapeDtypeStruct((M, D), out_dtype),
        compiler_params=_cparams(("parallel",)),
        name="rmsnorm",
    )(x, g.reshape(1, D).astype(F32))


def _mlstm_kernel(q_ref, k_ref, v_ref, og_ref, gates_ref, bias_ref, ng_ref, y_hbm_ref, o_ref,
                  c_ref, n_ref, m_ref, *, heads):
    H = heads
    L = q_ref.shape[0]
    dk = q_ref.shape[1] // H
    dv = v_ref.shape[1] // H

    @pl.when(pl.program_id(0) == 0)
    def _():
        c_ref[...] = jnp.zeros_like(c_ref)
        n_ref[...] = jnp.zeros_like(n_ref)
        m_ref[...] = jnp.zeros_like(m_ref)

    pre = gates_ref[...] + bias_ref[...]
    capped = GATE_CAP * jnp.tanh(pre / GATE_CAP)
    lsig = _log_sigmoid(capped)
    lane = lax.broadcasted_iota(jnp.int32, capped.shape, 1)
    row = lax.broadcasted_iota(jnp.int32, (L, L), 0)
    col = lax.broadcasted_iota(jnp.int32, (L, L), 1)
    eye = row == col
    lower = col <= row

    def gate_terms(h):
        li_col = jnp.sum(jnp.where(lane == h, capped, 0.0), axis=1, keepdims=True)
        lf_col = jnp.sum(jnp.where(lane == h + H, lsig, 0.0), axis=1, keepdims=True)
        lf_row = jnp.sum(jnp.where(eye, lf_col, 0.0), axis=0, keepdims=True)
        li_row = jnp.sum(jnp.where(eye, li_col, 0.0), axis=0, keepdims=True)
        b_col = jnp.sum(jnp.where(lower, lf_row, 0.0), axis=1, keepdims=True)
        b_row = jnp.sum(jnp.where(row <= col, lf_col, 0.0), axis=0, keepdims=True)
        g_row = li_row - b_row
        g_col = li_col - b_col
        m_prev = m_ref[h][:, 0:1]
        a_col = jnp.maximum(m_prev, jnp.max(jnp.where(lower, g_row, -jnp.inf), axis=1, keepdims=True))
        a_last = jnp.max(a_col, axis=0, keepdims=True)
        b_last = jnp.sum(lf_col, axis=0, keepdims=True)
        return dict(dmat=jnp.where(lower, jnp.exp(g_row - a_col), 0.0), g_inter=jnp.exp(m_prev - a_col),
                    floor=jnp.exp(-(b_col + a_col)), w_col=jnp.exp(g_col - a_last),
                    decay=jnp.exp(m_prev - a_last), m_new=b_last + a_last)

    hs = range(H)
    gt = [gate_terms(h) for h in hs]
    q = [q_ref[:, h * dk:(h + 1) * dk] * (dk ** -0.5) for h in hs]
    k = [k_ref[:, h * dk:(h + 1) * dk] for h in hs]
    vb = [v_ref[:, h * dv:(h + 1) * dv].astype(BF16) for h in hs]
    qb = [x.astype(BF16) for x in q]
    cst = [c_ref[h] for h in hs]
    s = _bdot(qb, [x.astype(BF16) for x in k], "nt")
    s = [s[h] * gt[h]["dmat"] for h in hs]
    qc = _bdot(qb, [c.astype(BF16) for c in cst], "nn")
    sv = _bdot([x.astype(BF16) for x in s], vb, "nn")
    kw = [k[h] * gt[h]["w_col"] for h in hs]
    upd = _bdot([x.astype(BF16) for x in kw], vb, "tn")
    for h in hs:
        g = gt[h]
        num = sv[h] + g["g_inter"] * qc[h]
        qn = jnp.sum(q[h] * n_ref[h], axis=1, keepdims=True)
        den = jnp.sum(s[h], axis=1, keepdims=True) + g["g_inter"] * qn
        hh = num / jnp.maximum(jnp.abs(den), g["floor"])
        c_ref[h] = g["decay"] * cst[h] + upd[h]
        n_ref[h] = g["decay"] * n_ref[h] + jnp.sum(kw[h], axis=0, keepdims=True)
        m_ref[h] = jnp.broadcast_to(g["m_new"], m_ref.shape[1:])
        y = hh * lax.rsqrt(jnp.mean(hh * hh, axis=-1, keepdims=True) + NORM_EPS)
        cs = slice(h * dv, (h + 1) * dv)
        o_ref[:, cs] = (y * ng_ref[:, cs] * _sigmoid(og_ref[:, cs])).astype(o_ref.dtype)


def _mlstm(proj, gates, bias_row, norm_g, dk, dv, y):
    T = proj.shape[0]
    H = MLSTM_HEADS
    L = MLSTM_CHUNK
    wk, wv = H * dk, H * dv
    assert (2 * wk) % wv == 0
    v0 = (2 * wk) // wv
    return pl.pallas_call(
        functools.partial(_mlstm_kernel, heads=H),
        grid=(T // L,),
        in_specs=[pl.BlockSpec((L, wk), lambda c: (c, 0)),
                  pl.BlockSpec((L, wk), lambda c: (c, 1)),
                  pl.BlockSpec((L, wv), lambda c: (c, v0)),
                  pl.BlockSpec((L, wv), lambda c: (c, v0 + 1)),
                  pl.BlockSpec((L, LANES), lambda c: (c, 0)),
                  pl.BlockSpec((1, LANES), lambda c: (0, 0)),
                  pl.BlockSpec((1, wv), lambda c: (0, 0)),
                  pl.BlockSpec(memory_space=pl.ANY)],
        out_specs=pl.BlockSpec((L, wv), lambda c: (c, 0)),
        out_shape=jax.ShapeDtypeStruct(y.shape, y.dtype),
        input_output_aliases={7: 0},
        scratch_shapes=[pltpu.VMEM((H, dk, dv), F32), pltpu.VMEM((H, 1, dk), F32), pltpu.VMEM((H, 1, LANES), F32)],
        compiler_params=_cparams(("arbitrary",)),
        name="mlstm",
    )(proj, proj, proj, proj, gates, bias_row, norm_g.reshape(1, wv).astype(F32), y)


def _hgrn_kernel(q_ref, f_ref, i_ref, g_ref, lb_ref, ng_ref, y_hbm_ref, o_ref, st_ref, *, lb_row, chunk):
    TB = q_ref.shape[0]
    dk = HGRN_HEAD_DIM
    GW = st_ref.shape[1]
    NG = q_ref.shape[1] // GW
    HG = GW // dk
    C = chunk

    @pl.when(pl.program_id(1) == 0)
    def _():
        st_ref[...] = jnp.zeros_like(st_ref)

    logits = lb_ref[...].astype(F32)
    e = jnp.exp(logits - jnp.max(logits, axis=0, keepdims=True))
    p = e / jnp.sum(e, axis=0, keepdims=True)
    lb_all = jnp.sum(p[0:lb_row + 1], axis=0, keepdims=True)
    ng_all = ng_ref[...]
    srow = lax.broadcasted_iota(jnp.int32, (C, GW), 0)
    srow_h = srow[:, :dk]
    same = _same_head(GW, dk)

    def elementwise(g, ins):
        qb, fb, vb = ins
        lb = lb_all[:, g * GW:(g + 1) * GW]
        f = lb + (1.0 - lb) * _sigmoid(fb)
        kk = 1.0 - f
        qh = qb * _sigmoid(qb)
        A = jnp.log2(f)
        sh = 1
        while sh < C:
            A = A + jnp.where(srow >= sh, pltpu.roll(A, sh, axis=0), 0.0)
            sh *= 2
        a_last = A[C - 1:C]
        intra = []
        for h in range(HG):
            hs = slice(h * dk, (h + 1) * dk)
            Ah, kh, qhh, vh = A[:, hs], kk[:, hs], qh[:, hs], vb[:, hs]
            rows = []
            for t in range(C):
                rel = jnp.where(srow_h <= t, jnp.exp2(Ah[t:t + 1] - Ah), 0.0)
                sc = jnp.sum(rel * kh * qhh[t:t + 1], axis=1, keepdims=True)
                rows.append(jnp.sum(sc * vh, axis=0, keepdims=True))
            intra.append(jnp.concatenate(rows, axis=0))
        qe = (qh * jnp.exp2(A)).astype(BF16)
        ke = (kk * jnp.exp2(a_last - A)).astype(BF16)
        return qe, ke, vb.astype(BF16), jnp.concatenate(intra, axis=1), jnp.exp2(a_last)

    def body(ci, carry):
        rs = pl.ds(pl.multiple_of(ci * C, C), C)
        G = range(NG)
        gsl = [slice(g * GW, (g + 1) * GW) for g in G]
        st = [st_ref[g] for g in G]
        ew = [elementwise(g, tuple(ref[rs, gsl[g]] for ref in (q_ref, f_ref, i_ref))) for g in G]
        o_inter = _bdot([e[0] for e in ew], [s.astype(BF16) for s in st], "nt")
        upd = _bdot([e[2] for e in ew], [e[1] for e in ew], "tn")
        for g in G:
            st_ref[g] = st[g] * ew[g][4] + (upd[g] if HG == 1 else jnp.where(same, upd[g], 0.0))
            o = o_inter[g] + ew[g][3]
            ys = []
            for h in range(HG):
                oh = o[:, h * dk:(h + 1) * dk]
                ys.append(oh * lax.rsqrt(jnp.mean(oh * oh, axis=-1, keepdims=True) + NORM_EPS))
            gb = g_ref[rs, gsl[g]]
            o_ref[rs, gsl[g]] = (jnp.concatenate(ys, axis=1) * ng_all[:, gsl[g]]
                                 * (gb * _sigmoid(gb))).astype(o_ref.dtype)
        return carry

    lax.fori_loop(0, TB // C, body, 0)


def _hgrn(proj, lb_logits, lb_row, norm_g, y, tb=256):
    T = proj.shape[0]
    W = proj.shape[1] // 4
    sw = _pick(W, STEP_LANES)
    gw = HGRN_HEAD_DIM
    H = W // sw
    tb = _pick(T, tb)
    R = lb_logits.shape[0]
    assert (y.shape[1] - W) % sw == 0
    off = (y.shape[1] - W) // sw
    return pl.pallas_call(
        functools.partial(_hgrn_kernel, lb_row=lb_row, chunk=HGRN_CHUNK),
        grid=(H, T // tb),
        in_specs=[pl.BlockSpec((tb, sw), lambda h, c: (c, h)),
                  pl.BlockSpec((tb, sw), lambda h, c: (c, H + h)),
                  pl.BlockSpec((tb, sw), lambda h, c: (c, 2 * H + h)),
                  pl.BlockSpec((tb, sw), lambda h, c: (c, 3 * H + h)),
                  pl.BlockSpec((R, sw), lambda h, c: (0, h)),
                  pl.BlockSpec((1, sw), lambda h, c: (0, h)),
                  pl.BlockSpec(memory_space=pl.ANY)],
        out_specs=pl.BlockSpec((tb, sw), lambda h, c: (c, off + h)),
        out_shape=jax.ShapeDtypeStruct(y.shape, y.dtype),
        input_output_aliases={6: 0},
        scratch_shapes=[pltpu.VMEM((sw // gw, gw, gw), F32)],
        compiler_params=_cparams(("parallel", "arbitrary")),
        name="hgrn2",
    )(proj, proj, proj, proj, lb_logits.astype(F32), norm_g.reshape(1, W).astype(F32), y)


def _xattn_weights(w_q, w_o, layer, mem_kv):
    D = w_q.shape[1]
    M = mem_kv.shape[0]
    H = XATTN_HEADS
    d = D // H
    tm = _pick(D, 1024)
    wqk = pl.pallas_call(
        functools.partial(_mm_blocks_kernel, trans_b=True),
        grid=(D // tm, H),
        in_specs=[pl.BlockSpec((None, tm, d), lambda i, h: (layer, i, h)),
                  pl.BlockSpec((M, d), lambda i, h: (0, h))],
        out_specs=pl.BlockSpec((tm, M), lambda i, h: (i, h)),
        out_shape=jax.ShapeDtypeStruct((D, H * M), BF16),
        compiler_params=_cparams(("parallel", "parallel")),
        name="xattn_wqk",
    )(w_q, mem_kv)
    tn = _pick(D, 1024)
    vwo = pl.pallas_call(
        functools.partial(_mm_blocks_kernel, trans_b=False),
        grid=(H, D // tn),
        in_specs=[pl.BlockSpec((M, d), lambda h, j: (0, H + h)),
                  pl.BlockSpec((None, d, tn), lambda h, j: (layer, h, j))],
        out_specs=pl.BlockSpec((M, tn), lambda h, j: (h, j)),
        out_shape=jax.ShapeDtypeStruct((H * M, D), BF16),
        compiler_params=_cparams(("parallel", "parallel")),
        name="xattn_vwo",
    )(mem_kv, w_o)
    return wqk, vwo


def _shift_kernel(h_ref, prev_ref, g_ref, mu_ref, *o_refs):
    def norm(h):
        return h * lax.rsqrt(jnp.mean(h * h, axis=-1, keepdims=True) + NORM_EPS) * g_ref[...]

    x = norm(h_ref[...])
    n = prev_ref.shape[0]
    last = jnp.where(pl.program_id(0) == 0, 0.0, norm(prev_ref[...])[n - 1:n, :])
    row = lax.broadcasted_iota(jnp.int32, x.shape, 0)
    xx = jnp.where(row == 0, last, pltpu.roll(x, 1, axis=0)) - x
    for j, o_ref in enumerate(o_refs):
        o_ref[...] = (x + xx * mu_ref[j:j + 1, :]).astype(o_ref.dtype)


def _rwkv_shift(h, g, mu, tm=256):
    T, D = h.shape
    J = mu.shape[0]
    tm = _pick(T, tm)
    pr = 8
    spec = pl.BlockSpec((tm, D), lambda i: (i, 0))
    return pl.pallas_call(
        _shift_kernel,
        grid=(T // tm,),
        in_specs=[spec,
                  pl.BlockSpec((pr, D), lambda i: (jnp.maximum(i * (tm // pr) - 1, 0), 0)),
                  pl.BlockSpec((1, D), lambda i: (0, 0)),
                  pl.BlockSpec((J, D), lambda i: (0, 0))],
        out_specs=[spec] * J,
        out_shape=[jax.ShapeDtypeStruct((T, D), BF16)] * J,
        compiler_params=_cparams(("parallel",)),
        name="rwkv_shift",
    )(h, h, g.reshape(1, D).astype(F32), mu.astype(F32))


def _split2(x):
    x1 = x.astype(BF16)
    return x1, (x - x1.astype(F32)).astype(BF16)


def _dot_exact_rhs(x, m):
    n = x.shape[0]
    d = jnp.dot(jnp.concatenate(_split2(x), axis=0), m, preferred_element_type=F32)
    return d[:n] + d[n:]


def _head_ones(n, head, scale, dtype):
    return jnp.where(_same_head(n, head), scale, 0.0).astype(dtype)


def _rwkv_prep_kernel(r_ref, k_ref, v_ref, lw_ref, la_ref, par_ref,
                      at_ref, rt_ref, bk_ref, tc_ref, rb_ref, akv_ref, rkv_ref, bon_ref, gl_ref, *, chunk):
    TB = r_ref.shape[0]
    L = chunk
    N = RWKV_HEAD
    GW = 4 * L
    HG = GW // N
    NG = r_ref.shape[1] // GW
    assert L == N and GW == MXU_DIM
    par = par_ref[...]
    seg1 = _head_ones(GW, N, 1.0, BF16)
    hmask = _head_masks(L, GW, N)
    lane = lax.broadcasted_iota(jnp.int32, (L, GW), 1)
    rowi = lax.broadcasted_iota(jnp.int32, (L, GW), 0)
    half = GW // 2
    lane_h = lax.broadcasted_iota(jnp.int32, (L, half), 1)
    row_h = lax.broadcasted_iota(jnp.int32, (L, half), 0)
    low = lane_h < N
    src = jnp.where(low, lane_h, lane_h - N)
    strict = src < row_h
    incl = src <= row_h
    eye_half = jnp.where(src == row_h, 1.0, 0.0)
    eye_cat = jnp.concatenate([eye_half, eye_half], axis=1)

    def elementwise(g, ins):
        gs = slice(g * GW, (g + 1) * GW)
        w0, a0, k_k, k_a, r_k = (par[i:i + 1, gs] for i in range(5))
        r, k, v, lw, la = ins
        logw = -(RWKV_DECAY_SCALE * LOG2E) * _sigmoid(w0 + lw)
        a_sig = _sigmoid(a0 + la)
        kkr = k * k_k
        k2 = k * (1.0 + (a_sig - 1.0) * k_a)
        cs = logw
        sh = 1
        while sh < L:
            cs = cs + jnp.where(rowi >= sh, pltpu.roll(cs, sh, axis=0), 0.0)
            sh *= 2
        return r, v, logw, a_sig, kkr, k2, cs, r * k2 * r_k

    def pre_dots(ew, ss, bsum):
        r, v, logw, a_sig, kkr, k2, cs, _ = ew
        kk = kkr / jnp.maximum(jnp.sqrt(ss), 1e-12)
        ginv = jnp.exp2(-cs)
        atb = (-kk * jnp.exp2(cs - logw)).astype(BF16)
        rtb = (r * jnp.exp2(cs)).astype(BF16)
        bt = (kk * a_sig * ginv).astype(BF16)
        kt = (k2 * ginv).astype(BF16)
        zero = jnp.zeros_like(atb)
        lhs = jnp.concatenate([jnp.where(m, x, zero) for m in hmask for x in (atb, rtb)], axis=0)
        return atb, rtb, bt, kt, bsum * v, jnp.exp2(cs[L - 1:L]), lhs, jnp.concatenate([bt, kt, kt, bt], axis=0)

    def cat(parts, even_sel, odd_sel, mask):
        cols = [jnp.where(mask, jnp.where(low, even_sel(parts[2 * c]), odd_sel(parts[2 * c + 1])), 0.0)
                for c in range(HG // 2)]
        return jnp.concatenate(cols, axis=1)

    lo = lambda x: x[:, :half]
    hi = lambda x: x[:, half:]

    def body(ci, carry):
        r0 = pl.multiple_of(ci * L, L)
        sl = pl.ds(r0, L)
        G = range(NG)
        ews = [elementwise(g, tuple(ref[sl, g * GW:(g + 1) * GW]
                                    for ref in (r_ref, k_ref, v_ref, lw_ref, la_ref))) for g in G]
        sums = _dot_exact_rhs(jnp.concatenate([x for ew in ews for x in (ew[4] * ew[4], ew[7])], axis=0), seg1)
        pre = [pre_dots(ews[g], sums[2 * g * L:(2 * g + 1) * L], sums[(2 * g + 1) * L:(2 * g + 2) * L]) for g in G]
        P = _bdot([p[6] for p in pre], [p[7] for p in pre], "nt")
        pat = [[P[g][2 * h * L:(2 * h + 1) * L] for h in range(HG)] for g in G]
        prt = [[P[g][(2 * h + 1) * L:(2 * h + 2) * L] for h in range(HG)] for g in G]
        ncat = [cat(pat[g], lo, hi, strict) for g in G]
        nb = [n.astype(BF16) for n in ncat]
        pw = _bdot(nb, [_stack_heads(n, hmask) for n in nb], "nn")
        tinv = [eye_cat + n for n in ncat]
        power = 2
        while power < L:
            pwb = [x.astype(BF16) for x in pw]
            pst = [_stack_heads(x, hmask) for x in pwb]
            if 2 * power < L:
                d = _bdot([jnp.concatenate([tinv[g].astype(BF16), pwb[g]], axis=0) for g in G], pst, "nn")
                tinv = [tinv[g] + d[g][:L] for g in G]
                pw = [d[g][L:] for g in G]
            else:
                d = _bdot([t.astype(BF16) for t in tinv], pst, "nn")
                tinv = [tinv[g] + d[g] for g in G]
            power *= 2
        akrk = [jnp.concatenate([cat(pat[g], hi, lo, strict), cat(prt[g], hi, lo, incl)], axis=0).astype(BF16)
                for g in G]
        kv = _bdot(akrk, [_stack_heads(ews[g][1].astype(BF16), hmask) for g in G], "nn")
        for g in G:
            gs = slice(g * GW, (g + 1) * GW)
            atb, rtb, bt, kt, bon, gl = pre[g][:6]
            at_ref[sl, gs] = atb
            rt_ref[sl, gs] = rtb
            bk_ref[pl.ds(pl.multiple_of(2 * r0, 2 * L), L), gs] = bt
            bk_ref[pl.ds(pl.multiple_of(2 * r0 + L, L), L), gs] = kt
            tc_ref[sl, gs] = tinv[g].astype(BF16)
            rb_ref[sl, gs] = cat(prt[g], lo, hi, incl).astype(BF16)
            akv_ref[sl, gs] = kv[g][:L]
            rkv_ref[sl, gs] = kv[g][L:]
            bon_ref[sl, gs] = bon
            gl_ref[ci, :, gs] = gl
        return carry

    lax.fori_loop(0, TB // L, body, 0)


def _rwkv_state_kernel(at_ref, rt_ref, bk_ref, tc_ref, rb_ref, akv_ref, rkv_ref, gl_ref, v_ref,
                       y_ref, ht_ref, *, chunk):
    TB = v_ref.shape[0]
    L = chunk
    N = RWKV_HEAD
    GW = ht_ref.shape[1]
    NG = v_ref.shape[1] // GW

    @pl.when(pl.program_id(1) == 0)
    def _():
        ht_ref[...] = jnp.zeros_like(ht_ref)

    hmask = _head_masks(L, GW, N)
    same = _same_head(GW, N)

    def body(ci, carry):
        r0 = pl.multiple_of(ci * L, L)
        sl = pl.ds(r0, L)
        sl2 = pl.ds(pl.multiple_of(2 * r0, 2 * L), 2 * L)
        G = range(NG)
        gsl = [slice(g * GW, (g + 1) * GW) for g in G]
        ht = [ht_ref[g] for g in G]
        xr = _bdot([jnp.concatenate([at_ref[sl, gs], rt_ref[sl, gs]], axis=0) for gs in gsl],
                   [h.astype(BF16) for h in ht], "nt")
        xb = [(xr[g][:L] + akv_ref[sl, gsl[g]]).astype(BF16) for g in G]
        u = _bdot([tc_ref[sl, gs] for gs in gsl], [_stack_heads(x, hmask) for x in xb], "nn")
        ub = [x.astype(BF16) for x in u]
        yd = _bdot([rb_ref[sl, gs] for gs in gsl], [_stack_heads(x, hmask) for x in ub], "nn")
        dht = _bdot([jnp.concatenate([ub[g], v_ref[sl, gsl[g]].astype(BF16)], axis=0) for g in G],
                    [bk_ref[sl2, gs] for gs in gsl], "tn")
        for g in G:
            ht_ref[g] = (ht[g] + jnp.where(same, dht[g], 0.0)) * gl_ref[ci, :, gsl[g]]
            y_ref[sl, gsl[g]] = xr[g][L:] + yd[g] + rkv_ref[sl, gsl[g]]
        return carry

    lax.fori_loop(0, TB // L, body, 0)


def _rwkv_out_kernel(y_ref, bon_ref, g_ref, ln_ref, o_ref):
    N = RWKV_HEAD
    GW = MXU_DIM
    seg_mean = _head_ones(GW, N, 1.0 / N, BF16)
    for g in range(y_ref.shape[1] // GW):
        gs = slice(g * GW, (g + 1) * GW)
        y = y_ref[:, gs]
        yc = y - _dot_exact_rhs(y, seg_mean)
        var = _dot_exact_rhs(yc * yc, seg_mean)
        yn = yc * lax.rsqrt(var + RWKV_LNX_EPS) * ln_ref[0:1, gs] + ln_ref[1:2, gs]
        o_ref[:, gs] = ((yn + bon_ref[:, gs]) * g_ref[:, gs]).astype(o_ref.dtype)


def _rwkv_core(r, k, v, lw, la, g, w0, a0, k_k, k_a, r_k, lnx_w, lnx_b, tb=128):
    T, D = r.shape
    L = RWKV_CHUNK
    gw = MXU_DIM
    W = _pick(D, STEP_LANES)
    assert W % gw == 0
    P = D // W
    tb = _pick(T, tb)
    nc = tb // L
    par = jnp.stack([w0, a0, k_k, k_a, r_k.reshape(D)]).astype(F32)
    par = jnp.concatenate([par, jnp.zeros((3, D), F32)], axis=0)
    spec = pl.BlockSpec((tb, W), lambda p, c: (c, p))
    spec2 = pl.BlockSpec((2 * tb, W), lambda p, c: (c, p))
    gspec = pl.BlockSpec((nc, 1, W), lambda p, c: (c, 0, p))
    bf = jax.ShapeDtypeStruct((T, D), BF16)
    f32 = jax.ShapeDtypeStruct((T, D), F32)
    at, rt, bk, tc, rb, akv, rkv, bon, gl = pl.pallas_call(
        functools.partial(_rwkv_prep_kernel, chunk=L),
        grid=(P, T // tb),
        in_specs=[spec] * 5 + [pl.BlockSpec((8, W), lambda p, c: (0, p))],
        out_specs=[spec, spec, spec2, spec, spec, spec, spec, spec, gspec],
        out_shape=[bf, bf, jax.ShapeDtypeStruct((2 * T, D), BF16), bf, bf, f32, f32, f32,
                   jax.ShapeDtypeStruct((T // L, 1, D), F32)],
        compiler_params=_cparams(("parallel", "parallel")),
        name="rwkv_prep",
    )(r, k, v, lw, la, par)
    y = pl.pallas_call(
        functools.partial(_rwkv_state_kernel, chunk=L),
        grid=(P, T // tb),
        in_specs=[spec, spec, spec2, spec, spec, spec, spec, gspec, spec],
        out_specs=spec,
        out_shape=f32,
        scratch_shapes=[pltpu.VMEM((W // gw, gw, gw), F32)],
        compiler_params=_cparams(("parallel", "arbitrary")),
        name="rwkv_state",
    )(at, rt, bk, tc, rb, akv, rkv, gl, v)
    ln = jnp.stack([lnx_w, lnx_b]).astype(F32)
    ln = jnp.concatenate([ln, jnp.zeros((6, D), F32)], axis=0)
    return pl.pallas_call(
        _rwkv_out_kernel,
        grid=(P, T // tb),
        in_specs=[spec, spec, spec, pl.BlockSpec((8, W), lambda p, c: (0, p))],
        out_specs=spec,
        out_shape=bf,
        compiler_params=_cparams(("parallel", "parallel")),
        name="rwkv_out",
    )(y, bon, g, ln)


def _mixer_even(h, norm_g, w_in, j, b_i, b_f, mlstm_g, lb_logits, hgrn_g, w_out):
    D = h.shape[1]
    H = MLSTM_HEADS
    mix_a = D // 2
    dv = mix_a // H
    dk = dv // 2
    na = 2 * H * dk + 2 * mix_a
    hn = _rmsnorm(h, norm_g, BF16)
    w_if = jnp.zeros((D, LANES), F32).at[:, :2 * H].set(w_in[j, :, na:na + 2 * H])
    w_b = w_in[j, :, na + 2 * H:]
    proj_a = _mm(hn, w_in, layer=j, cols=(0, na), name="mm_mlstm_in")
    gates = _mm(hn, w_if, name="mm_gates")
    proj_b = _mm(hn, w_b, name="mm_hgrn_in")
    bias_row = jnp.zeros((1, LANES), F32).at[0, :H].set(b_i.astype(F32)).at[0, H:2 * H].set(b_f.astype(F32))
    y = jnp.zeros((h.shape[0], D), BF16)
    y = _mlstm(proj_a, gates, bias_row, mlstm_g, dk, dv, y)
    y = _hgrn(proj_b, lb_logits, j, hgrn_g, y)
    return _mm(y, w_out, layer=j, res=h, name="mm_mix_out")


def _mixer_odd(h, norm_g, j, mu, w0, w1, w2, a0, a1, a2, g1, g2, k_k, k_a, r_k, w_r, w_k, w_v, w_o, lnx_w, lnx_b):
    s_r, s_w, s_k, s_v, s_a, s_g = _rwkv_shift(h, norm_g, mu[j])
    r = _mm(s_r, w_r, layer=j, name="mm_rwkv_r")
    k = _mm(s_k, w_k, layer=j, name="mm_rwkv_k")
    v = _mm(s_v, w_v, layer=j, name="mm_rwkv_v")
    wide = dict(tm=512, tn=h.shape[1])
    lw = _mm(_mm(s_w, w1, layer=j, act="tanh", out_dtype=BF16, name="mm_lora_w1"), w2, layer=j,
             name="mm_lora_w2", **wide)
    la = _mm(_mm(s_a, a1, layer=j, out_dtype=BF16, name="mm_lora_a1"), a2, layer=j, name="mm_lora_a2", **wide)
    g = _mm(_mm(s_g, g1, layer=j, act="sigmoid", out_dtype=BF16, name="mm_lora_g1"), g2, layer=j,
            name="mm_lora_g2", **wide)
    y = _rwkv_core(r, k, v, lw, la, g, w0[j], a0[j], k_k[j], k_a[j], r_k[j], lnx_w[j], lnx_b[j])
    return _mm(y, w_o, layer=j, res=h, name="mm_rwkv_out")


def kernel(x, mem, norm_mix_g, norm_xattn_g, norm_mlp_g, final_norm_g, mem_norm_g, ab_w_in, mlstm_b_i, mlstm_b_f, mlstm_norm_g, hgrn_lb_logits, hgrn_norm_g, ab_w_out, rwkv_mu, rwkv_w0, rwkv_w1, rwkv_w2, rwkv_a0, rwkv_a1, rwkv_a2, rwkv_g1, rwkv_g2, rwkv_k_k, rwkv_k_a, rwkv_r_k, rwkv_w_r, rwkv_w_k, rwkv_w_v, rwkv_w_o, rwkv_lnx_w, rwkv_lnx_b, xattn_w_q, xattn_w_o, mem_w_kv, mlp_w_up, mlp_w_down):
    B, T, D = x.shape
    depth = norm_mix_g.shape[0]
    M = mem.shape[1]
    xscale = (D // XATTN_HEADS) ** -0.5
    outs = []
    for b in range(B):
        mem_kv = _mm(_rmsnorm(mem[b], mem_norm_g, BF16), mem_w_kv, out_dtype=BF16, name="mm_mem_kv")
        h = x[b]
        for layer in range(depth):
            j = layer // 2
            if layer % 2 == 0:
                h = _mixer_even(h, norm_mix_g[layer], ab_w_in, j, mlstm_b_i[j], mlstm_b_f[j], mlstm_norm_g[j],
                                hgrn_lb_logits, hgrn_norm_g[j], ab_w_out)
            else:
                h = _mixer_odd(h, norm_mix_g[layer], j, rwkv_mu, rwkv_w0, rwkv_w1, rwkv_w2, rwkv_a0, rwkv_a1,
                               rwkv_a2, rwkv_g1, rwkv_g2, rwkv_k_k, rwkv_k_a, rwkv_r_k,
                               rwkv_w_r, rwkv_w_k, rwkv_w_v, rwkv_w_o, rwkv_lnx_w, rwkv_lnx_b)
            wqk, vwo = _xattn_weights(xattn_w_q, xattn_w_o, layer, mem_kv)
            p = _mm(_rmsnorm(h, norm_xattn_g[layer], BF16), wqk, act="softmax", seg=(xscale, M),
                    out_dtype=BF16, tn=XATTN_HEADS * M, name="mm_xattn_scores")
            h = _mm(p, vwo, res=h, tn=1024, name="mm_xattn_out")
            up = _mm(_rmsnorm(h, norm_mlp_g[layer], BF16), mlp_w_up, layer=layer,
                     act="relu2", out_dtype=BF16, name="mm_mlp_up")
            for k0 in range(0, up.shape[1], D):
                h = _mm(up, mlp_w_down, layer=layer, res=h, krange=(k0, D), name="mm_mlp_down")
        outs.append(_rmsnorm(h, final_norm_g, F32))
    return jnp.stack(outs)
```

```python
import functools
import math

import jax
import jax.numpy as jnp
from jax import lax
from jax.experimental import pallas as pl
from jax.experimental.pallas import tpu as pltpu

F32 = jnp.float32
BF16 = jnp.bfloat16

NORM_EPS = 1e-6
GATE_CAP = 15.0
MLSTM_HEADS = 4
MLSTM_CHUNK = 64
HGRN_HEAD_DIM = 128
HGRN_CHUNK = 16
RWKV_HEAD = 64
RWKV_CHUNK = 64
RWKV_LNX_EPS = 64e-5
RWKV_DECAY_SCALE = math.exp(-0.5)
LOG2E = math.log2(math.e)
XATTN_HEADS = 4
LANES = 128
MXU_DIM = 256
STEP_LANES = 2048
VMEM_LIMIT = 60 * 1024 * 1024

NT = (((1,), (1,)), ((), ()))
TN = (((0,), (0,)), ((), ()))


def _cparams(sem):
    return pltpu.CompilerParams(dimension_semantics=sem, vmem_limit_bytes=VMEM_LIMIT)


def _sigmoid(x):
    return 1.0 / (1.0 + jnp.exp(-x))


def _log_sigmoid(x):
    return jnp.minimum(x, 0.0) - jnp.log1p(jnp.exp(-jnp.abs(x)))


def _pick(dim, pref):
    t = min(dim, pref)
    while dim % t:
        t //= 2
    return t


def _head_masks(rows, width, head):
    lane = lax.broadcasted_iota(jnp.int32, (rows, width), 1) // head
    return [lane == h for h in range(width // head)]


def _stack_heads(x, masks):
    z = jnp.zeros_like(x)
    return jnp.concatenate([jnp.where(m, x, z) for m in masks], axis=0)


_BDOT = {"nn": "gmk,gkn->gmn", "nt": "gmk,gnk->gmn", "tn": "gtm,gtn->gmn"}


def _bdot(xs, ys, mode):
    d = jnp.einsum(_BDOT[mode], jnp.stack(xs), jnp.stack(ys), preferred_element_type=F32)
    return [d[g] for g in range(len(xs))]


def _same_head(n, head):
    r = lax.broadcasted_iota(jnp.int32, (n, n), 0) // head
    c = lax.broadcasted_iota(jnp.int32, (n, n), 1) // head
    return r == c


def _mm_kernel(*refs, nk, act, has_res, seg):
    if has_res:
        a_ref, b_ref, res_ref, o_ref = refs[:4]
        rest = refs[4:]
    else:
        a_ref, b_ref, o_ref = refs[:3]
        res_ref = None
        rest = refs[3:]

    def finish(r):
        if act == "relu2":
            r = jnp.square(jnp.maximum(r, 0.0))
        elif act == "tanh":
            r = jnp.tanh(r)
        elif act == "sigmoid":
            r = _sigmoid(r)
        elif act == "softmax":
            scale, width = seg
            parts = []
            for j in range(r.shape[1] // width):
                s = r[:, j * width:(j + 1) * width] * scale
                e = jnp.exp(s - jnp.max(s, axis=-1, keepdims=True))
                parts.append(e / jnp.sum(e, axis=-1, keepdims=True))
            r = jnp.concatenate(parts, axis=1)
        if has_res:
            r = r + res_ref[...].astype(F32)
        o_ref[...] = r.astype(o_ref.dtype)

    part = jnp.dot(a_ref[...].astype(BF16), b_ref[...].astype(BF16), preferred_element_type=F32)
    if nk == 1:
        finish(part)
    else:
        acc_ref = rest[0]
        k = pl.program_id(2)

        @pl.when(k == 0)
        def _():
            acc_ref[...] = part

        @pl.when(k > 0)
        def _():
            acc_ref[...] += part

        @pl.when(k == nk - 1)
        def _():
            finish(acc_ref[...])


def _mm_tiles(M, N, K):
    if K >= 4096 and M % 2048 == 0 and N % MXU_DIM == 0:
        return 2048, MXU_DIM, 4096
    return 1024, 512, 4096


def _mm(a, b, *, out_dtype=F32, act=None, res=None, seg=None, tm=None, tn=None, tk=None, name="mm",
        layer=None, cols=None, krange=None):
    M, Ka = a.shape
    K2, nb = b.shape[-2:]
    assert Ka == K2 and (b.ndim == 3) == (layer is not None)
    c0, N = cols if cols is not None else (0, nb)
    kf, K = krange if krange is not None else (0, Ka)
    dm, dn, dk = _mm_tiles(M, N, K)
    tm, tn, tk = _pick(M, tm or dm), _pick(N, tn or dn), _pick(K, tk or dk)
    assert c0 % tn == 0 and kf % tk == 0
    j0 = c0 // tn
    k0 = kf // tk
    nk = K // tk
    if layer is None:
        b_spec = pl.BlockSpec((tk, tn), lambda i, j, k: (k0 + k, j0 + j))
    else:
        b_spec = pl.BlockSpec((None, tk, tn), lambda i, j, k: (layer, k0 + k, j0 + j))
    in_specs = [pl.BlockSpec((tm, tk), lambda i, j, k: (i, k0 + k)), b_spec]
    args = [a, b]
    if res is not None:
        in_specs.append(pl.BlockSpec((tm, tn), lambda i, j, k: (i, j)))
        args.append(res)
    scratch = [] if nk == 1 else [pltpu.VMEM((tm, tn), F32)]
    return pl.pallas_call(
        functools.partial(_mm_kernel, nk=nk, act=act, has_res=res is not None, seg=seg),
        grid=(M // tm, N // tn, nk),
        in_specs=in_specs,
        out_specs=pl.BlockSpec((tm, tn), lambda i, j, k: (i, j)),
        out_shape=jax.ShapeDtypeStruct((M, N), out_dtype),
        scratch_shapes=scratch,
        compiler_params=_cparams(("parallel", "parallel", "arbitrary")),
        name=name,
    )(*args)


def _shift_cols_kernel(a_ref, b_ref, o_ref, *, shift):
    x = jnp.concatenate([a_ref[...], b_ref[:, :LANES]], axis=1)
    o_ref[...] = x[:, shift:shift + o_ref.shape[1]]


def _shift_cols(w, layer, start, width, tr=256, tc=2048):
    R = w.shape[1]
    base = (start // tc) * tc
    shift = start - base
    assert shift < LANES and width % tc == 0 and R % tr == 0
    j0 = base // tc
    return pl.pallas_call(
        functools.partial(_shift_cols_kernel, shift=shift),
        grid=(R // tr, width // tc),
        in_specs=[pl.BlockSpec((None, tr, tc), lambda i, j: (layer, i, j0 + j)),
                  pl.BlockSpec((None, tr, tc), lambda i, j: (layer, i, j0 + j + 1))],
        out_specs=pl.BlockSpec((tr, tc), lambda i, j: (i, j)),
        out_shape=jax.ShapeDtypeStruct((R, width), w.dtype),
        compiler_params=_cparams(("parallel", "parallel")),
        name="shift_cols",
    )(w, w)


def _mm_blocks_kernel(a_ref, b_ref, o_ref, *, trans_b):
    a = a_ref[...].astype(BF16)
    b = b_ref[...].astype(BF16)
    if trans_b:
        r = lax.dot_general(a, b, NT, preferred_element_type=F32)
    else:
        r = jnp.dot(a, b, preferred_element_type=F32)
    o_ref[...] = r.astype(o_ref.dtype)


def _rmsnorm_kernel(x_ref, g_ref, o_ref):
    x = x_ref[...].astype(F32)
    y = x * lax.rsqrt(jnp.mean(x * x, axis=-1, keepdims=True) + NORM_EPS)
    o_ref[...] = (y * g_ref[...]).astype(o_ref.dtype)


def _rmsnorm(x, g, out_dtype, tm=512):
    M, D = x.shape
    tm = _pick(M, tm)
    return pl.pallas_call(
        _rmsnorm_kernel,
        grid=(M // tm,),
        in_specs=[pl.BlockSpec((tm, D), lambda i: (i, 0)),
                  pl.BlockSpec((1, D), lambda i: (0, 0))],
        out_specs=pl.BlockSpec((tm, D), lambda i: (i, 0)),
        out_shape=jax.ShapeDtypeStruct((M, D), out_dtype),
        compiler_params=_cparams(("parallel",)),
        name="rmsnorm",
    )(x, g.reshape(1, D).astype(F32))


def _mlstm_kernel(q_ref, k_ref, v_ref, og_ref, gates_ref, bias_ref, ng_ref, y_hbm_ref, o_ref,
                  c_ref, n_ref, m_ref, *, heads):
    H = heads
    L = q_ref.shape[0]
    dk = q_ref.shape[1] // H
    dv = v_ref.shape[1] // H

    @pl.when(pl.program_id(0) == 0)
    def _():
        c_ref[...] = jnp.zeros_like(c_ref)
        n_ref[...] = jnp.zeros_like(n_ref)
        m_ref[...] = jnp.zeros_like(m_ref)

    pre = gates_ref[...] + bias_ref[...]
    capped = GATE_CAP * jnp.tanh(pre / GATE_CAP)
    lsig = _log_sigmoid(capped)
    lane = lax.broadcasted_iota(jnp.int32, capped.shape, 1)
    row = lax.broadcasted_iota(jnp.int32, (L, L), 0)
    col = lax.broadcasted_iota(jnp.int32, (L, L), 1)
    eye = row == col
    lower = col <= row

    def gate_terms(h):
        li_col = jnp.sum(jnp.where(lane == h, capped, 0.0), axis=1, keepdims=True)
        lf_col = jnp.sum(jnp.where(lane == h + H, lsig, 0.0), axis=1, keepdims=True)
        lf_row = jnp.sum(jnp.where(eye, lf_col, 0.0), axis=0, keepdims=True)
        li_row = jnp.sum(jnp.where(eye, li_col, 0.0), axis=0, keepdims=True)
        b_col = jnp.sum(jnp.where(lower, lf_row, 0.0), axis=1, keepdims=True)
        b_row = jnp.sum(jnp.where(row <= col, lf_col, 0.0), axis=0, keepdims=True)
        g_row = li_row - b_row
        g_col = li_col - b_col
        m_prev = m_ref[h][:, 0:1]
        a_col = jnp.maximum(m_prev, jnp.max(jnp.where(lower, g_row, -jnp.inf), axis=1, keepdims=True))
        a_last = jnp.max(a_col, axis=0, keepdims=True)
        b_last = jnp.sum(lf_col, axis=0, keepdims=True)
        return dict(dmat=jnp.where(lower, jnp.exp(g_row - a_col), 0.0), g_inter=jnp.exp(m_prev - a_col),
                    floor=jnp.exp(-(b_col + a_col)), w_col=jnp.exp(g_col - a_last),
                    decay=jnp.exp(m_prev - a_last), m_new=b_last + a_last)

    hs = range(H)
    gt = [gate_terms(h) for h in hs]
    q = [q_ref[:, h * dk:(h + 1) * dk] * (dk ** -0.5) for h in hs]
    k = [k_ref[:, h * dk:(h + 1) * dk] for h in hs]
    vb = [v_ref[:, h * dv:(h + 1) * dv].astype(BF16) for h in hs]
    qb = [x.astype(BF16) for x in q]
    cst = [c_ref[h] for h in hs]
    s = _bdot(qb, [x.astype(BF16) for x in k], "nt")
    s = [s[h] * gt[h]["dmat"] for h in hs]
    qc = _bdot(qb, [c.astype(BF16) for c in cst], "nn")
    sv = _bdot([x.astype(BF16) for x in s], vb, "nn")
    kw = [k[h] * gt[h]["w_col"] for h in hs]
    upd = _bdot([x.astype(BF16) for x in kw], vb, "tn")
    for h in hs:
        g = gt[h]
        num = sv[h] + g["g_inter"] * qc[h]
        qn = jnp.sum(q[h] * n_ref[h], axis=1, keepdims=True)
        den = jnp.sum(s[h], axis=1, keepdims=True) + g["g_inter"] * qn
        hh = num / jnp.maximum(jnp.abs(den), g["floor"])
        c_ref[h] = g["decay"] * cst[h] + upd[h]
        n_ref[h] = g["decay"] * n_ref[h] + jnp.sum(kw[h], axis=0, keepdims=True)
        m_ref[h] = jnp.broadcast_to(g["m_new"], m_ref.shape[1:])
        y = hh * lax.rsqrt(jnp.mean(hh * hh, axis=-1, keepdims=True) + NORM_EPS)
        cs = slice(h * dv, (h + 1) * dv)
        o_ref[:, cs] = (y * ng_ref[:, cs] * _sigmoid(og_ref[:, cs])).astype(o_ref.dtype)


def _mlstm(proj, gates, bias_row, norm_g, dk, dv, y):
    T = proj.shape[0]
    H = MLSTM_HEADS
    L = MLSTM_CHUNK
    wk, wv = H * dk, H * dv
    assert (2 * wk) % wv == 0
    v0 = (2 * wk) // wv
    return pl.pallas_call(
        functools.partial(_mlstm_kernel, heads=H),
        grid=(T // L,),
        in_specs=[pl.BlockSpec((L, wk), lambda c: (c, 0)),
                  pl.BlockSpec((L, wk), lambda c: (c, 1)),
                  pl.BlockSpec((L, wv), lambda c: (c, v0)),
                  pl.BlockSpec((L, wv), lambda c: (c, v0 + 1)),
                  pl.BlockSpec((L, LANES), lambda c: (c, 0)),
                  pl.BlockSpec((1, LANES), lambda c: (0, 0)),
                  pl.BlockSpec((1, wv), lambda c: (0, 0)),
                  pl.BlockSpec(memory_space=pl.ANY)],
        out_specs=pl.BlockSpec((L, wv), lambda c: (c, 0)),
        out_shape=jax.ShapeDtypeStruct(y.shape, y.dtype),
        input_output_aliases={7: 0},
        scratch_shapes=[pltpu.VMEM((H, dk, dv), F32), pltpu.VMEM((H, 1, dk), F32), pltpu.VMEM((H, 1, LANES), F32)],
        compiler_params=_cparams(("arbitrary",)),
        name="mlstm",
    )(proj, proj, proj, proj, gates, bias_row, norm_g.reshape(1, wv).astype(F32), y)


def _hgrn_kernel(q_ref, f_ref, i_ref, g_ref, lb_ref, ng_ref, y_hbm_ref, o_ref, st_ref, *, lb_row, chunk):
    TB = q_ref.shape[0]
    dk = HGRN_HEAD_DIM
    GW = st_ref.shape[1]
    NG = q_ref.shape[1] // GW
    HG = GW // dk
    C = chunk

    @pl.when(pl.program_id(1) == 0)
    def _():
        st_ref[...] = jnp.zeros_like(st_ref)

    logits = lb_ref[...].astype(F32)
    e = jnp.exp(logits - jnp.max(logits, axis=0, keepdims=True))
    p = e / jnp.sum(e, axis=0, keepdims=True)
    lb_all = jnp.sum(p[0:lb_row + 1], axis=0, keepdims=True)
    ng_all = ng_ref[...]
    srow = lax.broadcasted_iota(jnp.int32, (C, GW), 0)
    srow_h = srow[:, :dk]
    same = _same_head(GW, dk)

    def elementwise(g, ins):
        qb, fb, vb = ins
        lb = lb_all[:, g * GW:(g + 1) * GW]
        f = lb + (1.0 - lb) * _sigmoid(fb)
        kk = 1.0 - f
        qh = qb * _sigmoid(qb)
        A = jnp.log2(f)
        sh = 1
        while sh < C:
            A = A + jnp.where(srow >= sh, pltpu.roll(A, sh, axis=0), 0.0)
            sh *= 2
        a_last = A[C - 1:C]
        intra = []
        for h in range(HG):
            hs = slice(h * dk, (h + 1) * dk)
            Ah, kh, qhh, vh = A[:, hs], kk[:, hs], qh[:, hs], vb[:, hs]
            rows = []
            for t in range(C):
                rel = jnp.where(srow_h <= t, jnp.exp2(Ah[t:t + 1] - Ah), 0.0)
                sc = jnp.sum(rel * kh * qhh[t:t + 1], axis=1, keepdims=True)
                rows.append(jnp.sum(sc * vh, axis=0, keepdims=True))
            intra.append(jnp.concatenate(rows, axis=0))
        qe = (qh * jnp.exp2(A)).astype(BF16)
        ke = (kk * jnp.exp2(a_last - A)).astype(BF16)
        return qe, ke, vb.astype(BF16), jnp.concatenate(intra, axis=1), jnp.exp2(a_last)

    def body(ci, carry):
        rs = pl.ds(pl.multiple_of(ci * C, C), C)
        G = range(NG)
        gsl = [slice(g * GW, (g + 1) * GW) for g in G]
        st = [st_ref[g] for g in G]
        ew = [elementwise(g, tuple(ref[rs, gsl[g]] for ref in (q_ref, f_ref, i_ref))) for g in G]
        o_inter = _bdot([e[0] for e in ew], [s.astype(BF16) for s in st], "nt")
        upd = _bdot([e[2] for e in ew], [e[1] for e in ew], "tn")
        for g in G:
            st_ref[g] = st[g] * ew[g][4] + (upd[g] if HG == 1 else jnp.where(same, upd[g], 0.0))
            o = o_inter[g] + ew[g][3]
            ys = []
            for h in range(HG):
                oh = o[:, h * dk:(h + 1) * dk]
                ys.append(oh * lax.rsqrt(jnp.mean(oh * oh, axis=-1, keepdims=True) + NORM_EPS))
            gb = g_ref[rs, gsl[g]]
            o_ref[rs, gsl[g]] = (jnp.concatenate(ys, axis=1) * ng_all[:, gsl[g]]
                                 * (gb * _sigmoid(gb))).astype(o_ref.dtype)
        return carry

    lax.fori_loop(0, TB // C, body, 0)


def _hgrn(proj, lb_logits, lb_row, norm_g, y, tb=256):
    T = proj.shape[0]
    W = proj.shape[1] // 4
    sw = _pick(W, STEP_LANES)
    gw = HGRN_HEAD_DIM
    H = W // sw
    tb = _pick(T, tb)
    R = lb_logits.shape[0]
    assert (y.shape[1] - W) % sw == 0
    off = (y.shape[1] - W) // sw
    return pl.pallas_call(
        functools.partial(_hgrn_kernel, lb_row=lb_row, chunk=HGRN_CHUNK),
        grid=(H, T // tb),
        in_specs=[pl.BlockSpec((tb, sw), lambda h, c: (c, h)),
                  pl.BlockSpec((tb, sw), lambda h, c: (c, H + h)),
                  pl.BlockSpec((tb, sw), lambda h, c: (c, 2 * H + h)),
                  pl.BlockSpec((tb, sw), lambda h, c: (c, 3 * H + h)),
                  pl.BlockSpec((R, sw), lambda h, c: (0, h)),
                  pl.BlockSpec((1, sw), lambda h, c: (0, h)),
                  pl.BlockSpec(memory_space=pl.ANY)],
        out_specs=pl.BlockSpec((tb, sw), lambda h, c: (c, off + h)),
        out_shape=jax.ShapeDtypeStruct(y.shape, y.dtype),
        input_output_aliases={6: 0},
        scratch_shapes=[pltpu.VMEM((sw // gw, gw, gw), F32)],
        compiler_params=_cparams(("parallel", "arbitrary")),
        name="hgrn2",
    )(proj, proj, proj, proj, lb_logits.astype(F32), norm_g.reshape(1, W).astype(F32), y)


def _xattn_weights(w_q, w_o, layer, mem_kv):
    D = w_q.shape[1]
    M = mem_kv.shape[0]
    H = XATTN_HEADS
    d = D // H
    tm = _pick(D, 1024)
    wqk = pl.pallas_call(
        functools.partial(_mm_blocks_kernel, trans_b=True),
        grid=(D // tm, H),
        in_specs=[pl.BlockSpec((None, tm, d), lambda i, h: (layer, i, h)),
                  pl.BlockSpec((M, d), lambda i, h: (0, h))],
        out_specs=pl.BlockSpec((tm, M), lambda i, h: (i, h)),
        out_shape=jax.ShapeDtypeStruct((D, H * M), BF16),
        compiler_params=_cparams(("parallel", "parallel")),
        name="xattn_wqk",
    )(w_q, mem_kv)
    tn = _pick(D, 1024)
    vwo = pl.pallas_call(
        functools.partial(_mm_blocks_kernel, trans_b=False),
        grid=(H, D // tn),
        in_specs=[pl.BlockSpec((M, d), lambda h, j: (0, H + h)),
                  pl.BlockSpec((None, d, tn), lambda h, j: (layer, h, j))],
        out_specs=pl.BlockSpec((M, tn), lambda h, j: (h, j)),
        out_shape=jax.ShapeDtypeStruct((H * M, D), BF16),
        compiler_params=_cparams(("parallel", "parallel")),
        name="xattn_vwo",
    )(mem_kv, w_o)
    return wqk, vwo


def _shift_kernel(h_ref, prev_ref, g_ref, mu_ref, *o_refs):
    def norm(h):
        return h * lax.rsqrt(jnp.mean(h * h, axis=-1, keepdims=True) + NORM_EPS) * g_ref[...]

    x = norm(h_ref[...])
    n = prev_ref.shape[0]
    last = jnp.where(pl.program_id(0) == 0, 0.0, norm(prev_ref[...])[n - 1:n, :])
    row = lax.broadcasted_iota(jnp.int32, x.shape, 0)
    xx = jnp.where(row == 0, last, pltpu.roll(x, 1, axis=0)) - x
    for j, o_ref in enumerate(o_refs):
        o_ref[...] = (x + xx * mu_ref[j:j + 1, :]).astype(o_ref.dtype)


def _rwkv_shift(h, g, mu, tm=256):
    T, D = h.shape
    J = mu.shape[0]
    tm = _pick(T, tm)
    pr = 8
    spec = pl.BlockSpec((tm, D), lambda i: (i, 0))
    return pl.pallas_call(
        _shift_kernel,
        grid=(T // tm,),
        in_specs=[spec,
                  pl.BlockSpec((pr, D), lambda i: (jnp.maximum(i * (tm // pr) - 1, 0), 0)),
                  pl.BlockSpec((1, D), lambda i: (0, 0)),
                  pl.BlockSpec((J, D), lambda i: (0, 0))],
        out_specs=[spec] * J,
        out_shape=[jax.ShapeDtypeStruct((T, D), BF16)] * J,
        compiler_params=_cparams(("parallel",)),
        name="rwkv_shift",
    )(h, h, g.reshape(1, D).astype(F32), mu.astype(F32))


def _split2(x):
    x1 = x.astype(BF16)
    return x1, (x - x1.astype(F32)).astype(BF16)


def _dot_exact_rhs(x, m):
    n = x.shape[0]
    d = jnp.dot(jnp.concatenate(_split2(x), axis=0), m, preferred_element_type=F32)
    return d[:n] + d[n:]


def _head_ones(n, head, scale, dtype):
    return jnp.where(_same_head(n, head), scale, 0.0).astype(dtype)


def _rwkv_prep_kernel(r_ref, k_ref, v_ref, lw_ref, la_ref, par_ref,
                      at_ref, rt_ref, bk_ref, tc_ref, rb_ref, akv_ref, rkv_ref, bon_ref, gl_ref, *, chunk):
    TB = r_ref.shape[0]
    L = chunk
    N = RWKV_HEAD
    GW = 4 * L
    HG = GW // N
    NG = r_ref.shape[1] // GW
    assert L == N and GW == MXU_DIM
    par = par_ref[...]
    seg1 = _head_ones(GW, N, 1.0, BF16)
    hmask = _head_masks(L, GW, N)
    lane = lax.broadcasted_iota(jnp.int32, (L, GW), 1)
    rowi = lax.broadcasted_iota(jnp.int32, (L, GW), 0)
    half = GW // 2
    lane_h = lax.broadcasted_iota(jnp.int32, (L, half), 1)
    row_h = lax.broadcasted_iota(jnp.int32, (L, half), 0)
    low = lane_h < N
    src = jnp.where(low, lane_h, lane_h - N)
    strict = src < row_h
    incl = src <= row_h
    eye_half = jnp.where(src == row_h, 1.0, 0.0)
    eye_cat = jnp.concatenate([eye_half, eye_half], axis=1)

    def elementwise(g, ins):
        gs = slice(g * GW, (g + 1) * GW)
        w0, a0, k_k, k_a, r_k = (par[i:i + 1, gs] for i in range(5))
        r, k, v, lw, la = ins
        logw = -(RWKV_DECAY_SCALE * LOG2E) * _sigmoid(w0 + lw)
        a_sig = _sigmoid(a0 + la)
        kkr = k * k_k
        k2 = k * (1.0 + (a_sig - 1.0) * k_a)
        cs = logw
        sh = 1
        while sh < L:
            cs = cs + jnp.where(rowi >= sh, pltpu.roll(cs, sh, axis=0), 0.0)
            sh *= 2
        return r, v, logw, a_sig, kkr, k2, cs, r * k2 * r_k

    def pre_dots(ew, ss, bsum):
        r, v, logw, a_sig, kkr, k2, cs, _ = ew
        kk = kkr / jnp.maximum(jnp.sqrt(ss), 1e-12)
        ginv = jnp.exp2(-cs)
        atb = (-kk * jnp.exp2(cs - logw)).astype(BF16)
        rtb = (r * jnp.exp2(cs)).astype(BF16)
        bt = (kk * a_sig * ginv).astype(BF16)
        kt = (k2 * ginv).astype(BF16)
        zero = jnp.zeros_like(atb)
        lhs = jnp.concatenate([jnp.where(m, x, zero) for m in hmask for x in (atb, rtb)], axis=0)
        return atb, rtb, bt, kt, bsum * v, jnp.exp2(cs[L - 1:L]), lhs, jnp.concatenate([bt, kt, kt, bt], axis=0)

    def cat(parts, even_sel, odd_sel, mask):
        cols = [jnp.where(mask, jnp.where(low, even_sel(parts[2 * c]), odd_sel(parts[2 * c + 1])), 0.0)
                for c in range(HG // 2)]
        return jnp.concatenate(cols, axis=1)

    lo = lambda x: x[:, :half]
    hi = lambda x: x[:, half:]

    def body(ci, carry):
        r0 = pl.multiple_of(ci * L, L)
        sl = pl.ds(r0, L)
        G = range(NG)
        ews = [elementwise(g, tuple(ref[sl, g * GW:(g + 1) * GW]
                                    for ref in (r_ref, k_ref, v_ref, lw_ref, la_ref))) for g in G]
        sums = _dot_exact_rhs(jnp.concatenate([x for ew in ews for x in (ew[4] * ew[4], ew[7])], axis=0), seg1)
        pre = [pre_dots(ews[g], sums[2 * g * L:(2 * g + 1) * L], sums[(2 * g + 1) * L:(2 * g + 2) * L]) for g in G]
        P = _bdot([p[6] for p in pre], [p[7] for p in pre], "nt")
        pat = [[P[g][2 * h * L:(2 * h + 1) * L] for h in range(HG)] for g in G]
        prt = [[P[g][(2 * h + 1) * L:(2 * h + 2) * L] for h in range(HG)] for g in G]
        ncat = [cat(pat[g], lo, hi, strict) for g in G]
        nb = [n.astype(BF16) for n in ncat]
        pw = _bdot(nb, [_stack_heads(n, hmask) for n in nb], "nn")
        tinv = [eye_cat + n for n in ncat]
        power = 2
        while power < L:
            pwb = [x.astype(BF16) for x in pw]
            pst = [_stack_heads(x, hmask) for x in pwb]
            if 2 * power < L:
                d = _bdot([jnp.concatenate([tinv[g].astype(BF16), pwb[g]], axis=0) for g in G], pst, "nn")
                tinv = [tinv[g] + d[g][:L] for g in G]
                pw = [d[g][L:] for g in G]
            else:
                d = _bdot([t.astype(BF16) for t in tinv], pst, "nn")
                tinv = [tinv[g] + d[g] for g in G]
            power *= 2
        akrk = [jnp.concatenate([cat(pat[g], hi, lo, strict), cat(prt[g], hi, lo, incl)], axis=0).astype(BF16)
                for g in G]
        kv = _bdot(akrk, [_stack_heads(ews[g][1].astype(BF16), hmask) for g in G], "nn")
        for g in G:
            gs = slice(g * GW, (g + 1) * GW)
            atb, rtb, bt, kt, bon, gl = pre[g][:6]
            at_ref[sl, gs] = atb
            rt_ref[sl, gs] = rtb
            bk_ref[pl.ds(pl.multiple_of(2 * r0, 2 * L), L), gs] = bt
            bk_ref[pl.ds(pl.multiple_of(2 * r0 + L, L), L), gs] = kt
            tc_ref[sl, gs] = tinv[g].astype(BF16)
            rb_ref[sl, gs] = cat(prt[g], lo, hi, incl).astype(BF16)
            akv_ref[sl, gs] = kv[g][:L]
            rkv_ref[sl, gs] = kv[g][L:]
            bon_ref[sl, gs] = bon
            gl_ref[ci, :, gs] = gl
        return carry

    lax.fori_loop(0, TB // L, body, 0)


def _rwkv_state_kernel(at_ref, rt_ref, bk_ref, tc_ref, rb_ref, akv_ref, rkv_ref, gl_ref, v_ref,
                       y_ref, ht_ref, *, chunk):
    TB = v_ref.shape[0]
    L = chunk
    N = RWKV_HEAD
    GW = ht_ref.shape[1]
    NG = v_ref.shape[1] // GW

    @pl.when(pl.program_id(1) == 0)
    def _():
        ht_ref[...] = jnp.zeros_like(ht_ref)

    hmask = _head_masks(L, GW, N)
    same = _same_head(GW, N)

    def body(ci, carry):
        r0 = pl.multiple_of(ci * L, L)
        sl = pl.ds(r0, L)
        sl2 = pl.ds(pl.multiple_of(2 * r0, 2 * L), 2 * L)
        G = range(NG)
        gsl = [slice(g * GW, (g + 1) * GW) for g in G]
        ht = [ht_ref[g] for g in G]
        xr = _bdot([jnp.concatenate([at_ref[sl, gs], rt_ref[sl, gs]], axis=0) for gs in gsl],
                   [h.astype(BF16) for h in ht], "nt")
        xb = [(xr[g][:L] + akv_ref[sl, gsl[g]]).astype(BF16) for g in G]
        u = _bdot([tc_ref[sl, gs] for gs in gsl], [_stack_heads(x, hmask) for x in xb], "nn")
        ub = [x.astype(BF16) for x in u]
        yd = _bdot([rb_ref[sl, gs] for gs in gsl], [_stack_heads(x, hmask) for x in ub], "nn")
        dht = _bdot([jnp.concatenate([ub[g], v_ref[sl, gsl[g]].astype(BF16)], axis=0) for g in G],
                    [bk_ref[sl2, gs] for gs in gsl], "tn")
        for g in G:
            ht_ref[g] = (ht[g] + jnp.where(same, dht[g], 0.0)) * gl_ref[ci, :, gsl[g]]
            y_ref[sl, gsl[g]] = xr[g][L:] + yd[g] + rkv_ref[sl, gsl[g]]
        return carry

    lax.fori_loop(0, TB // L, body, 0)


def _rwkv_out_kernel(y_ref, bon_ref, g_ref, ln_ref, o_ref):
    N = RWKV_HEAD
    GW = MXU_DIM
    seg_mean = _head_ones(GW, N, 1.0 / N, BF16)
    for g in range(y_ref.shape[1] // GW):
        gs = slice(g * GW, (g + 1) * GW)
        y = y_ref[:, gs]
        yc = y - _dot_exact_rhs(y, seg_mean)
        var = _dot_exact_rhs(yc * yc, seg_mean)
        yn = yc * lax.rsqrt(var + RWKV_LNX_EPS) * ln_ref[0:1, gs] + ln_ref[1:2, gs]
        o_ref[:, gs] = ((yn + bon_ref[:, gs]) * g_ref[:, gs]).astype(o_ref.dtype)


def _rwkv_core(r, k, v, lw, la, g, w0, a0, k_k, k_a, r_k, lnx_w, lnx_b, tb=128):
    T, D = r.shape
    L = RWKV_CHUNK
    gw = MXU_DIM
    W = _pick(D, STEP_LANES)
    assert W % gw == 0
    P = D // W
    tb = _pick(T, tb)
    nc = tb // L
    par = jnp.stack([w0, a0, k_k, k_a, r_k.reshape(D)]).astype(F32)
    par = jnp.concatenate([par, jnp.zeros((3, D), F32)], axis=0)
    spec = pl.BlockSpec((tb, W), lambda p, c: (c, p))
    spec2 = pl.BlockSpec((2 * tb, W), lambda p, c: (c, p))
    gspec = pl.BlockSpec((nc, 1, W), lambda p, c: (c, 0, p))
    bf = jax.ShapeDtypeStruct((T, D), BF16)
    f32 = jax.ShapeDtypeStruct((T, D), F32)
    at, rt, bk, tc, rb, akv, rkv, bon, gl = pl.pallas_call(
        functools.partial(_rwkv_prep_kernel, chunk=L),
        grid=(P, T // tb),
        in_specs=[spec] * 5 + [pl.BlockSpec((8, W), lambda p, c: (0, p))],
        out_specs=[spec, spec, spec2, spec, spec, spec, spec, spec, gspec],
        out_shape=[bf, bf, jax.ShapeDtypeStruct((2 * T, D), BF16), bf, bf, f32, f32, f32,
                   jax.ShapeDtypeStruct((T // L, 1, D), F32)],
        compiler_params=_cparams(("parallel", "parallel")),
        name="rwkv_prep",
    )(r, k, v, lw, la, par)
    y = pl.pallas_call(
        functools.partial(_rwkv_state_kernel, chunk=L),
        grid=(P, T // tb),
        in_specs=[spec, spec, spec2, spec, spec, spec, spec, gspec, spec],
        out_specs=spec,
        out_shape=f32,
        scratch_shapes=[pltpu.VMEM((W // gw, gw, gw), F32)],
        compiler_params=_cparams(("parallel", "arbitrary")),
        name="rwkv_state",
    )(at, rt, bk, tc, rb, akv, rkv, gl, v)
    ln = jnp.stack([lnx_w, lnx_b]).astype(F32)
    ln = jnp.concatenate([ln, jnp.zeros((6, D), F32)], axis=0)
    return pl.pallas_call(
        _rwkv_out_kernel,
        grid=(P, T // tb),
        in_specs=[spec, spec, spec, pl.BlockSpec((8, W), lambda p, c: (0, p))],
        out_specs=spec,
        out_shape=bf,
        compiler_params=_cparams(("parallel", "parallel")),
        name="rwkv_out",
    )(y, bon, g, ln)


def _mixer_even(h, norm_g, w_in, j, b_i, b_f, mlstm_g, lb_logits, hgrn_g, w_out):
    D = h.shape[1]
    H = MLSTM_HEADS
    mix_a = D // 2
    dv = mix_a // H
    dk = dv // 2
    na = 2 * H * dk + 2 * mix_a
    hn = _rmsnorm(h, norm_g, BF16)
    w_if = jnp.zeros((D, LANES), F32).at[:, :2 * H].set(w_in[j, :, na:na + 2 * H])
    w_b = _shift_cols(w_in, j, na + 2 * H, w_in.shape[2] - na - 2 * H)
    proj_a = _mm(hn, w_in, layer=j, cols=(0, na), name="mm_mlstm_in")
    gates = _mm(hn, w_if, name="mm_gates")
    proj_b = _mm(hn, w_b, name="mm_hgrn_in")
    bias_row = jnp.zeros((1, LANES), F32).at[0, :H].set(b_i.astype(F32)).at[0, H:2 * H].set(b_f.astype(F32))
    y = jnp.zeros((h.shape[0], D), BF16)
    y = _mlstm(proj_a, gates, bias_row, mlstm_g, dk, dv, y)
    y = _hgrn(proj_b, lb_logits, j, hgrn_g, y)
    return _mm(y, w_out, layer=j, res=h, name="mm_mix_out")


def _mixer_odd(h, norm_g, j, mu, w0, w1, w2, a0, a1, a2, g1, g2, k_k, k_a, r_k, w_r, w_k, w_v, w_o, lnx_w, lnx_b):
    s_r, s_w, s_k, s_v, s_a, s_g = _rwkv_shift(h, norm_g, mu[j])
    r = _mm(s_r, w_r, layer=j, name="mm_rwkv_r")
    k = _mm(s_k, w_k, layer=j, name="mm_rwkv_k")
    v = _mm(s_v, w_v, layer=j, name="mm_rwkv_v")
    wide = dict(tm=512, tn=h.shape[1])
    lw = _mm(_mm(s_w, w1, layer=j, act="tanh", out_dtype=BF16, name="mm_lora_w1"), w2, layer=j,
             name="mm_lora_w2", **wide)
    la = _mm(_mm(s_a, a1, layer=j, out_dtype=BF16, name="mm_lora_a1"), a2, layer=j, name="mm_lora_a2", **wide)
    g = _mm(_mm(s_g, g1, layer=j, act="sigmoid", out_dtype=BF16, name="mm_lora_g1"), g2, layer=j,
            name="mm_lora_g2", **wide)
    y = _rwkv_core(r, k, v, lw, la, g, w0[j], a0[j], k_k[j], k_a[j], r_k[j], lnx_w[j], lnx_b[j])
    return _mm(y, w_o, layer=j, res=h, name="mm_rwkv_out")


def kernel(x, mem, norm_mix_g, norm_xattn_g, norm_mlp_g, final_norm_g, mem_norm_g, ab_w_in, mlstm_b_i, mlstm_b_f, mlstm_norm_g, hgrn_lb_logits, hgrn_norm_g, ab_w_out, rwkv_mu, rwkv_w0, rwkv_w1, rwkv_w2, rwkv_a0, rwkv_a1, rwkv_a2, rwkv_g1, rwkv_g2, rwkv_k_k, rwkv_k_a, rwkv_r_k, rwkv_w_r, rwkv_w_k, rwkv_w_v, rwkv_w_o, rwkv_lnx_w, rwkv_lnx_b, xattn_w_q, xattn_w_o, mem_w_kv, mlp_w_up, mlp_w_down):
    B, T, D = x.shape
    depth = norm_mix_g.shape[0]
    M = mem.shape[1]
    xscale = (D // XATTN_HEADS) ** -0.5
    outs = []
    for b in range(B):
        mem_kv = _mm(_rmsnorm(mem[b], mem_norm_g, BF16), mem_w_kv, out_dtype=BF16, name="mm_mem_kv")
        h = x[b]
        for layer in range(depth):
            j = layer // 2
            if layer % 2 == 0:
                h = _mixer_even(h, norm_mix_g[layer], ab_w_in, j, mlstm_b_i[j], mlstm_b_f[j], mlstm_norm_g[j],
                                hgrn_lb_logits, hgrn_norm_g[j], ab_w_out)
            else:
                h = _mixer_odd(h, norm_mix_g[layer], j, rwkv_mu, rwkv_w0, rwkv_w1, rwkv_w2, rwkv_a0, rwkv_a1,
                               rwkv_a2, rwkv_g1, rwkv_g2, rwkv_k_k, rwkv_k_a, rwkv_r_k,
                               rwkv_w_r, rwkv_w_k, rwkv_w_v, rwkv_w_o, rwkv_lnx_w, rwkv_lnx_b)
            wqk, vwo = _xattn_weights(xattn_w_q, xattn_w_o, layer, mem_kv)
            p = _mm(_rmsnorm(h, norm_xattn_g[layer], BF16), wqk, act="softmax", seg=(xscale, M),
                    out_dtype=BF16, tm=1024, tn=XATTN_HEADS * M, name="mm_xattn_scores")
            h = _mm(p, vwo, res=h, tn=1024, name="mm_xattn_out")
            up = _mm(_rmsnorm(h, norm_mlp_g[layer], BF16), mlp_w_up, layer=layer,
                     act="relu2", out_dtype=BF16, name="mm_mlp_up")
            for k0 in range(0, up.shape[1], D):
                h = _mm(up, mlp_w_down, layer=layer, res=h, krange=(k0, D), name="mm_mlp_down")
        outs.append(_rmsnorm(h, final_norm_g, F32))
    return jnp.stack(outs)
```

```python
import functools
import math

import jax
import jax.numpy as jnp
from jax import lax
from jax.experimental import pallas as pl
from jax.experimental.pallas import tpu as pltpu

F32 = jnp.float32
BF16 = jnp.bfloat16

NORM_EPS = 1e-6
GATE_CAP = 15.0
MLSTM_HEADS = 4
MLSTM_CHUNK = 64
HGRN_HEAD_DIM = 128
HGRN_CHUNK = 16
RWKV_HEAD = 64
RWKV_CHUNK = 64
RWKV_LNX_EPS = 64e-5
RWKV_DECAY_SCALE = math.exp(-0.5)
LOG2E = math.log2(math.e)
XATTN_HEADS = 4
LANES = 128
MXU_DIM = 256
STEP_LANES = 2048
VMEM_LIMIT = 60 * 1024 * 1024

NN = (((1,), (0,)), ((), ()))
NT = (((1,), (1,)), ((), ()))
TN = (((0,), (0,)), ((), ()))


def _cparams(sem):
    return pltpu.CompilerParams(dimension_semantics=sem, vmem_limit_bytes=VMEM_LIMIT)


def _sigmoid(x):
    return 1.0 / (1.0 + jnp.exp(-x))


def _log_sigmoid(x):
    return jnp.minimum(x, 0.0) - jnp.log1p(jnp.exp(-jnp.abs(x)))


def _pick(dim, pref):
    t = min(dim, pref)
    while dim % t:
        t //= 2
    return t


def _head_masks(rows, width, head):
    lane = lax.broadcasted_iota(jnp.int32, (rows, width), 1) // head
    return [lane == h for h in range(width // head)]


def _stack_heads(x, masks):
    z = jnp.zeros_like(x)
    return jnp.concatenate([jnp.where(m, x, z) for m in masks], axis=0)


_BDOT = {"nn": "gmk,gkn->gmn", "nt": "gmk,gnk->gmn", "tn": "gtm,gtn->gmn"}


def _bdot(xs, ys, mode):
    d = jnp.einsum(_BDOT[mode], jnp.stack(xs), jnp.stack(ys), preferred_element_type=F32)
    return [d[g] for g in range(len(xs))]


def _same_head(n, head):
    r = lax.broadcasted_iota(jnp.int32, (n, n), 0) // head
    c = lax.broadcasted_iota(jnp.int32, (n, n), 1) // head
    return r == c


def _mm_kernel(*refs, nk, act, has_res, seg, b_t):
    if has_res:
        a_ref, b_ref, res_ref, o_ref = refs[:4]
        rest = refs[4:]
    else:
        a_ref, b_ref, o_ref = refs[:3]
        res_ref = None
        rest = refs[3:]

    def finish(r):
        if act == "relu2":
            r = jnp.square(jnp.maximum(r, 0.0))
        elif act == "tanh":
            r = jnp.tanh(r)
        elif act == "sigmoid":
            r = _sigmoid(r)
        elif act == "softmax":
            scale, width = seg
            parts = []
            for j in range(r.shape[1] // width):
                s = r[:, j * width:(j + 1) * width] * scale
                e = jnp.exp(s - jnp.max(s, axis=-1, keepdims=True))
                parts.append(e / jnp.sum(e, axis=-1, keepdims=True))
            r = jnp.concatenate(parts, axis=1)
        if has_res:
            r = r + res_ref[...].astype(F32)
        o_ref[...] = r.astype(o_ref.dtype)

    b = b_ref[0] if b_t else b_ref[...]
    part = lax.dot_general(a_ref[...].astype(BF16), b.astype(BF16), NT if b_t else NN,
                           preferred_element_type=F32)
    if nk == 1:
        finish(part)
    else:
        acc_ref = rest[0]
        k = pl.program_id(2)

        @pl.when(k == 0)
        def _():
            acc_ref[...] = part

        @pl.when(k > 0)
        def _():
            acc_ref[...] += part

        @pl.when(k == nk - 1)
        def _():
            finish(acc_ref[...])


def _mm_tiles(M, N, K, has_res):
    if has_res and K >= 4096 and M % 2048 == 0 and N % MXU_DIM == 0:
        return 2048, MXU_DIM, 4096
    return 1024, 512, 4096


def _mm(a, b, *, out_dtype=F32, act=None, res=None, seg=None, tm=None, tn=None, tk=None, name="mm",
        layer=None, cols=None, krange=None, b_t=False):
    M, Ka = a.shape
    K2, nb = b.shape[-2:][::-1] if b_t else b.shape[-2:]
    assert Ka == K2 and (b.ndim == 3) == (layer is not None)
    c0, N = cols if cols is not None else (0, nb)
    kf, K = krange if krange is not None else (0, Ka)
    dm, dn, dk = _mm_tiles(M, N, K, res is not None)
    tm, tn, tk = _pick(M, tm or dm), _pick(N, tn or dn), _pick(K, tk or dk)
    assert kf % tk == 0
    k0 = kf // tk
    nk = K // tk
    if b_t:
        assert layer is not None and c0 % 8 == 0
        b_spec = pl.BlockSpec((pl.Element(1), pl.Element(tn), pl.Element(tk)),
                              lambda i, j, k: (layer, pl.multiple_of(c0 + j * tn, 8), (k0 + k) * tk))
    else:
        assert c0 % tn == 0
        j0 = c0 // tn
        if layer is None:
            b_spec = pl.BlockSpec((tk, tn), lambda i, j, k: (k0 + k, j0 + j))
        else:
            b_spec = pl.BlockSpec((None, tk, tn), lambda i, j, k: (layer, k0 + k, j0 + j))
    in_specs = [pl.BlockSpec((tm, tk), lambda i, j, k: (i, k0 + k)), b_spec]
    args = [a, b]
    if res is not None:
        in_specs.append(pl.BlockSpec((tm, tn), lambda i, j, k: (i, j)))
        args.append(res)
    scratch = [] if nk == 1 else [pltpu.VMEM((tm, tn), F32)]
    return pl.pallas_call(
        functools.partial(_mm_kernel, nk=nk, act=act, has_res=res is not None, seg=seg, b_t=b_t),
        grid=(M // tm, N // tn, nk),
        in_specs=in_specs,
        out_specs=pl.BlockSpec((tm, tn), lambda i, j, k: (i, j)),
        out_shape=jax.ShapeDtypeStruct((M, N), out_dtype),
        scratch_shapes=scratch,
        compiler_params=_cparams(("parallel", "parallel", "arbitrary")),
        name=name,
    )(*args)


def _mm_blocks_kernel(a_ref, b_ref, o_ref, *, trans_b):
    a = a_ref[...].astype(BF16)
    b = b_ref[...].astype(BF16)
    if trans_b:
        r = lax.dot_general(a, b, NT, preferred_element_type=F32)
    else:
        r = jnp.dot(a, b, preferred_element_type=F32)
    o_ref[...] = r.astype(o_ref.dtype)


def _rmsnorm_kernel(x_ref, g_ref, o_ref):
    x = x_ref[...].astype(F32)
    y = x * lax.rsqrt(jnp.mean(x * x, axis=-1, keepdims=True) + NORM_EPS)
    o_ref[...] = (y * g_ref[...]).astype(o_ref.dtype)


def _rmsnorm(x, g, out_dtype, tm=512):
    M, D = x.shape
    tm = _pick(M, tm)
    return pl.pallas_call(
        _rmsnorm_kernel,
        grid=(M // tm,),
        in_specs=[pl.BlockSpec((tm, D), lambda i: (i, 0)),
                  pl.BlockSpec((1, D), lambda i: (0, 0))],
        out_specs=pl.BlockSpec((tm, D), lambda i: (i, 0)),
        out_shape=jax.ShapeDtypeStruct((M, D), out_dtype),
        compiler_params=_cparams(("parallel",)),
        name="rmsnorm",
    )(x, g.reshape(1, D).astype(F32))


def _mlstm_kernel(q_ref, k_ref, v_ref, og_ref, gates_ref, bias_ref, ng_ref, y_hbm_ref, o_ref,
                  c_ref, n_ref, m_ref, *, heads):
    H = heads
    L = q_ref.shape[0]
    dk = q_ref.shape[1] // H
    dv = v_ref.shape[1] // H

    @pl.when(pl.program_id(0) == 0)
    def _():
        c_ref[...] = jnp.zeros_like(c_ref)
        n_ref[...] = jnp.zeros_like(n_ref)
        m_ref[...] = jnp.zeros_like(m_ref)

    pre = gates_ref[...] + bias_ref[...]
    capped = GATE_CAP * jnp.tanh(pre / GATE_CAP)
    lsig = _log_sigmoid(capped)
    lane = lax.broadcasted_iota(jnp.int32, capped.shape, 1)
    row = lax.broadcasted_iota(jnp.int32, (L, L), 0)
    col = lax.broadcasted_iota(jnp.int32, (L, L), 1)
    eye = row == col
    lower = col <= row

    def gate_terms(h):
        li_col = jnp.sum(jnp.where(lane == h, capped, 0.0), axis=1, keepdims=True)
        lf_col = jnp.sum(jnp.where(lane == h + H, lsig, 0.0), axis=1, keepdims=True)
        lf_row = jnp.sum(jnp.where(eye, lf_col, 0.0), axis=0, keepdims=True)
        li_row = jnp.sum(jnp.where(eye, li_col, 0.0), axis=0, keepdims=True)
        b_col = jnp.sum(jnp.where(lower, lf_row, 0.0), axis=1, keepdims=True)
        b_row = jnp.sum(jnp.where(row <= col, lf_col, 0.0), axis=0, keepdims=True)
        g_row = li_row - b_row
        g_col = li_col - b_col
        m_prev = m_ref[h][:, 0:1]
        a_col = jnp.maximum(m_prev, jnp.max(jnp.where(lower, g_row, -jnp.inf), axis=1, keepdims=True))
        a_last = jnp.max(a_col, axis=0, keepdims=True)
        b_last = jnp.sum(lf_col, axis=0, keepdims=True)
        return dict(dmat=jnp.where(lower, jnp.exp(g_row - a_col), 0.0), g_inter=jnp.exp(m_prev - a_col),
                    floor=jnp.exp(-(b_col + a_col)), w_col=jnp.exp(g_col - a_last),
                    decay=jnp.exp(m_prev - a_last), m_new=b_last + a_last)

    hs = range(H)
    gt = [gate_terms(h) for h in hs]
    q = [q_ref[:, h * dk:(h + 1) * dk] * (dk ** -0.5) for h in hs]
    k = [k_ref[:, h * dk:(h + 1) * dk] for h in hs]
    vb = [v_ref[:, h * dv:(h + 1) * dv].astype(BF16) for h in hs]
    qb = [x.astype(BF16) for x in q]
    cst = [c_ref[h] for h in hs]
    s = _bdot(qb, [x.astype(BF16) for x in k], "nt")
    s = [s[h] * gt[h]["dmat"] for h in hs]
    qc = _bdot(qb, [c.astype(BF16) for c in cst], "nn")
    sv = _bdot([x.astype(BF16) for x in s], vb, "nn")
    kw = [k[h] * gt[h]["w_col"] for h in hs]
    upd = _bdot([x.astype(BF16) for x in kw], vb, "tn")
    for h in hs:
        g = gt[h]
        num = sv[h] + g["g_inter"] * qc[h]
        qn = jnp.sum(q[h] * n_ref[h], axis=1, keepdims=True)
        den = jnp.sum(s[h], axis=1, keepdims=True) + g["g_inter"] * qn
        hh = num / jnp.maximum(jnp.abs(den), g["floor"])
        c_ref[h] = g["decay"] * cst[h] + upd[h]
        n_ref[h] = g["decay"] * n_ref[h] + jnp.sum(kw[h], axis=0, keepdims=True)
        m_ref[h] = jnp.broadcast_to(g["m_new"], m_ref.shape[1:])
        y = hh * lax.rsqrt(jnp.mean(hh * hh, axis=-1, keepdims=True) + NORM_EPS)
        cs = slice(h * dv, (h + 1) * dv)
        o_ref[:, cs] = (y * ng_ref[:, cs] * _sigmoid(og_ref[:, cs])).astype(o_ref.dtype)


def _mlstm(proj, gates, bias_row, norm_g, dk, dv, y):
    T = proj.shape[0]
    H = MLSTM_HEADS
    L = MLSTM_CHUNK
    wk, wv = H * dk, H * dv
    assert (2 * wk) % wv == 0
    v0 = (2 * wk) // wv
    return pl.pallas_call(
        functools.partial(_mlstm_kernel, heads=H),
        grid=(T // L,),
        in_specs=[pl.BlockSpec((L, wk), lambda c: (c, 0)),
                  pl.BlockSpec((L, wk), lambda c: (c, 1)),
                  pl.BlockSpec((L, wv), lambda c: (c, v0)),
                  pl.BlockSpec((L, wv), lambda c: (c, v0 + 1)),
                  pl.BlockSpec((L, LANES), lambda c: (c, 0)),
                  pl.BlockSpec((1, LANES), lambda c: (0, 0)),
                  pl.BlockSpec((1, wv), lambda c: (0, 0)),
                  pl.BlockSpec(memory_space=pl.ANY)],
        out_specs=pl.BlockSpec((L, wv), lambda c: (c, 0)),
        out_shape=jax.ShapeDtypeStruct(y.shape, y.dtype),
        input_output_aliases={7: 0},
        scratch_shapes=[pltpu.VMEM((H, dk, dv), F32), pltpu.VMEM((H, 1, dk), F32), pltpu.VMEM((H, 1, LANES), F32)],
        compiler_params=_cparams(("arbitrary",)),
        name="mlstm",
    )(proj, proj, proj, proj, gates, bias_row, norm_g.reshape(1, wv).astype(F32), y)


def _hgrn_kernel(q_ref, f_ref, i_ref, g_ref, lb_ref, ng_ref, y_hbm_ref, o_ref, st_ref, *, lb_row, chunk):
    TB = q_ref.shape[0]
    dk = HGRN_HEAD_DIM
    GW = st_ref.shape[1]
    NG = q_ref.shape[1] // GW
    HG = GW // dk
    C = chunk

    @pl.when(pl.program_id(1) == 0)
    def _():
        st_ref[...] = jnp.zeros_like(st_ref)

    logits = lb_ref[...].astype(F32)
    e = jnp.exp(logits - jnp.max(logits, axis=0, keepdims=True))
    p = e / jnp.sum(e, axis=0, keepdims=True)
    lb_all = jnp.sum(p[0:lb_row + 1], axis=0, keepdims=True)
    ng_all = ng_ref[...]
    srow = lax.broadcasted_iota(jnp.int32, (C, GW), 0)
    srow_h = srow[:, :dk]
    same = _same_head(GW, dk)

    def elementwise(g, ins):
        qb, fb, vb = ins
        lb = lb_all[:, g * GW:(g + 1) * GW]
        f = lb + (1.0 - lb) * _sigmoid(fb)
        kk = 1.0 - f
        qh = qb * _sigmoid(qb)
        A = jnp.log2(f)
        sh = 1
        while sh < C:
            A = A + jnp.where(srow >= sh, pltpu.roll(A, sh, axis=0), 0.0)
            sh *= 2
        a_last = A[C - 1:C]
        intra = []
        for h in range(HG):
            hs = slice(h * dk, (h + 1) * dk)
            Ah, kh, qhh, vh = A[:, hs], kk[:, hs], qh[:, hs], vb[:, hs]
            rows = []
            for t in range(C):
                rel = jnp.where(srow_h <= t, jnp.exp2(Ah[t:t + 1] - Ah), 0.0)
                sc = jnp.sum(rel * kh * qhh[t:t + 1], axis=1, keepdims=True)
                rows.append(jnp.sum(sc * vh, axis=0, keepdims=True))
            intra.append(jnp.concatenate(rows, axis=0))
        qe = (qh * jnp.exp2(A)).astype(BF16)
        ke = (kk * jnp.exp2(a_last - A)).astype(BF16)
        return qe, ke, vb.astype(BF16), jnp.concatenate(intra, axis=1), jnp.exp2(a_last)

    def body(ci, carry):
        rs = pl.ds(pl.multiple_of(ci * C, C), C)
        G = range(NG)
        gsl = [slice(g * GW, (g + 1) * GW) for g in G]
        st = [st_ref[g] for g in G]
        ew = [elementwise(g, tuple(ref[rs, gsl[g]] for ref in (q_ref, f_ref, i_ref))) for g in G]
        o_inter = _bdot([e[0] for e in ew], [s.astype(BF16) for s in st], "nt")
        upd = _bdot([e[2] for e in ew], [e[1] for e in ew], "tn")
        for g in G:
            st_ref[g] = st[g] * ew[g][4] + (upd[g] if HG == 1 else jnp.where(same, upd[g], 0.0))
            o = o_inter[g] + ew[g][3]
            ys = []
            for h in range(HG):
                oh = o[:, h * dk:(h + 1) * dk]
                ys.append(oh * lax.rsqrt(jnp.mean(oh * oh, axis=-1, keepdims=True) + NORM_EPS))
            gb = g_ref[rs, gsl[g]]
            o_ref[rs, gsl[g]] = (jnp.concatenate(ys, axis=1) * ng_all[:, gsl[g]]
                                 * (gb * _sigmoid(gb))).astype(o_ref.dtype)
        return carry

    lax.fori_loop(0, TB // C, body, 0)


def _hgrn(proj, lb_logits, lb_row, norm_g, y, tb=256):
    T = proj.shape[0]
    W = proj.shape[1] // 4
    sw = _pick(W, STEP_LANES)
    gw = HGRN_HEAD_DIM
    H = W // sw
    tb = _pick(T, tb)
    R = lb_logits.shape[0]
    assert (y.shape[1] - W) % sw == 0
    off = (y.shape[1] - W) // sw
    return pl.pallas_call(
        functools.partial(_hgrn_kernel, lb_row=lb_row, chunk=HGRN_CHUNK),
        grid=(H, T // tb),
        in_specs=[pl.BlockSpec((tb, sw), lambda h, c: (c, h)),
                  pl.BlockSpec((tb, sw), lambda h, c: (c, H + h)),
                  pl.BlockSpec((tb, sw), lambda h, c: (c, 2 * H + h)),
                  pl.BlockSpec((tb, sw), lambda h, c: (c, 3 * H + h)),
                  pl.BlockSpec((R, sw), lambda h, c: (0, h)),
                  pl.BlockSpec((1, sw), lambda h, c: (0, h)),
                  pl.BlockSpec(memory_space=pl.ANY)],
        out_specs=pl.BlockSpec((tb, sw), lambda h, c: (c, off + h)),
        out_shape=jax.ShapeDtypeStruct(y.shape, y.dtype),
        input_output_aliases={6: 0},
        scratch_shapes=[pltpu.VMEM((sw // gw, gw, gw), F32)],
        compiler_params=_cparams(("parallel", "arbitrary")),
        name="hgrn2",
    )(proj, proj, proj, proj, lb_logits.astype(F32), norm_g.reshape(1, W).astype(F32), y)


def _xattn_weights(w_q, w_o, layer, mem_kv):
    D = w_q.shape[1]
    M = mem_kv.shape[0]
    H = XATTN_HEADS
    d = D // H
    tm = _pick(D, 1024)
    wqk = pl.pallas_call(
        functools.partial(_mm_blocks_kernel, trans_b=True),
        grid=(D // tm, H),
        in_specs=[pl.BlockSpec((None, tm, d), lambda i, h: (layer, i, h)),
                  pl.BlockSpec((M, d), lambda i, h: (0, h))],
        out_specs=pl.BlockSpec((tm, M), lambda i, h: (i, h)),
        out_shape=jax.ShapeDtypeStruct((D, H * M), BF16),
        compiler_params=_cparams(("parallel", "parallel")),
        name="xattn_wqk",
    )(w_q, mem_kv)
    tn = _pick(D, 1024)
    vwo = pl.pallas_call(
        functools.partial(_mm_blocks_kernel, trans_b=False),
        grid=(H, D // tn),
        in_specs=[pl.BlockSpec((M, d), lambda h, j: (0, H + h)),
                  pl.BlockSpec((None, d, tn), lambda h, j: (layer, h, j))],
        out_specs=pl.BlockSpec((M, tn), lambda h, j: (h, j)),
        out_shape=jax.ShapeDtypeStruct((H * M, D), BF16),
        compiler_params=_cparams(("parallel", "parallel")),
        name="xattn_vwo",
    )(mem_kv, w_o)
    return wqk, vwo


def _shift_kernel(h_ref, prev_ref, g_ref, mu_ref, *o_refs):
    def norm(h):
        return h * lax.rsqrt(jnp.mean(h * h, axis=-1, keepdims=True) + NORM_EPS) * g_ref[...]

    x = norm(h_ref[...])
    n = prev_ref.shape[0]
    last = jnp.where(pl.program_id(0) == 0, 0.0, norm(prev_ref[...])[n - 1:n, :])
    row = lax.broadcasted_iota(jnp.int32, x.shape, 0)
    xx = jnp.where(row == 0, last, pltpu.roll(x, 1, axis=0)) - x
    for j, o_ref in enumerate(o_refs):
        o_ref[...] = (x + xx * mu_ref[j:j + 1, :]).astype(o_ref.dtype)


def _rwkv_shift(h, g, mu, tm=256):
    T, D = h.shape
    J = mu.shape[0]
    tm = _pick(T, tm)
    pr = 8
    spec = pl.BlockSpec((tm, D), lambda i: (i, 0))
    return pl.pallas_call(
        _shift_kernel,
        grid=(T // tm,),
        in_specs=[spec,
                  pl.BlockSpec((pr, D), lambda i: (jnp.maximum(i * (tm // pr) - 1, 0), 0)),
                  pl.BlockSpec((1, D), lambda i: (0, 0)),
                  pl.BlockSpec((J, D), lambda i: (0, 0))],
        out_specs=[spec] * J,
        out_shape=[jax.ShapeDtypeStruct((T, D), BF16)] * J,
        compiler_params=_cparams(("parallel",)),
        name="rwkv_shift",
    )(h, h, g.reshape(1, D).astype(F32), mu.astype(F32))


def _split2(x):
    x1 = x.astype(BF16)
    return x1, (x - x1.astype(F32)).astype(BF16)


def _dot_exact_rhs(x, m):
    n = x.shape[0]
    d = jnp.dot(jnp.concatenate(_split2(x), axis=0), m, preferred_element_type=F32)
    return d[:n] + d[n:]


def _head_ones(n, head, scale, dtype):
    return jnp.where(_same_head(n, head), scale, 0.0).astype(dtype)


def _rwkv_prep_kernel(r_ref, k_ref, v_ref, lw_ref, la_ref, par_ref,
                      at_ref, rt_ref, bk_ref, tc_ref, rb_ref, akv_ref, rkv_ref, bon_ref, gl_ref, *, chunk):
    TB = r_ref.shape[0]
    L = chunk
    N = RWKV_HEAD
    GW = 4 * L
    HG = GW // N
    NG = r_ref.shape[1] // GW
    assert L == N and GW == MXU_DIM
    par = par_ref[...]
    seg1 = _head_ones(GW, N, 1.0, BF16)
    hmask = _head_masks(L, GW, N)
    lane = lax.broadcasted_iota(jnp.int32, (L, GW), 1)
    rowi = lax.broadcasted_iota(jnp.int32, (L, GW), 0)
    half = GW // 2
    lane_h = lax.broadcasted_iota(jnp.int32, (L, half), 1)
    row_h = lax.broadcasted_iota(jnp.int32, (L, half), 0)
    low = lane_h < N
    src = jnp.where(low, lane_h, lane_h - N)
    strict = src < row_h
    incl = src <= row_h
    eye_half = jnp.where(src == row_h, 1.0, 0.0)
    eye_cat = jnp.concatenate([eye_half, eye_half], axis=1)

    def elementwise(g, ins):
        gs = slice(g * GW, (g + 1) * GW)
        w0, a0, k_k, k_a, r_k = (par[i:i + 1, gs] for i in range(5))
        r, k, v, lw, la = ins
        logw = -(RWKV_DECAY_SCALE * LOG2E) * _sigmoid(w0 + lw)
        a_sig = _sigmoid(a0 + la)
        kkr = k * k_k
        k2 = k * (1.0 + (a_sig - 1.0) * k_a)
        cs = logw
        sh = 1
        while sh < L:
            cs = cs + jnp.where(rowi >= sh, pltpu.roll(cs, sh, axis=0), 0.0)
            sh *= 2
        return r, v, logw, a_sig, kkr, k2, cs, r * k2 * r_k

    def pre_dots(ew, ss, bsum):
        r, v, logw, a_sig, kkr, k2, cs, _ = ew
        kk = kkr / jnp.maximum(jnp.sqrt(ss), 1e-12)
        ginv = jnp.exp2(-cs)
        atb = (-kk * jnp.exp2(cs - logw)).astype(BF16)
        rtb = (r * jnp.exp2(cs)).astype(BF16)
        bt = (kk * a_sig * ginv).astype(BF16)
        kt = (k2 * ginv).astype(BF16)
        zero = jnp.zeros_like(atb)
        lhs = jnp.concatenate([jnp.where(m, x, zero) for m in hmask for x in (atb, rtb)], axis=0)
        return atb, rtb, bt, kt, bsum * v, jnp.exp2(cs[L - 1:L]), lhs, jnp.concatenate([bt, kt, kt, bt], axis=0)

    def cat(parts, even_sel, odd_sel, mask):
        cols = [jnp.where(mask, jnp.where(low, even_sel(parts[2 * c]), odd_sel(parts[2 * c + 1])), 0.0)
                for c in range(HG // 2)]
        return jnp.concatenate(cols, axis=1)

    lo = lambda x: x[:, :half]
    hi = lambda x: x[:, half:]

    def body(ci, carry):
        r0 = pl.multiple_of(ci * L, L)
        sl = pl.ds(r0, L)
        G = range(NG)
        ews = [elementwise(g, tuple(ref[sl, g * GW:(g + 1) * GW]
                                    for ref in (r_ref, k_ref, v_ref, lw_ref, la_ref))) for g in G]
        sums = _dot_exact_rhs(jnp.concatenate([x for ew in ews for x in (ew[4] * ew[4], ew[7])], axis=0), seg1)
        pre = [pre_dots(ews[g], sums[2 * g * L:(2 * g + 1) * L], sums[(2 * g + 1) * L:(2 * g + 2) * L]) for g in G]
        P = _bdot([p[6] for p in pre], [p[7] for p in pre], "nt")
        pat = [[P[g][2 * h * L:(2 * h + 1) * L] for h in range(HG)] for g in G]
        prt = [[P[g][(2 * h + 1) * L:(2 * h + 2) * L] for h in range(HG)] for g in G]
        ncat = [cat(pat[g], lo, hi, strict) for g in G]
        nb = [n.astype(BF16) for n in ncat]
        pw = _bdot(nb, [_stack_heads(n, hmask) for n in nb], "nn")
        tinv = [eye_cat + n for n in ncat]
        power = 2
        while power < L:
            pwb = [x.astype(BF16) for x in pw]
            pst = [_stack_heads(x, hmask) for x in pwb]
            if 2 * power < L:
                d = _bdot([jnp.concatenate([tinv[g].astype(BF16), pwb[g]], axis=0) for g in G], pst, "nn")
                tinv = [tinv[g] + d[g][:L] for g in G]
                pw = [d[g][L:] for g in G]
            else:
                d = _bdot([t.astype(BF16) for t in tinv], pst, "nn")
                tinv = [tinv[g] + d[g] for g in G]
            power *= 2
        akrk = [jnp.concatenate([cat(pat[g], hi, lo, strict), cat(prt[g], hi, lo, incl)], axis=0).astype(BF16)
                for g in G]
        kv = _bdot(akrk, [_stack_heads(ews[g][1].astype(BF16), hmask) for g in G], "nn")
        for g in G:
            gs = slice(g * GW, (g + 1) * GW)
            atb, rtb, bt, kt, bon, gl = pre[g][:6]
            at_ref[sl, gs] = atb
            rt_ref[sl, gs] = rtb
            bk_ref[pl.ds(pl.multiple_of(2 * r0, 2 * L), L), gs] = bt
            bk_ref[pl.ds(pl.multiple_of(2 * r0 + L, L), L), gs] = kt
            tc_ref[sl, gs] = tinv[g].astype(BF16)
            rb_ref[sl, gs] = cat(prt[g], lo, hi, incl).astype(BF16)
            akv_ref[sl, gs] = kv[g][:L]
            rkv_ref[sl, gs] = kv[g][L:]
            bon_ref[sl, gs] = bon
            gl_ref[ci, :, gs] = gl
        return carry

    lax.fori_loop(0, TB // L, body, 0)


def _rwkv_state_kernel(at_ref, rt_ref, bk_ref, tc_ref, rb_ref, akv_ref, rkv_ref, gl_ref, v_ref,
                       y_ref, ht_ref, *, chunk):
    TB = v_ref.shape[0]
    L = chunk
    N = RWKV_HEAD
    GW = ht_ref.shape[1]
    NG = v_ref.shape[1] // GW

    @pl.when(pl.program_id(1) == 0)
    def _():
        ht_ref[...] = jnp.zeros_like(ht_ref)

    hmask = _head_masks(L, GW, N)
    same = _same_head(GW, N)

    def body(ci, carry):
        r0 = pl.multiple_of(ci * L, L)
        sl = pl.ds(r0, L)
        sl2 = pl.ds(pl.multiple_of(2 * r0, 2 * L), 2 * L)
        G = range(NG)
        gsl = [slice(g * GW, (g + 1) * GW) for g in G]
        ht = [ht_ref[g] for g in G]
        xr = _bdot([jnp.concatenate([at_ref[sl, gs], rt_ref[sl, gs]], axis=0) for gs in gsl],
                   [h.astype(BF16) for h in ht], "nt")
        xb = [(xr[g][:L] + akv_ref[sl, gsl[g]]).astype(BF16) for g in G]
        u = _bdot([tc_ref[sl, gs] for gs in gsl], [_stack_heads(x, hmask) for x in xb], "nn")
        ub = [x.astype(BF16) for x in u]
        yd = _bdot([rb_ref[sl, gs] for gs in gsl], [_stack_heads(x, hmask) for x in ub], "nn")
        dht = _bdot([jnp.concatenate([ub[g], v_ref[sl, gsl[g]].astype(BF16)], axis=0) for g in G],
                    [bk_ref[sl2, gs] for gs in gsl], "tn")
        for g in G:
            ht_ref[g] = (ht[g] + jnp.where(same, dht[g], 0.0)) * gl_ref[ci, :, gsl[g]]
            y_ref[sl, gsl[g]] = xr[g][L:] + yd[g] + rkv_ref[sl, gsl[g]]
        return carry

    lax.fori_loop(0, TB // L, body, 0)


def _rwkv_out_kernel(y_ref, bon_ref, g_ref, ln_ref, o_ref):
    N = RWKV_HEAD
    GW = MXU_DIM
    seg_mean = _head_ones(GW, N, 1.0 / N, BF16)
    for g in range(y_ref.shape[1] // GW):
        gs = slice(g * GW, (g + 1) * GW)
        y = y_ref[:, gs]
        yc = y - _dot_exact_rhs(y, seg_mean)
        var = _dot_exact_rhs(yc * yc, seg_mean)
        yn = yc * lax.rsqrt(var + RWKV_LNX_EPS) * ln_ref[0:1, gs] + ln_ref[1:2, gs]
        o_ref[:, gs] = ((yn + bon_ref[:, gs]) * g_ref[:, gs]).astype(o_ref.dtype)


def _rwkv_core(r, k, v, lw, la, g, w0, a0, k_k, k_a, r_k, lnx_w, lnx_b, tb=128):
    T, D = r.shape
    L = RWKV_CHUNK
    gw = MXU_DIM
    W = _pick(D, STEP_LANES)
    assert W % gw == 0
    P = D // W
    tb = _pick(T, tb)
    nc = tb // L
    par = jnp.stack([w0, a0, k_k, k_a, r_k.reshape(D)]).astype(F32)
    par = jnp.concatenate([par, jnp.zeros((3, D), F32)], axis=0)
    spec = pl.BlockSpec((tb, W), lambda p, c: (c, p))
    spec2 = pl.BlockSpec((2 * tb, W), lambda p, c: (c, p))
    gspec = pl.BlockSpec((nc, 1, W), lambda p, c: (c, 0, p))
    bf = jax.ShapeDtypeStruct((T, D), BF16)
    f32 = jax.ShapeDtypeStruct((T, D), F32)
    at, rt, bk, tc, rb, akv, rkv, bon, gl = pl.pallas_call(
        functools.partial(_rwkv_prep_kernel, chunk=L),
        grid=(P, T // tb),
        in_specs=[spec] * 5 + [pl.BlockSpec((8, W), lambda p, c: (0, p))],
        out_specs=[spec, spec, spec2, spec, spec, spec, spec, spec, gspec],
        out_shape=[bf, bf, jax.ShapeDtypeStruct((2 * T, D), BF16), bf, bf, f32, f32, f32,
                   jax.ShapeDtypeStruct((T // L, 1, D), F32)],
        compiler_params=_cparams(("parallel", "parallel")),
        name="rwkv_prep",
    )(r, k, v, lw, la, par)
    y = pl.pallas_call(
        functools.partial(_rwkv_state_kernel, chunk=L),
        grid=(P, T // tb),
        in_specs=[spec, spec, spec2, spec, spec, spec, spec, gspec, spec],
        out_specs=spec,
        out_shape=f32,
        scratch_shapes=[pltpu.VMEM((W // gw, gw, gw), F32)],
        compiler_params=_cparams(("parallel", "arbitrary")),
        name="rwkv_state",
    )(at, rt, bk, tc, rb, akv, rkv, gl, v)
    ln = jnp.stack([lnx_w, lnx_b]).astype(F32)
    ln = jnp.concatenate([ln, jnp.zeros((6, D), F32)], axis=0)
    return pl.pallas_call(
        _rwkv_out_kernel,
        grid=(P, T // tb),
        in_specs=[spec, spec, spec, pl.BlockSpec((8, W), lambda p, c: (0, p))],
        out_specs=spec,
        out_shape=bf,
        compiler_params=_cparams(("parallel", "parallel")),
        name="rwkv_out",
    )(y, bon, g, ln)


def _mixer_even(h, norm_g, w_in, j, b_i, b_f, mlstm_g, lb_logits, hgrn_g, w_out):
    D = h.shape[1]
    H = MLSTM_HEADS
    mix_a = D // 2
    dv = mix_a // H
    dk = dv // 2
    na = 2 * H * dk + 2 * mix_a
    hn = _rmsnorm(h, norm_g, BF16)
    w_t = jnp.swapaxes(w_in, 1, 2)
    nb = w_in.shape[2] - na - 2 * H
    proj_a = _mm(hn, w_t, layer=j, cols=(0, na), b_t=True, name="mm_mlstm_in")
    gates = _mm(hn, w_t, layer=j, cols=(na, LANES), b_t=True, name="mm_gates")
    proj_b = _mm(hn, w_t, layer=j, cols=(na + 2 * H, nb), b_t=True, name="mm_hgrn_in")
    bias_row = jnp.zeros((1, LANES), F32).at[0, :H].set(b_i.astype(F32)).at[0, H:2 * H].set(b_f.astype(F32))
    y = jnp.zeros((h.shape[0], D), BF16)
    y = _mlstm(proj_a, gates, bias_row, mlstm_g, dk, dv, y)
    y = _hgrn(proj_b, lb_logits, j, hgrn_g, y)
    return _mm(y, w_out, layer=j, res=h, name="mm_mix_out")


def _mixer_odd(h, norm_g, j, mu, w0, w1, w2, a0, a1, a2, g1, g2, k_k, k_a, r_k, w_r, w_k, w_v, w_o, lnx_w, lnx_b):
    s_r, s_w, s_k, s_v, s_a, s_g = _rwkv_shift(h, norm_g, mu[j])
    r = _mm(s_r, w_r, layer=j, name="mm_rwkv_r")
    k = _mm(s_k, w_k, layer=j, name="mm_rwkv_k")
    v = _mm(s_v, w_v, layer=j, name="mm_rwkv_v")
    wide = dict(tm=512, tn=h.shape[1])
    lw = _mm(_mm(s_w, w1, layer=j, act="tanh", out_dtype=BF16, name="mm_lora_w1"), w2, layer=j,
             name="mm_lora_w2", **wide)
    la = _mm(_mm(s_a, a1, layer=j, out_dtype=BF16, name="mm_lora_a1"), a2, layer=j, name="mm_lora_a2", **wide)
    g = _mm(_mm(s_g, g1, layer=j, act="sigmoid", out_dtype=BF16, name="mm_lora_g1"), g2, layer=j,
            name="mm_lora_g2", **wide)
    y = _rwkv_core(r, k, v, lw, la, g, w0[j], a0[j], k_k[j], k_a[j], r_k[j], lnx_w[j], lnx_b[j])
    return _mm(y, w_o, layer=j, res=h, name="mm_rwkv_out")


def kernel(x, mem, norm_mix_g, norm_xattn_g, norm_mlp_g, final_norm_g, mem_norm_g, ab_w_in, mlstm_b_i, mlstm_b_f, mlstm_norm_g, hgrn_lb_logits, hgrn_norm_g, ab_w_out, rwkv_mu, rwkv_w0, rwkv_w1, rwkv_w2, rwkv_a0, rwkv_a1, rwkv_a2, rwkv_g1, rwkv_g2, rwkv_k_k, rwkv_k_a, rwkv_r_k, rwkv_w_r, rwkv_w_k, rwkv_w_v, rwkv_w_o, rwkv_lnx_w, rwkv_lnx_b, xattn_w_q, xattn_w_o, mem_w_kv, mlp_w_up, mlp_w_down):
    B, T, D = x.shape
    depth = norm_mix_g.shape[0]
    M = mem.shape[1]
    xscale = (D // XATTN_HEADS) ** -0.5
    outs = []
    for b in range(B):
        mem_kv = _mm(_rmsnorm(mem[b], mem_norm_g, BF16), mem_w_kv, out_dtype=BF16, name="mm_mem_kv")
        h = x[b]
        for layer in range(depth):
            j = layer // 2
            if layer % 2 == 0:
                h = _mixer_even(h, norm_mix_g[layer], ab_w_in, j, mlstm_b_i[j], mlstm_b_f[j], mlstm_norm_g[j],
                                hgrn_lb_logits, hgrn_norm_g[j], ab_w_out)
            else:
                h = _mixer_odd(h, norm_mix_g[layer], j, rwkv_mu, rwkv_w0, rwkv_w1, rwkv_w2, rwkv_a0, rwkv_a1,
                               rwkv_a2, rwkv_g1, rwkv_g2, rwkv_k_k, rwkv_k_a, rwkv_r_k,
                               rwkv_w_r, rwkv_w_k, rwkv_w_v, rwkv_w_o, rwkv_lnx_w, rwkv_lnx_b)
            wqk, vwo = _xattn_weights(xattn_w_q, xattn_w_o, layer, mem_kv)
            p = _mm(_rmsnorm(h, norm_xattn_g[layer], BF16), wqk, act="softmax", seg=(xscale, M),
                    out_dtype=BF16, tm=1024, tn=XATTN_HEADS * M, name="mm_xattn_scores")
            h = _mm(p, vwo, res=h, tn=1024, name="mm_xattn_out")
            up = _mm(_rmsnorm(h, norm_mlp_g[layer], BF16), mlp_w_up, layer=layer,
                     act="relu2", out_dtype=BF16, name="mm_mlp_up")
            for k0 in range(0, up.shape[1], D):
                h = _mm(up, mlp_w_down, layer=layer, res=h, krange=(k0, D), name="mm_mlp_down")
        outs.append(_rmsnorm(h, final_norm_g, F32))
    return jnp.stack(outs)
```

```python
import functools
import math

import jax
import jax.numpy as jnp
from jax import lax
from jax.experimental import pallas as pl
from jax.experimental.pallas import tpu as pltpu

F32 = jnp.float32
BF16 = jnp.bfloat16

NORM_EPS = 1e-6
GATE_CAP = 15.0
MLSTM_HEADS = 4
MLSTM_CHUNK = 64
HGRN_HEAD_DIM = 128
HGRN_CHUNK = 16
RWKV_HEAD = 64
RWKV_CHUNK = 64
RWKV_LNX_EPS = 64e-5
RWKV_DECAY_SCALE = math.exp(-0.5)
LOG2E = math.log2(math.e)
XATTN_HEADS = 4
LANES = 128
MXU_DIM = 256
STEP_LANES = 4096
VMEM_LIMIT = 60 * 1024 * 1024

NN = (((1,), (0,)), ((), ()))
NT = (((1,), (1,)), ((), ()))
TN = (((0,), (0,)), ((), ()))


def _cparams(sem):
    return pltpu.CompilerParams(dimension_semantics=sem, vmem_limit_bytes=VMEM_LIMIT)


def _sigmoid(x):
    return 1.0 / (1.0 + jnp.exp(-x))


def _log_sigmoid(x):
    return jnp.minimum(x, 0.0) - jnp.log1p(jnp.exp(-jnp.abs(x)))


def _pick(dim, pref):
    t = min(dim, pref)
    while dim % t:
        t //= 2
    return t


def _head_masks(rows, width, head):
    lane = lax.broadcasted_iota(jnp.int32, (rows, width), 1) // head
    return [lane == h for h in range(width // head)]


def _stack_heads(x, masks):
    z = jnp.zeros_like(x)
    return jnp.concatenate([jnp.where(m, x, z) for m in masks], axis=0)


_BDOT = {"nn": "gmk,gkn->gmn", "nt": "gmk,gnk->gmn", "tn": "gtm,gtn->gmn"}


def _bdot(xs, ys, mode):
    d = jnp.einsum(_BDOT[mode], jnp.stack(xs), jnp.stack(ys), preferred_element_type=F32)
    return [d[g] for g in range(len(xs))]


def _same_head(n, head):
    r = lax.broadcasted_iota(jnp.int32, (n, n), 0) // head
    c = lax.broadcasted_iota(jnp.int32, (n, n), 1) // head
    return r == c


def _mm_kernel(*refs, nk, act, has_res, b_t):
    if has_res:
        a_ref, b_ref, res_ref, o_ref = refs[:4]
        rest = refs[4:]
    else:
        a_ref, b_ref, o_ref = refs[:3]
        res_ref = None
        rest = refs[3:]

    def finish(r):
        if act == "relu2":
            r = jnp.square(jnp.maximum(r, 0.0))
        elif act == "tanh":
            r = jnp.tanh(r)
        elif act == "sigmoid":
            r = _sigmoid(r)
        if has_res:
            r = r + res_ref[...].astype(F32)
        o_ref[...] = r.astype(o_ref.dtype)

    b = b_ref[0] if b_t else b_ref[...]
    part = lax.dot_general(a_ref[...].astype(BF16), b.astype(BF16), NT if b_t else NN,
                           preferred_element_type=F32)
    if nk == 1:
        finish(part)
    else:
        acc_ref = rest[0]
        k = pl.program_id(2)

        @pl.when(k == 0)
        def _():
            acc_ref[...] = part

        @pl.when(k > 0)
        def _():
            acc_ref[...] += part

        @pl.when(k == nk - 1)
        def _():
            finish(acc_ref[...])


def _mm_tiles(M, N, K, has_res):
    if has_res and K >= 4096 and M % 2048 == 0 and N % MXU_DIM == 0:
        return 2048, MXU_DIM, 4096
    return 1024, 512, 4096


def _mm(a, b, *, out_dtype=F32, act=None, res=None, tm=None, tn=None, tk=None, name="mm",
        layer=None, cols=None, krange=None, b_t=False):
    M, Ka = a.shape
    K2, nb = b.shape[-2:][::-1] if b_t else b.shape[-2:]
    assert Ka == K2 and (b.ndim == 3) == (layer is not None)
    c0, N = cols if cols is not None else (0, nb)
    kf, K = krange if krange is not None else (0, Ka)
    dm, dn, dk = _mm_tiles(M, N, K, res is not None)
    tm, tn, tk = _pick(M, tm or dm), _pick(N, tn or dn), _pick(K, tk or dk)
    assert kf % tk == 0
    k0 = kf // tk
    nk = K // tk
    if b_t:
        assert layer is not None and c0 % 8 == 0
        b_spec = pl.BlockSpec((pl.Element(1), pl.Element(tn), pl.Element(tk)),
                              lambda i, j, k: (layer, pl.multiple_of(c0 + j * tn, 8), (k0 + k) * tk))
    else:
        assert c0 % tn == 0
        j0 = c0 // tn
        if layer is None:
            b_spec = pl.BlockSpec((tk, tn), lambda i, j, k: (k0 + k, j0 + j))
        else:
            b_spec = pl.BlockSpec((None, tk, tn), lambda i, j, k: (layer, k0 + k, j0 + j))
    in_specs = [pl.BlockSpec((tm, tk), lambda i, j, k: (i, k0 + k)), b_spec]
    args = [a, b]
    if res is not None:
        in_specs.append(pl.BlockSpec((tm, tn), lambda i, j, k: (i, j)))
        args.append(res)
    scratch = [] if nk == 1 else [pltpu.VMEM((tm, tn), F32)]
    return pl.pallas_call(
        functools.partial(_mm_kernel, nk=nk, act=act, has_res=res is not None, b_t=b_t),
        grid=(M // tm, N // tn, nk),
        in_specs=in_specs,
        out_specs=pl.BlockSpec((tm, tn), lambda i, j, k: (i, j)),
        out_shape=jax.ShapeDtypeStruct((M, N), out_dtype),
        scratch_shapes=scratch,
        compiler_params=_cparams(("parallel", "parallel", "arbitrary")),
        name=name,
    )(*args)


def _mm_blocks_kernel(a_ref, b_ref, o_ref, *, trans_b):
    a = a_ref[...].astype(BF16)
    b = b_ref[...].astype(BF16)
    if trans_b:
        r = lax.dot_general(a, b, NT, preferred_element_type=F32)
    else:
        r = jnp.dot(a, b, preferred_element_type=F32)
    o_ref[...] = r.astype(o_ref.dtype)


def _rmsnorm_kernel(x_ref, g_ref, o_ref):
    x = x_ref[...].astype(F32)
    y = x * lax.rsqrt(jnp.mean(x * x, axis=-1, keepdims=True) + NORM_EPS)
    o_ref[...] = (y * g_ref[...]).astype(o_ref.dtype)


def _rmsnorm(x, g, out_dtype, tm=512):
    M, D = x.shape
    tm = _pick(M, tm)
    return pl.pallas_call(
        _rmsnorm_kernel,
        grid=(M // tm,),
        in_specs=[pl.BlockSpec((tm, D), lambda i: (i, 0)),
                  pl.BlockSpec((1, D), lambda i: (0, 0))],
        out_specs=pl.BlockSpec((tm, D), lambda i: (i, 0)),
        out_shape=jax.ShapeDtypeStruct((M, D), out_dtype),
        compiler_params=_cparams(("parallel",)),
        name="rmsnorm",
    )(x, g.reshape(1, D).astype(F32))


def _mlstm_kernel(q_ref, k_ref, v_ref, og_ref, gates_ref, bias_ref, ng_ref, y_hbm_ref, o_ref,
                  c_ref, n_ref, m_ref, *, heads):
    H = heads
    L = q_ref.shape[0]
    dk = q_ref.shape[1] // H
    dv = v_ref.shape[1] // H

    @pl.when(pl.program_id(0) == 0)
    def _():
        c_ref[...] = jnp.zeros_like(c_ref)
        n_ref[...] = jnp.zeros_like(n_ref)
        m_ref[...] = jnp.zeros_like(m_ref)

    pre = gates_ref[...] + bias_ref[...]
    capped = GATE_CAP * jnp.tanh(pre / GATE_CAP)
    lsig = _log_sigmoid(capped)
    lane = lax.broadcasted_iota(jnp.int32, capped.shape, 1)
    row = lax.broadcasted_iota(jnp.int32, (L, L), 0)
    col = lax.broadcasted_iota(jnp.int32, (L, L), 1)
    eye = row == col
    lower = col <= row

    def gate_terms(h):
        li_col = jnp.sum(jnp.where(lane == h, capped, 0.0), axis=1, keepdims=True)
        lf_col = jnp.sum(jnp.where(lane == h + H, lsig, 0.0), axis=1, keepdims=True)
        lf_row = jnp.sum(jnp.where(eye, lf_col, 0.0), axis=0, keepdims=True)
        li_row = jnp.sum(jnp.where(eye, li_col, 0.0), axis=0, keepdims=True)
        b_col = jnp.sum(jnp.where(lower, lf_row, 0.0), axis=1, keepdims=True)
        b_row = jnp.sum(jnp.where(row <= col, lf_col, 0.0), axis=0, keepdims=True)
        g_row = li_row - b_row
        g_col = li_col - b_col
        m_prev = m_ref[h][:, 0:1]
        a_col = jnp.maximum(m_prev, jnp.max(jnp.where(lower, g_row, -jnp.inf), axis=1, keepdims=True))
        a_last = jnp.max(a_col, axis=0, keepdims=True)
        b_last = jnp.sum(lf_col, axis=0, keepdims=True)
        return dict(dmat=jnp.where(lower, jnp.exp(g_row - a_col), 0.0), g_inter=jnp.exp(m_prev - a_col),
                    floor=jnp.exp(-(b_col + a_col)), w_col=jnp.exp(g_col - a_last),
                    decay=jnp.exp(m_prev - a_last), m_new=b_last + a_last)

    hs = range(H)
    gt = [gate_terms(h) for h in hs]
    q = [q_ref[:, h * dk:(h + 1) * dk] * (dk ** -0.5) for h in hs]
    k = [k_ref[:, h * dk:(h + 1) * dk] for h in hs]
    vb = [v_ref[:, h * dv:(h + 1) * dv].astype(BF16) for h in hs]
    qb = [x.astype(BF16) for x in q]
    cst = [c_ref[h] for h in hs]
    s = _bdot(qb, [x.astype(BF16) for x in k], "nt")
    s = [s[h] * gt[h]["dmat"] for h in hs]
    qc = _bdot(qb, [c.astype(BF16) for c in cst], "nn")
    sv = _bdot([x.astype(BF16) for x in s], vb, "nn")
    kw = [k[h] * gt[h]["w_col"] for h in hs]
    upd = _bdot([x.astype(BF16) for x in kw], vb, "tn")
    for h in hs:
        g = gt[h]
        num = sv[h] + g["g_inter"] * qc[h]
        qn = jnp.sum(q[h] * n_ref[h], axis=1, keepdims=True)
        den = jnp.sum(s[h], axis=1, keepdims=True) + g["g_inter"] * qn
        hh = num / jnp.maximum(jnp.abs(den), g["floor"])
        c_ref[h] = g["decay"] * cst[h] + upd[h]
        n_ref[h] = g["decay"] * n_ref[h] + jnp.sum(kw[h], axis=0, keepdims=True)
        m_ref[h] = jnp.broadcast_to(g["m_new"], m_ref.shape[1:])
        y = hh * lax.rsqrt(jnp.mean(hh * hh, axis=-1, keepdims=True) + NORM_EPS)
        cs = slice(h * dv, (h + 1) * dv)
        o_ref[:, cs] = (y * ng_ref[:, cs] * _sigmoid(og_ref[:, cs])).astype(o_ref.dtype)


def _mlstm(proj, gates, bias_row, norm_g, dk, dv, y):
    T = proj.shape[0]
    H = MLSTM_HEADS
    L = MLSTM_CHUNK
    wk, wv = H * dk, H * dv
    assert (2 * wk) % wv == 0
    v0 = (2 * wk) // wv
    return pl.pallas_call(
        functools.partial(_mlstm_kernel, heads=H),
        grid=(T // L,),
        in_specs=[pl.BlockSpec((L, wk), lambda c: (c, 0)),
                  pl.BlockSpec((L, wk), lambda c: (c, 1)),
                  pl.BlockSpec((L, wv), lambda c: (c, v0)),
                  pl.BlockSpec((L, wv), lambda c: (c, v0 + 1)),
                  pl.BlockSpec((L, LANES), lambda c: (c, 0)),
                  pl.BlockSpec((1, LANES), lambda c: (0, 0)),
                  pl.BlockSpec((1, wv), lambda c: (0, 0)),
                  pl.BlockSpec(memory_space=pl.ANY)],
        out_specs=pl.BlockSpec((L, wv), lambda c: (c, 0)),
        out_shape=jax.ShapeDtypeStruct(y.shape, y.dtype),
        input_output_aliases={7: 0},
        scratch_shapes=[pltpu.VMEM((H, dk, dv), F32), pltpu.VMEM((H, 1, dk), F32), pltpu.VMEM((H, 1, LANES), F32)],
        compiler_params=_cparams(("arbitrary",)),
        name="mlstm",
    )(proj, proj, proj, proj, gates, bias_row, norm_g.reshape(1, wv).astype(F32), y)


def _hgrn_kernel(q_ref, f_ref, i_ref, g_ref, lb_ref, ng_ref, y_hbm_ref, o_ref, st_ref, *, lb_row, chunk):
    TB = q_ref.shape[0]
    dk = HGRN_HEAD_DIM
    GW = st_ref.shape[1]
    NG = q_ref.shape[1] // GW
    HG = GW // dk
    C = chunk

    @pl.when(pl.program_id(1) == 0)
    def _():
        st_ref[...] = jnp.zeros_like(st_ref)

    logits = lb_ref[...].astype(F32)
    e = jnp.exp(logits - jnp.max(logits, axis=0, keepdims=True))
    p = e / jnp.sum(e, axis=0, keepdims=True)
    lb_all = jnp.sum(p[0:lb_row + 1], axis=0, keepdims=True)
    ng_all = ng_ref[...]
    srow = lax.broadcasted_iota(jnp.int32, (C, GW), 0)
    srow_h = srow[:, :dk]
    same = _same_head(GW, dk)

    def elementwise(g, ins):
        qb, fb, vb = ins
        lb = lb_all[:, g * GW:(g + 1) * GW]
        f = lb + (1.0 - lb) * _sigmoid(fb)
        kk = 1.0 - f
        qh = qb * _sigmoid(qb)
        A = jnp.log2(f)
        sh = 1
        while sh < C:
            A = A + jnp.where(srow >= sh, pltpu.roll(A, sh, axis=0), 0.0)
            sh *= 2
        a_last = A[C - 1:C]
        intra = []
        for h in range(HG):
            hs = slice(h * dk, (h + 1) * dk)
            Ah, kh, qhh, vh = A[:, hs], kk[:, hs], qh[:, hs], vb[:, hs]
            rows = []
            for t in range(C):
                rel = jnp.where(srow_h <= t, jnp.exp2(Ah[t:t + 1] - Ah), 0.0)
                sc = jnp.sum(rel * kh * qhh[t:t + 1], axis=1, keepdims=True)
                rows.append(jnp.sum(sc * vh, axis=0, keepdims=True))
            intra.append(jnp.concatenate(rows, axis=0))
        qe = (qh * jnp.exp2(A)).astype(BF16)
        ke = (kk * jnp.exp2(a_last - A)).astype(BF16)
        return qe, ke, vb.astype(BF16), jnp.concatenate(intra, axis=1), jnp.exp2(a_last)

    def body(ci, carry):
        rs = pl.ds(pl.multiple_of(ci * C, C), C)
        G = range(NG)
        gsl = [slice(g * GW, (g + 1) * GW) for g in G]
        st = [st_ref[g] for g in G]
        ew = [elementwise(g, tuple(ref[rs, gsl[g]] for ref in (q_ref, f_ref, i_ref))) for g in G]
        o_inter = _bdot([e[0] for e in ew], [s.astype(BF16) for s in st], "nt")
        upd = _bdot([e[2] for e in ew], [e[1] for e in ew], "tn")
        for g in G:
            st_ref[g] = st[g] * ew[g][4] + (upd[g] if HG == 1 else jnp.where(same, upd[g], 0.0))
            o = o_inter[g] + ew[g][3]
            ys = []
            for h in range(HG):
                oh = o[:, h * dk:(h + 1) * dk]
                ys.append(oh * lax.rsqrt(jnp.mean(oh * oh, axis=-1, keepdims=True) + NORM_EPS))
            gb = g_ref[rs, gsl[g]]
            o_ref[rs, gsl[g]] = (jnp.concatenate(ys, axis=1) * ng_all[:, gsl[g]]
                                 * (gb * _sigmoid(gb))).astype(o_ref.dtype)
        return carry

    lax.fori_loop(0, TB // C, body, 0)


def _hgrn(proj, lb_logits, lb_row, norm_g, y, tb=256):
    T = proj.shape[0]
    W = proj.shape[1] // 4
    sw = _pick(W, STEP_LANES)
    gw = HGRN_HEAD_DIM
    H = W // sw
    tb = _pick(T, tb)
    R = lb_logits.shape[0]
    assert (y.shape[1] - W) % sw == 0
    off = (y.shape[1] - W) // sw
    return pl.pallas_call(
        functools.partial(_hgrn_kernel, lb_row=lb_row, chunk=HGRN_CHUNK),
        grid=(H, T // tb),
        in_specs=[pl.BlockSpec((tb, sw), lambda h, c: (c, h)),
                  pl.BlockSpec((tb, sw), lambda h, c: (c, H + h)),
                  pl.BlockSpec((tb, sw), lambda h, c: (c, 2 * H + h)),
                  pl.BlockSpec((tb, sw), lambda h, c: (c, 3 * H + h)),
                  pl.BlockSpec((R, sw), lambda h, c: (0, h)),
                  pl.BlockSpec((1, sw), lambda h, c: (0, h)),
                  pl.BlockSpec(memory_space=pl.ANY)],
        out_specs=pl.BlockSpec((tb, sw), lambda h, c: (c, off + h)),
        out_shape=jax.ShapeDtypeStruct(y.shape, y.dtype),
        input_output_aliases={6: 0},
        scratch_shapes=[pltpu.VMEM((sw // gw, gw, gw), F32)],
        compiler_params=_cparams(("parallel", "arbitrary")),
        name="hgrn2",
    )(proj, proj, proj, proj, lb_logits.astype(F32), norm_g.reshape(1, W).astype(F32), y)


def _xattn_scores_kernel(h_ref, g_ref, w_ref, o_ref, *, scale, width):
    x = h_ref[...]
    xn = x * lax.rsqrt(jnp.mean(x * x, axis=-1, keepdims=True) + NORM_EPS) * g_ref[...]
    r = jnp.dot(xn.astype(BF16), w_ref[...], preferred_element_type=F32)
    for j in range(r.shape[1] // width):
        cs = slice(j * width, (j + 1) * width)
        s = r[:, cs] * scale
        e = jnp.exp(s - jnp.max(s, axis=-1, keepdims=True))
        o_ref[:, cs] = (e / jnp.sum(e, axis=-1, keepdims=True)).astype(o_ref.dtype)


def _xattn_scores(h, g, wqk, scale, width, tm=512):
    T, D = h.shape
    N = wqk.shape[1]
    tm = _pick(T, tm)
    return pl.pallas_call(
        functools.partial(_xattn_scores_kernel, scale=scale, width=width),
        grid=(T // tm,),
        in_specs=[pl.BlockSpec((tm, D), lambda i: (i, 0)),
                  pl.BlockSpec((1, D), lambda i: (0, 0)),
                  pl.BlockSpec((D, N), lambda i: (0, 0))],
        out_specs=pl.BlockSpec((tm, N), lambda i: (i, 0)),
        out_shape=jax.ShapeDtypeStruct((T, N), BF16),
        compiler_params=_cparams(("parallel",)),
        name="xattn_scores",
    )(h, g.reshape(1, D).astype(F32), wqk)


def _xattn_weights(w_q, w_o, layer, mem_kv):
    D = w_q.shape[1]
    M = mem_kv.shape[0]
    H = XATTN_HEADS
    d = D // H
    tm = _pick(D, 1024)
    wqk = pl.pallas_call(
        functools.partial(_mm_blocks_kernel, trans_b=True),
        grid=(D // tm, H),
        in_specs=[pl.BlockSpec((None, tm, d), lambda i, h: (layer, i, h)),
                  pl.BlockSpec((M, d), lambda i, h: (0, h))],
        out_specs=pl.BlockSpec((tm, M), lambda i, h: (i, h)),
        out_shape=jax.ShapeDtypeStruct((D, H * M), BF16),
        compiler_params=_cparams(("parallel", "parallel")),
        name="xattn_wqk",
    )(w_q, mem_kv)
    tn = _pick(D, 1024)
    vwo = pl.pallas_call(
        functools.partial(_mm_blocks_kernel, trans_b=False),
        grid=(H, D // tn),
        in_specs=[pl.BlockSpec((M, d), lambda h, j: (0, H + h)),
                  pl.BlockSpec((None, d, tn), lambda h, j: (layer, h, j))],
        out_specs=pl.BlockSpec((M, tn), lambda h, j: (h, j)),
        out_shape=jax.ShapeDtypeStruct((H * M, D), BF16),
        compiler_params=_cparams(("parallel", "parallel")),
        name="xattn_vwo",
    )(mem_kv, w_o)
    return wqk, vwo


def _shift_kernel(h_ref, prev_ref, g_ref, mu_ref, *o_refs):
    def norm(h):
        return h * lax.rsqrt(jnp.mean(h * h, axis=-1, keepdims=True) + NORM_EPS) * g_ref[...]

    x = norm(h_ref[...])
    n = prev_ref.shape[0]
    last = jnp.where(pl.program_id(0) == 0, 0.0, norm(prev_ref[...])[n - 1:n, :])
    row = lax.broadcasted_iota(jnp.int32, x.shape, 0)
    xx = jnp.where(row == 0, last, pltpu.roll(x, 1, axis=0)) - x
    for j, o_ref in enumerate(o_refs):
        o_ref[...] = (x + xx * mu_ref[j:j + 1, :]).astype(o_ref.dtype)


def _rwkv_shift(h, g, mu, tm=256):
    T, D = h.shape
    J = mu.shape[0]
    tm = _pick(T, tm)
    pr = 8
    spec = pl.BlockSpec((tm, D), lambda i: (i, 0))
    return pl.pallas_call(
        _shift_kernel,
        grid=(T // tm,),
        in_specs=[spec,
                  pl.BlockSpec((pr, D), lambda i: (jnp.maximum(i * (tm // pr) - 1, 0), 0)),
                  pl.BlockSpec((1, D), lambda i: (0, 0)),
                  pl.BlockSpec((J, D), lambda i: (0, 0))],
        out_specs=[spec] * J,
        out_shape=[jax.ShapeDtypeStruct((T, D), BF16)] * J,
        compiler_params=_cparams(("parallel",)),
        name="rwkv_shift",
    )(h, h, g.reshape(1, D).astype(F32), mu.astype(F32))


def _split2(x):
    x1 = x.astype(BF16)
    return x1, (x - x1.astype(F32)).astype(BF16)


def _dot_exact_rhs(x, m):
    n = x.shape[0]
    d = jnp.dot(jnp.concatenate(_split2(x), axis=0), m, preferred_element_type=F32)
    return d[:n] + d[n:]


def _head_ones(n, head, scale, dtype):
    return jnp.where(_same_head(n, head), scale, 0.0).astype(dtype)


def _rwkv_prep_kernel(r_ref, k_ref, v_ref, lw_ref, la_ref, par_ref,
                      at_ref, rt_ref, bk_ref, tc_ref, rb_ref, akv_ref, rkv_ref, bon_ref, gl_ref, *, chunk):
    TB = r_ref.shape[0]
    L = chunk
    N = RWKV_HEAD
    GW = 4 * L
    HG = GW // N
    NG = r_ref.shape[1] // GW
    assert L == N and GW == MXU_DIM
    par = par_ref[...]
    seg1 = _head_ones(GW, N, 1.0, BF16)
    hmask = _head_masks(L, GW, N)
    lane = lax.broadcasted_iota(jnp.int32, (L, GW), 1)
    rowi = lax.broadcasted_iota(jnp.int32, (L, GW), 0)
    half = GW // 2
    lane_h = lax.broadcasted_iota(jnp.int32, (L, half), 1)
    row_h = lax.broadcasted_iota(jnp.int32, (L, half), 0)
    low = lane_h < N
    src = jnp.where(low, lane_h, lane_h - N)
    strict = src < row_h
    incl = src <= row_h
    eye_half = jnp.where(src == row_h, 1.0, 0.0)
    eye_cat = jnp.concatenate([eye_half, eye_half], axis=1)

    def elementwise(g, ins):
        gs = slice(g * GW, (g + 1) * GW)
        w0, a0, k_k, k_a, r_k = (par[i:i + 1, gs] for i in range(5))
        r, k, v, lw, la = ins
        logw = -(RWKV_DECAY_SCALE * LOG2E) * _sigmoid(w0 + lw)
        a_sig = _sigmoid(a0 + la)
        kkr = k * k_k
        k2 = k * (1.0 + (a_sig - 1.0) * k_a)
        cs = logw
        sh = 1
        while sh < L:
            cs = cs + jnp.where(rowi >= sh, pltpu.roll(cs, sh, axis=0), 0.0)
            sh *= 2
        return r, v, logw, a_sig, kkr, k2, cs, r * k2 * r_k

    def pre_dots(ew, ss, bsum):
        r, v, logw, a_sig, kkr, k2, cs, _ = ew
        kk = kkr / jnp.maximum(jnp.sqrt(ss), 1e-12)
        ginv = jnp.exp2(-cs)
        atb = (-kk * jnp.exp2(cs - logw)).astype(BF16)
        rtb = (r * jnp.exp2(cs)).astype(BF16)
        bt = (kk * a_sig * ginv).astype(BF16)
        kt = (k2 * ginv).astype(BF16)
        zero = jnp.zeros_like(atb)
        lhs = jnp.concatenate([jnp.where(m, x, zero) for m in hmask for x in (atb, rtb)], axis=0)
        return atb, rtb, bt, kt, bsum * v, jnp.exp2(cs[L - 1:L]), lhs, jnp.concatenate([bt, kt, kt, bt], axis=0)

    def cat(parts, even_sel, odd_sel, mask):
        cols = [jnp.where(mask, jnp.where(low, even_sel(parts[2 * c]), odd_sel(parts[2 * c + 1])), 0.0)
                for c in range(HG // 2)]
        return jnp.concatenate(cols, axis=1)

    lo = lambda x: x[:, :half]
    hi = lambda x: x[:, half:]

    def body(ci, carry):
        r0 = pl.multiple_of(ci * L, L)
        sl = pl.ds(r0, L)
        G = range(NG)
        ews = [elementwise(g, tuple(ref[sl, g * GW:(g + 1) * GW]
                                    for ref in (r_ref, k_ref, v_ref, lw_ref, la_ref))) for g in G]
        sums = _dot_exact_rhs(jnp.concatenate([x for ew in ews for x in (ew[4] * ew[4], ew[7])], axis=0), seg1)
        pre = [pre_dots(ews[g], sums[2 * g * L:(2 * g + 1) * L], sums[(2 * g + 1) * L:(2 * g + 2) * L]) for g in G]
        P = _bdot([p[6] for p in pre], [p[7] for p in pre], "nt")
        pat = [[P[g][2 * h * L:(2 * h + 1) * L] for h in range(HG)] for g in G]
        prt = [[P[g][(2 * h + 1) * L:(2 * h + 2) * L] for h in range(HG)] for g in G]
        ncat = [cat(pat[g], lo, hi, strict) for g in G]
        nb = [n.astype(BF16) for n in ncat]
        pw = _bdot(nb, [_stack_heads(n, hmask) for n in nb], "nn")
        tinv = [eye_cat + n for n in ncat]
        power = 2
        while power < L:
            pwb = [x.astype(BF16) for x in pw]
            pst = [_stack_heads(x, hmask) for x in pwb]
            if 2 * power < L:
                d = _bdot([jnp.concatenate([tinv[g].astype(BF16), pwb[g]], axis=0) for g in G], pst, "nn")
                tinv = [tinv[g] + d[g][:L] for g in G]
                pw = [d[g][L:] for g in G]
            else:
                d = _bdot([t.astype(BF16) for t in tinv], pst, "nn")
                tinv = [tinv[g] + d[g] for g in G]
            power *= 2
        akrk = [jnp.concatenate([cat(pat[g], hi, lo, strict), cat(prt[g], hi, lo, incl)], axis=0).astype(BF16)
                for g in G]
        kv = _bdot(akrk, [_stack_heads(ews[g][1].astype(BF16), hmask) for g in G], "nn")
        for g in G:
            gs = slice(g * GW, (g + 1) * GW)
            atb, rtb, bt, kt, bon, gl = pre[g][:6]
            at_ref[sl, gs] = atb
            rt_ref[sl, gs] = rtb
            bk_ref[pl.ds(pl.multiple_of(2 * r0, 2 * L), L), gs] = bt
            bk_ref[pl.ds(pl.multiple_of(2 * r0 + L, L), L), gs] = kt
            tc_ref[sl, gs] = tinv[g].astype(BF16)
            rb_ref[sl, gs] = cat(prt[g], lo, hi, incl).astype(BF16)
            akv_ref[sl, gs] = kv[g][:L]
            rkv_ref[sl, gs] = kv[g][L:]
            bon_ref[sl, gs] = bon
            gl_ref[ci, :, gs] = gl
        return carry

    lax.fori_loop(0, TB // L, body, 0)


def _rwkv_state_kernel(at_ref, rt_ref, bk_ref, tc_ref, rb_ref, akv_ref, rkv_ref, gl_ref, v_ref,
                       y_ref, ht_ref, *, chunk):
    TB = v_ref.shape[0]
    L = chunk
    N = RWKV_HEAD
    GW = ht_ref.shape[1]
    NG = v_ref.shape[1] // GW

    @pl.when(pl.program_id(1) == 0)
    def _():
        ht_ref[...] = jnp.zeros_like(ht_ref)

    hmask = _head_masks(L, GW, N)
    same = _same_head(GW, N)

    def body(ci, carry):
        r0 = pl.multiple_of(ci * L, L)
        sl = pl.ds(r0, L)
        sl2 = pl.ds(pl.multiple_of(2 * r0, 2 * L), 2 * L)
        G = range(NG)
        gsl = [slice(g * GW, (g + 1) * GW) for g in G]
        ht = [ht_ref[g] for g in G]
        xr = _bdot([jnp.concatenate([at_ref[sl, gs], rt_ref[sl, gs]], axis=0) for gs in gsl],
                   [h.astype(BF16) for h in ht], "nt")
        xb = [(xr[g][:L] + akv_ref[sl, gsl[g]]).astype(BF16) for g in G]
        u = _bdot([tc_ref[sl, gs] for gs in gsl], [_stack_heads(x, hmask) for x in xb], "nn")
        ub = [x.astype(BF16) for x in u]
        yd = _bdot([rb_ref[sl, gs] for gs in gsl], [_stack_heads(x, hmask) for x in ub], "nn")
        dht = _bdot([jnp.concatenate([ub[g], v_ref[sl, gsl[g]].astype(BF16)], axis=0) for g in G],
                    [bk_ref[sl2, gs] for gs in gsl], "tn")
        for g in G:
            ht_ref[g] = (ht[g] + jnp.where(same, dht[g], 0.0)) * gl_ref[ci, :, gsl[g]]
            y_ref[sl, gsl[g]] = xr[g][L:] + yd[g] + rkv_ref[sl, gsl[g]]
        return carry

    lax.fori_loop(0, TB // L, body, 0)


def _rwkv_out_kernel(y_ref, bon_ref, g_ref, ln_ref, o_ref):
    N = RWKV_HEAD
    GW = MXU_DIM
    seg_mean = _head_ones(GW, N, 1.0 / N, BF16)
    for g in range(y_ref.shape[1] // GW):
        gs = slice(g * GW, (g + 1) * GW)
        y = y_ref[:, gs]
        yc = y - _dot_exact_rhs(y, seg_mean)
        var = _dot_exact_rhs(yc * yc, seg_mean)
        yn = yc * lax.rsqrt(var + RWKV_LNX_EPS) * ln_ref[0:1, gs] + ln_ref[1:2, gs]
        o_ref[:, gs] = ((yn + bon_ref[:, gs]) * g_ref[:, gs]).astype(o_ref.dtype)


def _rwkv_core(r, k, v, lw, la, g, w0, a0, k_k, k_a, r_k, lnx_w, lnx_b, tb=RWKV_CHUNK):
    T, D = r.shape
    L = RWKV_CHUNK
    gw = MXU_DIM
    W = _pick(D, STEP_LANES)
    assert W % gw == 0
    P = D // W
    tb = _pick(T, tb)
    nc = tb // L
    par = jnp.stack([w0, a0, k_k, k_a, r_k.reshape(D)]).astype(F32)
    par = jnp.concatenate([par, jnp.zeros((3, D), F32)], axis=0)
    spec = pl.BlockSpec((tb, W), lambda p, c: (c, p))
    spec2 = pl.BlockSpec((2 * tb, W), lambda p, c: (c, p))
    gspec = pl.BlockSpec((nc, 1, W), lambda p, c: (c, 0, p))
    bf = jax.ShapeDtypeStruct((T, D), BF16)
    f32 = jax.ShapeDtypeStruct((T, D), F32)
    at, rt, bk, tc, rb, akv, rkv, bon, gl = pl.pallas_call(
        functools.partial(_rwkv_prep_kernel, chunk=L),
        grid=(P, T // tb),
        in_specs=[spec] * 5 + [pl.BlockSpec((8, W), lambda p, c: (0, p))],
        out_specs=[spec, spec, spec2, spec, spec, spec, spec, spec, gspec],
        out_shape=[bf, bf, jax.ShapeDtypeStruct((2 * T, D), BF16), bf, bf, f32, f32, f32,
                   jax.ShapeDtypeStruct((T // L, 1, D), F32)],
        compiler_params=_cparams(("parallel", "parallel")),
        name="rwkv_prep",
    )(r, k, v, lw, la, par)
    y = pl.pallas_call(
        functools.partial(_rwkv_state_kernel, chunk=L),
        grid=(P, T // tb),
        in_specs=[spec, spec, spec2, spec, spec, spec, spec, gspec, spec],
        out_specs=spec,
        out_shape=f32,
        scratch_shapes=[pltpu.VMEM((W // gw, gw, gw), F32)],
        compiler_params=_cparams(("parallel", "arbitrary")),
        name="rwkv_state",
    )(at, rt, bk, tc, rb, akv, rkv, gl, v)
    ln = jnp.stack([lnx_w, lnx_b]).astype(F32)
    ln = jnp.concatenate([ln, jnp.zeros((6, D), F32)], axis=0)
    return pl.pallas_call(
        _rwkv_out_kernel,
        grid=(P, T // tb),
        in_specs=[spec, spec, spec, pl.BlockSpec((8, W), lambda p, c: (0, p))],
        out_specs=spec,
        out_shape=bf,
        compiler_params=_cparams(("parallel", "parallel")),
        name="rwkv_out",
    )(y, bon, g, ln)


def _mixer_even(h, norm_g, w_in, j, b_i, b_f, mlstm_g, lb_logits, hgrn_g, w_out):
    D = h.shape[1]
    H = MLSTM_HEADS
    mix_a = D // 2
    dv = mix_a // H
    dk = dv // 2
    na = 2 * H * dk + 2 * mix_a
    hn = _rmsnorm(h, norm_g, BF16)
    w_t = jnp.swapaxes(w_in, 1, 2)
    nb = w_in.shape[2] - na - 2 * H
    proj_a = _mm(hn, w_t, layer=j, cols=(0, na), b_t=True, name="mm_mlstm_in")
    gates = _mm(hn, w_t, layer=j, cols=(na, LANES), b_t=True, name="mm_gates")
    proj_b = _mm(hn, w_t, layer=j, cols=(na + 2 * H, nb), b_t=True, name="mm_hgrn_in")
    bias_row = jnp.zeros((1, LANES), F32).at[0, :H].set(b_i.astype(F32)).at[0, H:2 * H].set(b_f.astype(F32))
    y = jnp.zeros((h.shape[0], D), BF16)
    y = _mlstm(proj_a, gates, bias_row, mlstm_g, dk, dv, y)
    y = _hgrn(proj_b, lb_logits, j, hgrn_g, y)
    return _mm(y, w_out, layer=j, res=h, name="mm_mix_out")


def _mixer_odd(h, norm_g, j, mu, w0, w1, w2, a0, a1, a2, g1, g2, k_k, k_a, r_k, w_r, w_k, w_v, w_o, lnx_w, lnx_b):
    s_r, s_w, s_k, s_v, s_a, s_g = _rwkv_shift(h, norm_g, mu[j])
    r = _mm(s_r, w_r, layer=j, name="mm_rwkv_r")
    k = _mm(s_k, w_k, layer=j, name="mm_rwkv_k")
    v = _mm(s_v, w_v, layer=j, name="mm_rwkv_v")
    wide = dict(tm=512, tn=h.shape[1])
    lw = _mm(_mm(s_w, w1, layer=j, act="tanh", out_dtype=BF16, name="mm_lora_w1"), w2, layer=j,
             name="mm_lora_w2", **wide)
    la = _mm(_mm(s_a, a1, layer=j, out_dtype=BF16, name="mm_lora_a1"), a2, layer=j, name="mm_lora_a2", **wide)
    g = _mm(_mm(s_g, g1, layer=j, act="sigmoid", out_dtype=BF16, name="mm_lora_g1"), g2, layer=j,
            name="mm_lora_g2", **wide)
    y = _rwkv_core(r, k, v, lw, la, g, w0[j], a0[j], k_k[j], k_a[j], r_k[j], lnx_w[j], lnx_b[j])
    return _mm(y, w_o, layer=j, res=h, name="mm_rwkv_out")


def kernel(x, mem, norm_mix_g, norm_xattn_g, norm_mlp_g, final_norm_g, mem_norm_g, ab_w_in, mlstm_b_i, mlstm_b_f, mlstm_norm_g, hgrn_lb_logits, hgrn_norm_g, ab_w_out, rwkv_mu, rwkv_w0, rwkv_w1, rwkv_w2, rwkv_a0, rwkv_a1, rwkv_a2, rwkv_g1, rwkv_g2, rwkv_k_k, rwkv_k_a, rwkv_r_k, rwkv_w_r, rwkv_w_k, rwkv_w_v, rwkv_w_o, rwkv_lnx_w, rwkv_lnx_b, xattn_w_q, xattn_w_o, mem_w_kv, mlp_w_up, mlp_w_down):
    B, T, D = x.shape
    depth = norm_mix_g.shape[0]
    M = mem.shape[1]
    xscale = (D // XATTN_HEADS) ** -0.5
    outs = []
    for b in range(B):
        mem_kv = _mm(_rmsnorm(mem[b], mem_norm_g, BF16), mem_w_kv, out_dtype=BF16, name="mm_mem_kv")
        h = x[b]
        for layer in range(depth):
            j = layer // 2
            if layer % 2 == 0:
                h = _mixer_even(h, norm_mix_g[layer], ab_w_in, j, mlstm_b_i[j], mlstm_b_f[j], mlstm_norm_g[j],
                                hgrn_lb_logits, hgrn_norm_g[j], ab_w_out)
            else:
                h = _mixer_odd(h, norm_mix_g[layer], j, rwkv_mu, rwkv_w0, rwkv_w1, rwkv_w2, rwkv_a0, rwkv_a1,
                               rwkv_a2, rwkv_g1, rwkv_g2, rwkv_k_k, rwkv_k_a, rwkv_r_k,
                               rwkv_w_r, rwkv_w_k, rwkv_w_v, rwkv_w_o, rwkv_lnx_w, rwkv_lnx_b)
            wqk, vwo = _xattn_weights(xattn_w_q, xattn_w_o, layer, mem_kv)
            p = _xattn_scores(h, norm_xattn_g[layer], wqk, xscale, M)
            h = _mm(p, vwo, res=h, tn=1024, name="mm_xattn_out")
            up = _mm(_rmsnorm(h, norm_mlp_g[layer], BF16), mlp_w_up, layer=layer,
                     act="relu2", out_dtype=BF16, name="mm_mlp_up")
            for k0 in range(0, up.shape[1], D):
                h = _mm(up, mlp_w_down, layer=layer, res=h, krange=(k0, D), name="mm_mlp_down")
        outs.append(_rmsnorm(h, final_norm_g, F32))
    return jnp.stack(outs)
```

```python
import functools
import math

import jax
import jax.numpy as jnp
from jax import lax
from jax.experimental import pallas as pl
from jax.experimental.pallas import tpu as pltpu

F32 = jnp.float32
BF16 = jnp.bfloat16

NORM_EPS = 1e-6
GATE_CAP = 15.0
MLSTM_HEADS = 4
MLSTM_CHUNK = 64
HGRN_HEAD_DIM = 128
HGRN_CHUNK = 16
RWKV_HEAD = 64
RWKV_CHUNK = 64
RWKV_LNX_EPS = 64e-5
RWKV_DECAY_SCALE = math.exp(-0.5)
LOG2E = math.log2(math.e)
XATTN_HEADS = 4
LANES = 128
MXU_DIM = 256
STEP_LANES = 4096
VMEM_LIMIT = 60 * 1024 * 1024

NN = (((1,), (0,)), ((), ()))
NT = (((1,), (1,)), ((), ()))
TN = (((0,), (0,)), ((), ()))


def _cparams(sem):
    return pltpu.CompilerParams(dimension_semantics=sem, vmem_limit_bytes=VMEM_LIMIT)


def _sigmoid(x):
    return 1.0 / (1.0 + jnp.exp(-x))


def _log_sigmoid(x):
    return jnp.minimum(x, 0.0) - jnp.log1p(jnp.exp(-jnp.abs(x)))


def _pick(dim, pref):
    t = min(dim, pref)
    while dim % t:
        t //= 2
    return t


def _head_masks(rows, width, head):
    lane = lax.broadcasted_iota(jnp.int32, (rows, width), 1) // head
    return [lane == h for h in range(width // head)]


def _stack_heads(x, masks):
    z = jnp.zeros_like(x)
    return jnp.concatenate([jnp.where(m, x, z) for m in masks], axis=0)


_BDOT = {"nn": "gmk,gkn->gmn", "nt": "gmk,gnk->gmn", "tn": "gtm,gtn->gmn"}


def _bdot(xs, ys, mode):
    d = jnp.einsum(_BDOT[mode], jnp.stack(xs), jnp.stack(ys), preferred_element_type=F32)
    return [d[g] for g in range(len(xs))]


def _same_head(n, head):
    r = lax.broadcasted_iota(jnp.int32, (n, n), 0) // head
    c = lax.broadcasted_iota(jnp.int32, (n, n), 1) // head
    return r == c


def _mm_kernel(*refs, nk, act, has_res, b_t):
    if has_res:
        a_ref, b_ref, res_ref, o_ref = refs[:4]
        rest = refs[4:]
    else:
        a_ref, b_ref, o_ref = refs[:3]
        res_ref = None
        rest = refs[3:]

    def finish(r):
        if act == "relu2":
            r = jnp.square(jnp.maximum(r, 0.0))
        elif act == "tanh":
            r = jnp.tanh(r)
        elif act == "sigmoid":
            r = _sigmoid(r)
        if has_res:
            r = r + res_ref[...].astype(F32)
        o_ref[...] = r.astype(o_ref.dtype)

    b = b_ref[0] if b_t else b_ref[...]
    part = lax.dot_general(a_ref[...].astype(BF16), b.astype(BF16), NT if b_t else NN,
                           preferred_element_type=F32)
    if nk == 1:
        finish(part)
    else:
        acc_ref = rest[0]
        k = pl.program_id(2)

        @pl.when(k == 0)
        def _():
            acc_ref[...] = part

        @pl.when(k > 0)
        def _():
            acc_ref[...] += part

        @pl.when(k == nk - 1)
        def _():
            finish(acc_ref[...])


def _mm_tiles(M, N, K, has_res):
    if has_res and K >= 4096 and M % 2048 == 0 and N % MXU_DIM == 0:
        return 2048, MXU_DIM, 4096
    return 1024, 512, 4096


def _mm(a, b, *, out_dtype=F32, act=None, res=None, tm=None, tn=None, tk=None, name="mm",
        layer=None, cols=None, krange=None, b_t=False):
    M, Ka = a.shape
    K2, nb = b.shape[-2:][::-1] if b_t else b.shape[-2:]
    assert Ka == K2 and (b.ndim == 3) == (layer is not None)
    c0, N = cols if cols is not None else (0, nb)
    kf, K = krange if krange is not None else (0, Ka)
    dm, dn, dk = _mm_tiles(M, N, K, res is not None)
    tm, tn, tk = _pick(M, tm or dm), _pick(N, tn or dn), _pick(K, tk or dk)
    assert kf % tk == 0
    k0 = kf // tk
    nk = K // tk
    if b_t:
        assert layer is not None and c0 % 8 == 0
        b_spec = pl.BlockSpec((pl.Element(1), pl.Element(tn), pl.Element(tk)),
                              lambda i, j, k: (layer, pl.multiple_of(c0 + j * tn, 8), (k0 + k) * tk))
    else:
        assert c0 % tn == 0
        j0 = c0 // tn
        if layer is None:
            b_spec = pl.BlockSpec((tk, tn), lambda i, j, k: (k0 + k, j0 + j))
        else:
            b_spec = pl.BlockSpec((None, tk, tn), lambda i, j, k: (layer, k0 + k, j0 + j))
    in_specs = [pl.BlockSpec((tm, tk), lambda i, j, k: (i, k0 + k)), b_spec]
    args = [a, b]
    if res is not None:
        in_specs.append(pl.BlockSpec((tm, tn), lambda i, j, k: (i, j)))
        args.append(res)
    scratch = [] if nk == 1 else [pltpu.VMEM((tm, tn), F32)]
    return pl.pallas_call(
        functools.partial(_mm_kernel, nk=nk, act=act, has_res=res is not None, b_t=b_t),
        grid=(M // tm, N // tn, nk),
        in_specs=in_specs,
        out_specs=pl.BlockSpec((tm, tn), lambda i, j, k: (i, j)),
        out_shape=jax.ShapeDtypeStruct((M, N), out_dtype),
        scratch_shapes=scratch,
        compiler_params=_cparams(("parallel", "parallel", "arbitrary")),
        name=name,
    )(*args)


def _mm_blocks_kernel(a_ref, b_ref, o_ref, *, trans_b):
    a = a_ref[...].astype(BF16)
    b = b_ref[...].astype(BF16)
    if trans_b:
        r = lax.dot_general(a, b, NT, preferred_element_type=F32)
    else:
        r = jnp.dot(a, b, preferred_element_type=F32)
    o_ref[...] = r.astype(o_ref.dtype)


def _rmsnorm_kernel(x_ref, g_ref, o_ref):
    x = x_ref[...].astype(F32)
    y = x * lax.rsqrt(jnp.mean(x * x, axis=-1, keepdims=True) + NORM_EPS)
    o_ref[...] = (y * g_ref[...]).astype(o_ref.dtype)


def _rmsnorm(x, g, out_dtype, tm=512):
    M, D = x.shape
    tm = _pick(M, tm)
    return pl.pallas_call(
        _rmsnorm_kernel,
        grid=(M // tm,),
        in_specs=[pl.BlockSpec((tm, D), lambda i: (i, 0)),
                  pl.BlockSpec((1, D), lambda i: (0, 0))],
        out_specs=pl.BlockSpec((tm, D), lambda i: (i, 0)),
        out_shape=jax.ShapeDtypeStruct((M, D), out_dtype),
        compiler_params=_cparams(("parallel",)),
        name="rmsnorm",
    )(x, g.reshape(1, D).astype(F32))


def _mlstm_kernel(q_ref, k_ref, v_ref, og_ref, gates_ref, bias_ref, ng_ref, y_hbm_ref, o_ref,
                  c_ref, n_ref, m_ref, *, heads):
    H = heads
    L = q_ref.shape[0]
    dk = q_ref.shape[1] // H
    dv = v_ref.shape[1] // H

    @pl.when(pl.program_id(0) == 0)
    def _():
        c_ref[...] = jnp.zeros_like(c_ref)
        n_ref[...] = jnp.zeros_like(n_ref)
        m_ref[...] = jnp.zeros_like(m_ref)

    pre = gates_ref[...] + bias_ref[...]
    capped = GATE_CAP * jnp.tanh(pre / GATE_CAP)
    lsig = _log_sigmoid(capped)
    lane = lax.broadcasted_iota(jnp.int32, capped.shape, 1)
    row = lax.broadcasted_iota(jnp.int32, (L, L), 0)
    col = lax.broadcasted_iota(jnp.int32, (L, L), 1)
    eye = row == col
    lower = col <= row

    def gate_terms(h):
        li_col = jnp.sum(jnp.where(lane == h, capped, 0.0), axis=1, keepdims=True)
        lf_col = jnp.sum(jnp.where(lane == h + H, lsig, 0.0), axis=1, keepdims=True)
        lf_row = jnp.sum(jnp.where(eye, lf_col, 0.0), axis=0, keepdims=True)
        li_row = jnp.sum(jnp.where(eye, li_col, 0.0), axis=0, keepdims=True)
        b_col = jnp.sum(jnp.where(lower, lf_row, 0.0), axis=1, keepdims=True)
        b_row = jnp.sum(jnp.where(row <= col, lf_col, 0.0), axis=0, keepdims=True)
        g_row = li_row - b_row
        g_col = li_col - b_col
        m_prev = m_ref[h][:, 0:1]
        a_col = jnp.maximum(m_prev, jnp.max(jnp.where(lower, g_row, -jnp.inf), axis=1, keepdims=True))
        a_last = jnp.max(a_col, axis=0, keepdims=True)
        b_last = jnp.sum(lf_col, axis=0, keepdims=True)
        return dict(dmat=jnp.where(lower, jnp.exp(g_row - a_col), 0.0), g_inter=jnp.exp(m_prev - a_col),
                    floor=jnp.exp(-(b_col + a_col)), w_col=jnp.exp(g_col - a_last),
                    decay=jnp.exp(m_prev - a_last), m_new=b_last + a_last)

    hs = range(H)
    gt = [gate_terms(h) for h in hs]
    q = [q_ref[:, h * dk:(h + 1) * dk] * (dk ** -0.5) for h in hs]
    k = [k_ref[:, h * dk:(h + 1) * dk] for h in hs]
    vb = [v_ref[:, h * dv:(h + 1) * dv].astype(BF16) for h in hs]
    qb = [x.astype(BF16) for x in q]
    cst = [c_ref[h] for h in hs]
    s = _bdot(qb, [x.astype(BF16) for x in k], "nt")
    s = [s[h] * gt[h]["dmat"] for h in hs]
    qc = _bdot(qb, [c.astype(BF16) for c in cst], "nn")
    sv = _bdot([x.astype(BF16) for x in s], vb, "nn")
    kw = [k[h] * gt[h]["w_col"] for h in hs]
    upd = _bdot([x.astype(BF16) for x in kw], vb, "tn")
    for h in hs:
        g = gt[h]
        num = sv[h] + g["g_inter"] * qc[h]
        qn = jnp.sum(q[h] * n_ref[h], axis=1, keepdims=True)
        den = jnp.sum(s[h], axis=1, keepdims=True) + g["g_inter"] * qn
        hh = num / jnp.maximum(jnp.abs(den), g["floor"])
        c_ref[h] = g["decay"] * cst[h] + upd[h]
        n_ref[h] = g["decay"] * n_ref[h] + jnp.sum(kw[h], axis=0, keepdims=True)
        m_ref[h] = jnp.broadcast_to(g["m_new"], m_ref.shape[1:])
        y = hh * lax.rsqrt(jnp.mean(hh * hh, axis=-1, keepdims=True) + NORM_EPS)
        cs = slice(h * dv, (h + 1) * dv)
        o_ref[:, cs] = (y * ng_ref[:, cs] * _sigmoid(og_ref[:, cs])).astype(o_ref.dtype)


def _mlstm(proj, gates, bias_row, norm_g, dk, dv, y):
    T = proj.shape[0]
    H = MLSTM_HEADS
    L = MLSTM_CHUNK
    wk, wv = H * dk, H * dv
    assert (2 * wk) % wv == 0
    v0 = (2 * wk) // wv
    return pl.pallas_call(
        functools.partial(_mlstm_kernel, heads=H),
        grid=(T // L,),
        in_specs=[pl.BlockSpec((L, wk), lambda c: (c, 0)),
                  pl.BlockSpec((L, wk), lambda c: (c, 1)),
                  pl.BlockSpec((L, wv), lambda c: (c, v0)),
                  pl.BlockSpec((L, wv), lambda c: (c, v0 + 1)),
                  pl.BlockSpec((L, LANES), lambda c: (c, 0)),
                  pl.BlockSpec((1, LANES), lambda c: (0, 0)),
                  pl.BlockSpec((1, wv), lambda c: (0, 0)),
                  pl.BlockSpec(memory_space=pl.ANY)],
        out_specs=pl.BlockSpec((L, wv), lambda c: (c, 0)),
        out_shape=jax.ShapeDtypeStruct(y.shape, y.dtype),
        input_output_aliases={7: 0},
        scratch_shapes=[pltpu.VMEM((H, dk, dv), F32), pltpu.VMEM((H, 1, dk), F32), pltpu.VMEM((H, 1, LANES), F32)],
        compiler_params=_cparams(("arbitrary",)),
        name="mlstm",
    )(proj, proj, proj, proj, gates, bias_row, norm_g.reshape(1, wv).astype(F32), y)


def _hgrn_kernel(q_ref, f_ref, i_ref, g_ref, lb_ref, ng_ref, y_hbm_ref, o_ref, st_ref, *, lb_row, chunk):
    TB = q_ref.shape[0]
    dk = HGRN_HEAD_DIM
    GW = st_ref.shape[1]
    NG = q_ref.shape[1] // GW
    HG = GW // dk
    C = chunk

    @pl.when(pl.program_id(1) == 0)
    def _():
        st_ref[...] = jnp.zeros_like(st_ref)

    logits = lb_ref[...].astype(F32)
    e = jnp.exp(logits - jnp.max(logits, axis=0, keepdims=True))
    p = e / jnp.sum(e, axis=0, keepdims=True)
    lb_all = jnp.sum(p[0:lb_row + 1], axis=0, keepdims=True)
    ng_all = ng_ref[...]
    srow = lax.broadcasted_iota(jnp.int32, (C, GW), 0)
    srow_h = srow[:, :dk]
    same = _same_head(GW, dk)

    def elementwise(g, ins):
        qb, fb, vb = ins
        lb = lb_all[:, g * GW:(g + 1) * GW]
        f = lb + (1.0 - lb) * _sigmoid(fb)
        kk = 1.0 - f
        qh = qb * _sigmoid(qb)
        A = jnp.log2(f)
        sh = 1
        while sh < C:
            A = A + jnp.where(srow >= sh, pltpu.roll(A, sh, axis=0), 0.0)
            sh *= 2
        a_last = A[C - 1:C]
        intra = []
        for h in range(HG):
            hs = slice(h * dk, (h + 1) * dk)
            Ah, kh, qhh, vh = A[:, hs], kk[:, hs], qh[:, hs], vb[:, hs]
            rows = []
            for t in range(C):
                rel = jnp.where(srow_h <= t, jnp.exp2(Ah[t:t + 1] - Ah), 0.0)
                sc = jnp.sum(rel * kh * qhh[t:t + 1], axis=1, keepdims=True)
                rows.append(jnp.sum(sc * vh, axis=0, keepdims=True))
            intra.append(jnp.concatenate(rows, axis=0))
        qe = (qh * jnp.exp2(A)).astype(BF16)
        ke = (kk * jnp.exp2(a_last - A)).astype(BF16)
        return qe, ke, vb.astype(BF16), jnp.concatenate(intra, axis=1), jnp.exp2(a_last)

    def body(ci, carry):
        rs = pl.ds(pl.multiple_of(ci * C, C), C)
        G = range(NG)
        gsl = [slice(g * GW, (g + 1) * GW) for g in G]
        st = [st_ref[g] for g in G]
        ew = [elementwise(g, tuple(ref[rs, gsl[g]] for ref in (q_ref, f_ref, i_ref))) for g in G]
        o_inter = _bdot([e[0] for e in ew], [s.astype(BF16) for s in st], "nt")
        upd = _bdot([e[2] for e in ew], [e[1] for e in ew], "tn")
        for g in G:
            st_ref[g] = st[g] * ew[g][4] + (upd[g] if HG == 1 else jnp.where(same, upd[g], 0.0))
            o = o_inter[g] + ew[g][3]
            ys = []
            for h in range(HG):
                oh = o[:, h * dk:(h + 1) * dk]
                ys.append(oh * lax.rsqrt(jnp.mean(oh * oh, axis=-1, keepdims=True) + NORM_EPS))
            gb = g_ref[rs, gsl[g]]
            o_ref[rs, gsl[g]] = (jnp.concatenate(ys, axis=1) * ng_all[:, gsl[g]]
                                 * (gb * _sigmoid(gb))).astype(o_ref.dtype)
        return carry

    lax.fori_loop(0, TB // C, body, 0)


def _hgrn(proj, lb_logits, lb_row, norm_g, y, tb=256):
    T = proj.shape[0]
    W = proj.shape[1] // 4
    sw = _pick(W, STEP_LANES)
    gw = HGRN_HEAD_DIM
    H = W // sw
    tb = _pick(T, tb)
    R = lb_logits.shape[0]
    assert (y.shape[1] - W) % sw == 0
    off = (y.shape[1] - W) // sw
    return pl.pallas_call(
        functools.partial(_hgrn_kernel, lb_row=lb_row, chunk=HGRN_CHUNK),
        grid=(H, T // tb),
        in_specs=[pl.BlockSpec((tb, sw), lambda h, c: (c, h)),
                  pl.BlockSpec((tb, sw), lambda h, c: (c, H + h)),
                  pl.BlockSpec((tb, sw), lambda h, c: (c, 2 * H + h)),
                  pl.BlockSpec((tb, sw), lambda h, c: (c, 3 * H + h)),
                  pl.BlockSpec((R, sw), lambda h, c: (0, h)),
                  pl.BlockSpec((1, sw), lambda h, c: (0, h)),
                  pl.BlockSpec(memory_space=pl.ANY)],
        out_specs=pl.BlockSpec((tb, sw), lambda h, c: (c, off + h)),
        out_shape=jax.ShapeDtypeStruct(y.shape, y.dtype),
        input_output_aliases={6: 0},
        scratch_shapes=[pltpu.VMEM((sw // gw, gw, gw), F32)],
        compiler_params=_cparams(("parallel", "arbitrary")),
        name="hgrn2",
    )(proj, proj, proj, proj, lb_logits.astype(F32), norm_g.reshape(1, W).astype(F32), y)


def _xattn_scores_kernel(h_ref, g_ref, w_ref, o_ref, *, scale, width):
    x = h_ref[...]
    xn = x * lax.rsqrt(jnp.mean(x * x, axis=-1, keepdims=True) + NORM_EPS) * g_ref[...]
    r = jnp.dot(xn.astype(BF16), w_ref[...], preferred_element_type=F32)
    for j in range(r.shape[1] // width):
        cs = slice(j * width, (j + 1) * width)
        s = r[:, cs] * scale
        e = jnp.exp(s - jnp.max(s, axis=-1, keepdims=True))
        o_ref[:, cs] = (e / jnp.sum(e, axis=-1, keepdims=True)).astype(o_ref.dtype)


def _xattn_scores(h, g, wqk, scale, width, tm=512):
    T, D = h.shape
    N = wqk.shape[1]
    tm = _pick(T, tm)
    return pl.pallas_call(
        functools.partial(_xattn_scores_kernel, scale=scale, width=width),
        grid=(T // tm,),
        in_specs=[pl.BlockSpec((tm, D), lambda i: (i, 0)),
                  pl.BlockSpec((1, D), lambda i: (0, 0)),
                  pl.BlockSpec((D, N), lambda i: (0, 0))],
        out_specs=pl.BlockSpec((tm, N), lambda i: (i, 0)),
        out_shape=jax.ShapeDtypeStruct((T, N), BF16),
        compiler_params=_cparams(("parallel",)),
        name="xattn_scores",
    )(h, g.reshape(1, D).astype(F32), wqk)


def _xattn_out_kernel(p_ref, w_ref, res_ref, g_ref, h_ref, hn_ref):
    h = res_ref[...] + jnp.dot(p_ref[...], w_ref[...], preferred_element_type=F32)
    h_ref[...] = h
    hn_ref[...] = (h * lax.rsqrt(jnp.mean(h * h, axis=-1, keepdims=True) + NORM_EPS) * g_ref[...]).astype(hn_ref.dtype)


def _xattn_out(p, vwo, res, g, tm=256):
    T, K = p.shape
    D = vwo.shape[1]
    tm = _pick(T, tm)
    row = pl.BlockSpec((tm, D), lambda i: (i, 0))
    return pl.pallas_call(
        _xattn_out_kernel,
        grid=(T // tm,),
        in_specs=[pl.BlockSpec((tm, K), lambda i: (i, 0)),
                  pl.BlockSpec((K, D), lambda i: (0, 0)),
                  row,
                  pl.BlockSpec((1, D), lambda i: (0, 0))],
        out_specs=[row, row],
        out_shape=[jax.ShapeDtypeStruct((T, D), F32), jax.ShapeDtypeStruct((T, D), BF16)],
        compiler_params=_cparams(("parallel",)),
        name="xattn_out",
    )(p, vwo, res, g.reshape(1, D).astype(F32))


def _xattn_weights(w_q, w_o, layer, mem_kv):
    D = w_q.shape[1]
    M = mem_kv.shape[0]
    H = XATTN_HEADS
    d = D // H
    tm = _pick(D, 1024)
    wqk = pl.pallas_call(
        functools.partial(_mm_blocks_kernel, trans_b=True),
        grid=(D // tm, H),
        in_specs=[pl.BlockSpec((None, tm, d), lambda i, h: (layer, i, h)),
                  pl.BlockSpec((M, d), lambda i, h: (0, h))],
        out_specs=pl.BlockSpec((tm, M), lambda i, h: (i, h)),
        out_shape=jax.ShapeDtypeStruct((D, H * M), BF16),
        compiler_params=_cparams(("parallel", "parallel")),
        name="xattn_wqk",
    )(w_q, mem_kv)
    tn = _pick(D, 1024)
    vwo = pl.pallas_call(
        functools.partial(_mm_blocks_kernel, trans_b=False),
        grid=(H, D // tn),
        in_specs=[pl.BlockSpec((M, d), lambda h, j: (0, H + h)),
                  pl.BlockSpec((None, d, tn), lambda h, j: (layer, h, j))],
        out_specs=pl.BlockSpec((M, tn), lambda h, j: (h, j)),
        out_shape=jax.ShapeDtypeStruct((H * M, D), BF16),
        compiler_params=_cparams(("parallel", "parallel")),
        name="xattn_vwo",
    )(mem_kv, w_o)
    return wqk, vwo


def _shift_kernel(h_ref, prev_ref, g_ref, mu_ref, *o_refs):
    def norm(h):
        return h * lax.rsqrt(jnp.mean(h * h, axis=-1, keepdims=True) + NORM_EPS) * g_ref[...]

    x = norm(h_ref[...])
    n = prev_ref.shape[0]
    last = jnp.where(pl.program_id(0) == 0, 0.0, norm(prev_ref[...])[n - 1:n, :])
    row = lax.broadcasted_iota(jnp.int32, x.shape, 0)
    xx = jnp.where(row == 0, last, pltpu.roll(x, 1, axis=0)) - x
    for j, o_ref in enumerate(o_refs):
        o_ref[...] = (x + xx * mu_ref[j:j + 1, :]).astype(o_ref.dtype)


def _rwkv_shift(h, g, mu, tm=256):
    T, D = h.shape
    J = mu.shape[0]
    tm = _pick(T, tm)
    pr = 8
    spec = pl.BlockSpec((tm, D), lambda i: (i, 0))
    return pl.pallas_call(
        _shift_kernel,
        grid=(T // tm,),
        in_specs=[spec,
                  pl.BlockSpec((pr, D), lambda i: (jnp.maximum(i * (tm // pr) - 1, 0), 0)),
                  pl.BlockSpec((1, D), lambda i: (0, 0)),
                  pl.BlockSpec((J, D), lambda i: (0, 0))],
        out_specs=[spec] * J,
        out_shape=[jax.ShapeDtypeStruct((T, D), BF16)] * J,
        compiler_params=_cparams(("parallel",)),
        name="rwkv_shift",
    )(h, h, g.reshape(1, D).astype(F32), mu.astype(F32))


def _split2(x):
    x1 = x.astype(BF16)
    return x1, (x - x1.astype(F32)).astype(BF16)


def _dot_exact_rhs(x, m):
    n = x.shape[0]
    d = jnp.dot(jnp.concatenate(_split2(x), axis=0), m, preferred_element_type=F32)
    return d[:n] + d[n:]


def _head_ones(n, head, scale, dtype):
    return jnp.where(_same_head(n, head), scale, 0.0).astype(dtype)


def _rwkv_prep_kernel(r_ref, k_ref, v_ref, lw_ref, la_ref, par_ref,
                      at_ref, rt_ref, bk_ref, tc_ref, rb_ref, akv_ref, rkv_ref, bon_ref, gl_ref, *, chunk):
    TB = r_ref.shape[0]
    L = chunk
    N = RWKV_HEAD
    GW = 4 * L
    HG = GW // N
    NG = r_ref.shape[1] // GW
    assert L == N and GW == MXU_DIM
    par = par_ref[...]
    seg1 = _head_ones(GW, N, 1.0, BF16)
    hmask = _head_masks(L, GW, N)
    lane = lax.broadcasted_iota(jnp.int32, (L, GW), 1)
    rowi = lax.broadcasted_iota(jnp.int32, (L, GW), 0)
    half = GW // 2
    lane_h = lax.broadcasted_iota(jnp.int32, (L, half), 1)
    row_h = lax.broadcasted_iota(jnp.int32, (L, half), 0)
    low = lane_h < N
    src = jnp.where(low, lane_h, lane_h - N)
    strict = src < row_h
    incl = src <= row_h
    eye_half = jnp.where(src == row_h, 1.0, 0.0)
    eye_cat = jnp.concatenate([eye_half, eye_half], axis=1)

    def elementwise(g, ins):
        gs = slice(g * GW, (g + 1) * GW)
        w0, a0, k_k, k_a, r_k = (par[i:i + 1, gs] for i in range(5))
        r, k, v, lw, la = ins
        logw = -(RWKV_DECAY_SCALE * LOG2E) * _sigmoid(w0 + lw)
        a_sig = _sigmoid(a0 + la)
        kkr = k * k_k
        k2 = k * (1.0 + (a_sig - 1.0) * k_a)
        cs = logw
        sh = 1
        while sh < L:
            cs = cs + jnp.where(rowi >= sh, pltpu.roll(cs, sh, axis=0), 0.0)
            sh *= 2
        return r, v, logw, a_sig, kkr, k2, cs, r * k2 * r_k

    def pre_dots(ew, ss, bsum):
        r, v, logw, a_sig, kkr, k2, cs, _ = ew
        kk = kkr / jnp.maximum(jnp.sqrt(ss), 1e-12)
        ginv = jnp.exp2(-cs)
        atb = (-kk * jnp.exp2(cs - logw)).astype(BF16)
        rtb = (r * jnp.exp2(cs)).astype(BF16)
        bt = (kk * a_sig * ginv).astype(BF16)
        kt = (k2 * ginv).astype(BF16)
        zero = jnp.zeros_like(atb)
        lhs = jnp.concatenate([jnp.where(m, x, zero) for m in hmask for x in (atb, rtb)], axis=0)
        return atb, rtb, bt, kt, bsum * v, jnp.exp2(cs[L - 1:L]), lhs, jnp.concatenate([bt, kt, kt, bt], axis=0)

    def cat(parts, even_sel, odd_sel, mask):
        cols = [jnp.where(mask, jnp.where(low, even_sel(parts[2 * c]), odd_sel(parts[2 * c + 1])), 0.0)
                for c in range(HG // 2)]
        return jnp.concatenate(cols, axis=1)

    lo = lambda x: x[:, :half]
    hi = lambda x: x[:, half:]

    def body(ci, carry):
        r0 = pl.multiple_of(ci * L, L)
        sl = pl.ds(r0, L)
        G = range(NG)
        ews = [elementwise(g, tuple(ref[sl, g * GW:(g + 1) * GW]
                                    for ref in (r_ref, k_ref, v_ref, lw_ref, la_ref))) for g in G]
        sums = _dot_exact_rhs(jnp.concatenate([x for ew in ews for x in (ew[4] * ew[4], ew[7])], axis=0), seg1)
        pre = [pre_dots(ews[g], sums[2 * g * L:(2 * g + 1) * L], sums[(2 * g + 1) * L:(2 * g + 2) * L]) for g in G]
        P = _bdot([p[6] for p in pre], [p[7] for p in pre], "nt")
        pat = [[P[g][2 * h * L:(2 * h + 1) * L] for h in range(HG)] for g in G]
        prt = [[P[g][(2 * h + 1) * L:(2 * h + 2) * L] for h in range(HG)] for g in G]
        ncat = [cat(pat[g], lo, hi, strict) for g in G]
        nb = [n.astype(BF16) for n in ncat]
        pw = _bdot(nb, [_stack_heads(n, hmask) for n in nb], "nn")
        tinv = [eye_cat + n for n in ncat]
        power = 2
        while power < L:
            pwb = [x.astype(BF16) for x in pw]
            pst = [_stack_heads(x, hmask) for x in pwb]
            if 2 * power < L:
                d = _bdot([jnp.concatenate([tinv[g].astype(BF16), pwb[g]], axis=0) for g in G], pst, "nn")
                tinv = [tinv[g] + d[g][:L] for g in G]
                pw = [d[g][L:] for g in G]
            else:
                d = _bdot([t.astype(BF16) for t in tinv], pst, "nn")
                tinv = [tinv[g] + d[g] for g in G]
            power *= 2
        akrk = [jnp.concatenate([cat(pat[g], hi, lo, strict), cat(prt[g], hi, lo, incl)], axis=0).astype(BF16)
                for g in G]
        kv = _bdot(akrk, [_stack_heads(ews[g][1].astype(BF16), hmask) for g in G], "nn")
        for g in G:
            gs = slice(g * GW, (g + 1) * GW)
            atb, rtb, bt, kt, bon, gl = pre[g][:6]
            at_ref[sl, gs] = atb
            rt_ref[sl, gs] = rtb
            bk_ref[pl.ds(pl.multiple_of(2 * r0, 2 * L), L), gs] = bt
            bk_ref[pl.ds(pl.multiple_of(2 * r0 + L, L), L), gs] = kt
            tc_ref[sl, gs] = tinv[g].astype(BF16)
            rb_ref[sl, gs] = cat(prt[g], lo, hi, incl).astype(BF16)
            akv_ref[sl, gs] = kv[g][:L]
            rkv_ref[sl, gs] = kv[g][L:]
            bon_ref[sl, gs] = bon
            gl_ref[ci, :, gs] = gl
        return carry

    lax.fori_loop(0, TB // L, body, 0)


def _rwkv_state_kernel(at_ref, rt_ref, bk_ref, tc_ref, rb_ref, akv_ref, rkv_ref, gl_ref, v_ref,
                       y_ref, ht_ref, *, chunk):
    TB = v_ref.shape[0]
    L = chunk
    N = RWKV_HEAD
    GW = ht_ref.shape[1]
    NG = v_ref.shape[1] // GW

    @pl.when(pl.program_id(1) == 0)
    def _():
        ht_ref[...] = jnp.zeros_like(ht_ref)

    hmask = _head_masks(L, GW, N)
    same = _same_head(GW, N)

    def body(ci, carry):
        r0 = pl.multiple_of(ci * L, L)
        sl = pl.ds(r0, L)
        sl2 = pl.ds(pl.multiple_of(2 * r0, 2 * L), 2 * L)
        G = range(NG)
        gsl = [slice(g * GW, (g + 1) * GW) for g in G]
        ht = [ht_ref[g] for g in G]
        xr = _bdot([jnp.concatenate([at_ref[sl, gs], rt_ref[sl, gs]], axis=0) for gs in gsl],
                   [h.astype(BF16) for h in ht], "nt")
        xb = [(xr[g][:L] + akv_ref[sl, gsl[g]]).astype(BF16) for g in G]
        u = _bdot([tc_ref[sl, gs] for gs in gsl], [_stack_heads(x, hmask) for x in xb], "nn")
        ub = [x.astype(BF16) for x in u]
        yd = _bdot([rb_ref[sl, gs] for gs in gsl], [_stack_heads(x, hmask) for x in ub], "nn")
        dht = _bdot([jnp.concatenate([ub[g], v_ref[sl, gsl[g]].astype(BF16)], axis=0) for g in G],
                    [bk_ref[sl2, gs] for gs in gsl], "tn")
        for g in G:
            ht_ref[g] = (ht[g] + jnp.where(same, dht[g], 0.0)) * gl_ref[ci, :, gsl[g]]
            y_ref[sl, gsl[g]] = xr[g][L:] + yd[g] + rkv_ref[sl, gsl[g]]
        return carry

    lax.fori_loop(0, TB // L, body, 0)


def _rwkv_out_kernel(y_ref, bon_ref, g_ref, ln_ref, o_ref):
    N = RWKV_HEAD
    GW = MXU_DIM
    seg_mean = _head_ones(GW, N, 1.0 / N, BF16)
    for g in range(y_ref.shape[1] // GW):
        gs = slice(g * GW, (g + 1) * GW)
        y = y_ref[:, gs]
        yc = y - _dot_exact_rhs(y, seg_mean)
        var = _dot_exact_rhs(yc * yc, seg_mean)
        yn = yc * lax.rsqrt(var + RWKV_LNX_EPS) * ln_ref[0:1, gs] + ln_ref[1:2, gs]
        o_ref[:, gs] = ((yn + bon_ref[:, gs]) * g_ref[:, gs]).astype(o_ref.dtype)


def _rwkv_core(r, k, v, lw, la, g, w0, a0, k_k, k_a, r_k, lnx_w, lnx_b, tb=RWKV_CHUNK):
    T, D = r.shape
    L = RWKV_CHUNK
    gw = MXU_DIM
    W = _pick(D, STEP_LANES)
    assert W % gw == 0
    P = D // W
    tb = _pick(T, tb)
    nc = tb // L
    par = jnp.stack([w0, a0, k_k, k_a, r_k.reshape(D)]).astype(F32)
    par = jnp.concatenate([par, jnp.zeros((3, D), F32)], axis=0)
    spec = pl.BlockSpec((tb, W), lambda p, c: (c, p))
    spec2 = pl.BlockSpec((2 * tb, W), lambda p, c: (c, p))
    gspec = pl.BlockSpec((nc, 1, W), lambda p, c: (c, 0, p))
    bf = jax.ShapeDtypeStruct((T, D), BF16)
    f32 = jax.ShapeDtypeStruct((T, D), F32)
    at, rt, bk, tc, rb, akv, rkv, bon, gl = pl.pallas_call(
        functools.partial(_rwkv_prep_kernel, chunk=L),
        grid=(P, T // tb),
        in_specs=[spec] * 5 + [pl.BlockSpec((8, W), lambda p, c: (0, p))],
        out_specs=[spec, spec, spec2, spec, spec, spec, spec, spec, gspec],
        out_shape=[bf, bf, jax.ShapeDtypeStruct((2 * T, D), BF16), bf, bf, f32, f32, f32,
                   jax.ShapeDtypeStruct((T // L, 1, D), F32)],
        compiler_params=_cparams(("parallel", "parallel")),
        name="rwkv_prep",
    )(r, k, v, lw, la, par)
    y = pl.pallas_call(
        functools.partial(_rwkv_state_kernel, chunk=L),
        grid=(P, T // tb),
        in_specs=[spec, spec, spec2, spec, spec, spec, spec, gspec, spec],
        out_specs=spec,
        out_shape=f32,
        scratch_shapes=[pltpu.VMEM((W // gw, gw, gw), F32)],
        compiler_params=_cparams(("parallel", "arbitrary")),
        name="rwkv_state",
    )(at, rt, bk, tc, rb, akv, rkv, gl, v)
    ln = jnp.stack([lnx_w, lnx_b]).astype(F32)
    ln = jnp.concatenate([ln, jnp.zeros((6, D), F32)], axis=0)
    return pl.pallas_call(
        _rwkv_out_kernel,
        grid=(P, T // tb),
        in_specs=[spec, spec, spec, pl.BlockSpec((8, W), lambda p, c: (0, p))],
        out_specs=spec,
        out_shape=bf,
        compiler_params=_cparams(("parallel", "parallel")),
        name="rwkv_out",
    )(y, bon, g, ln)


def _mixer_even(h, norm_g, w_in, j, b_i, b_f, mlstm_g, lb_logits, hgrn_g, w_out):
    D = h.shape[1]
    H = MLSTM_HEADS
    mix_a = D // 2
    dv = mix_a // H
    dk = dv // 2
    na = 2 * H * dk + 2 * mix_a
    hn = _rmsnorm(h, norm_g, BF16)
    w_t = jnp.swapaxes(w_in, 1, 2)
    nb = w_in.shape[2] - na - 2 * H
    proj_a = _mm(hn, w_t, layer=j, cols=(0, na), b_t=True, name="mm_mlstm_in")
    gates = _mm(hn, w_t, layer=j, cols=(na, LANES), b_t=True, name="mm_gates")
    proj_b = _mm(hn, w_t, layer=j, cols=(na + 2 * H, nb), b_t=True, name="mm_hgrn_in")
    bias_row = jnp.zeros((1, LANES), F32).at[0, :H].set(b_i.astype(F32)).at[0, H:2 * H].set(b_f.astype(F32))
    y = jnp.zeros((h.shape[0], D), BF16)
    y = _mlstm(proj_a, gates, bias_row, mlstm_g, dk, dv, y)
    y = _hgrn(proj_b, lb_logits, j, hgrn_g, y)
    return _mm(y, w_out, layer=j, res=h, name="mm_mix_out")


def _mixer_odd(h, norm_g, j, mu, w0, w1, w2, a0, a1, a2, g1, g2, k_k, k_a, r_k, w_r, w_k, w_v, w_o, lnx_w, lnx_b):
    s_r, s_w, s_k, s_v, s_a, s_g = _rwkv_shift(h, norm_g, mu[j])
    r = _mm(s_r, w_r, layer=j, name="mm_rwkv_r")
    k = _mm(s_k, w_k, layer=j, name="mm_rwkv_k")
    v = _mm(s_v, w_v, layer=j, name="mm_rwkv_v")
    wide = dict(tm=512, tn=h.shape[1])
    lw = _mm(_mm(s_w, w1, layer=j, act="tanh", out_dtype=BF16, name="mm_lora_w1"), w2, layer=j,
             name="mm_lora_w2", **wide)
    la = _mm(_mm(s_a, a1, layer=j, out_dtype=BF16, name="mm_lora_a1"), a2, layer=j, name="mm_lora_a2", **wide)
    g = _mm(_mm(s_g, g1, layer=j, act="sigmoid", out_dtype=BF16, name="mm_lora_g1"), g2, layer=j,
            name="mm_lora_g2", **wide)
    y = _rwkv_core(r, k, v, lw, la, g, w0[j], a0[j], k_k[j], k_a[j], r_k[j], lnx_w[j], lnx_b[j])
    return _mm(y, w_o, layer=j, res=h, name="mm_rwkv_out")


def kernel(x, mem, norm_mix_g, norm_xattn_g, norm_mlp_g, final_norm_g, mem_norm_g, ab_w_in, mlstm_b_i, mlstm_b_f, mlstm_norm_g, hgrn_lb_logits, hgrn_norm_g, ab_w_out, rwkv_mu, rwkv_w0, rwkv_w1, rwkv_w2, rwkv_a0, rwkv_a1, rwkv_a2, rwkv_g1, rwkv_g2, rwkv_k_k, rwkv_k_a, rwkv_r_k, rwkv_w_r, rwkv_w_k, rwkv_w_v, rwkv_w_o, rwkv_lnx_w, rwkv_lnx_b, xattn_w_q, xattn_w_o, mem_w_kv, mlp_w_up, mlp_w_down):
    B, T, D = x.shape
    depth = norm_mix_g.shape[0]
    M = mem.shape[1]
    xscale = (D // XATTN_HEADS) ** -0.5
    outs = []
    for b in range(B):
        mem_kv = _mm(_rmsnorm(mem[b], mem_norm_g, BF16), mem_w_kv, out_dtype=BF16, name="mm_mem_kv")
        h = x[b]
        for layer in range(depth):
            j = layer // 2
            if layer % 2 == 0:
                h = _mixer_even(h, norm_mix_g[layer], ab_w_in, j, mlstm_b_i[j], mlstm_b_f[j], mlstm_norm_g[j],
                                hgrn_lb_logits, hgrn_norm_g[j], ab_w_out)
            else:
                h = _mixer_odd(h, norm_mix_g[layer], j, rwkv_mu, rwkv_w0, rwkv_w1, rwkv_w2, rwkv_a0, rwkv_a1,
                               rwkv_a2, rwkv_g1, rwkv_g2, rwkv_k_k, rwkv_k_a, rwkv_r_k,
                               rwkv_w_r, rwkv_w_k, rwkv_w_v, rwkv_w_o, rwkv_lnx_w, rwkv_lnx_b)
            wqk, vwo = _xattn_weights(xattn_w_q, xattn_w_o, layer, mem_kv)
            p = _xattn_scores(h, norm_xattn_g[layer], wqk, xscale, M)
            h, hn = _xattn_out(p, vwo, h, norm_mlp_g[layer])
            up = _mm(hn, mlp_w_up, layer=layer, act="relu2", out_dtype=BF16, name="mm_mlp_up")
            for k0 in range(0, up.shape[1], D):
                h = _mm(up, mlp_w_down, layer=layer, res=h, krange=(k0, D), name="mm_mlp_down")
        outs.append(_rmsnorm(h, final_norm_g, F32))
    return jnp.stack(outs)
```

```python
import functools
import math

import jax
import jax.numpy as jnp
from jax import lax
from jax.experimental import pallas as pl
from jax.experimental.pallas import tpu as pltpu

F32 = jnp.float32
BF16 = jnp.bfloat16

NORM_EPS = 1e-6
GATE_CAP = 15.0
MLSTM_HEADS = 4
MLSTM_CHUNK = 64
HGRN_HEAD_DIM = 128
HGRN_CHUNK = 16
RWKV_HEAD = 64
RWKV_CHUNK = 64
RWKV_LNX_EPS = 64e-5
RWKV_DECAY_SCALE = math.exp(-0.5)
LOG2E = math.log2(math.e)
XATTN_HEADS = 4
LANES = 128
MXU_DIM = 256
STEP_LANES = 4096
VMEM_LIMIT = 60 * 1024 * 1024

NN = (((1,), (0,)), ((), ()))
NT = (((1,), (1,)), ((), ()))
TN = (((0,), (0,)), ((), ()))


def _cparams(sem):
    return pltpu.CompilerParams(dimension_semantics=sem, vmem_limit_bytes=VMEM_LIMIT)


def _sigmoid(x):
    return 1.0 / (1.0 + jnp.exp(-x))


def _log_sigmoid(x):
    return jnp.minimum(x, 0.0) - jnp.log1p(jnp.exp(-jnp.abs(x)))


def _pick(dim, pref):
    t = min(dim, pref)
    while dim % t:
        t //= 2
    return t


def _head_masks(rows, width, head):
    lane = lax.broadcasted_iota(jnp.int32, (rows, width), 1) // head
    return [lane == h for h in range(width // head)]


def _stack_heads(x, masks):
    z = jnp.zeros_like(x)
    return jnp.concatenate([jnp.where(m, x, z) for m in masks], axis=0)


_BDOT = {"nn": "gmk,gkn->gmn", "nt": "gmk,gnk->gmn", "tn": "gtm,gtn->gmn"}


def _bdot(xs, ys, mode):
    d = jnp.einsum(_BDOT[mode], jnp.stack(xs), jnp.stack(ys), preferred_element_type=F32)
    return [d[g] for g in range(len(xs))]


def _same_head(n, head):
    r = lax.broadcasted_iota(jnp.int32, (n, n), 0) // head
    c = lax.broadcasted_iota(jnp.int32, (n, n), 1) // head
    return r == c


def _mm_kernel(*refs, nk, act, has_res, b_t):
    if has_res:
        a_ref, b_ref, res_ref, o_ref = refs[:4]
        rest = refs[4:]
    else:
        a_ref, b_ref, o_ref = refs[:3]
        res_ref = None
        rest = refs[3:]

    def finish(r):
        if act == "relu2":
            r = jnp.square(jnp.maximum(r, 0.0))
        elif act == "tanh":
            r = jnp.tanh(r)
        elif act == "sigmoid":
            r = _sigmoid(r)
        if has_res:
            r = r + res_ref[...].astype(F32)
        o_ref[...] = r.astype(o_ref.dtype)

    b = b_ref[0] if b_t else b_ref[...]
    part = lax.dot_general(a_ref[...].astype(BF16), b.astype(BF16), NT if b_t else NN,
                           preferred_element_type=F32)
    if nk == 1:
        finish(part)
    else:
        acc_ref = rest[0]
        k = pl.program_id(2)

        @pl.when(k == 0)
        def _():
            acc_ref[...] = part

        @pl.when(k > 0)
        def _():
            acc_ref[...] += part

        @pl.when(k == nk - 1)
        def _():
            finish(acc_ref[...])


def _mm_tiles(M, N, K, has_res):
    if has_res and K >= 4096 and M % 2048 == 0 and N % MXU_DIM == 0:
        return 2048, MXU_DIM, 4096
    return 1024, 512, 4096


def _mm(a, b, *, out_dtype=F32, act=None, res=None, tm=None, tn=None, tk=None, name="mm",
        layer=None, cols=None, krange=None, b_t=False):
    M, Ka = a.shape
    K2, nb = b.shape[-2:][::-1] if b_t else b.shape[-2:]
    assert Ka == K2 and (b.ndim == 3) == (layer is not None)
    c0, N = cols if cols is not None else (0, nb)
    kf, K = krange if krange is not None else (0, Ka)
    dm, dn, dk = _mm_tiles(M, N, K, res is not None)
    tm, tn, tk = _pick(M, tm or dm), _pick(N, tn or dn), _pick(K, tk or dk)
    assert kf % tk == 0
    k0 = kf // tk
    nk = K // tk
    if b_t:
        assert layer is not None and c0 % 8 == 0
        b_spec = pl.BlockSpec((pl.Element(1), pl.Element(tn), pl.Element(tk)),
                              lambda i, j, k: (layer, pl.multiple_of(c0 + j * tn, 8), (k0 + k) * tk))
    else:
        assert c0 % tn == 0
        j0 = c0 // tn
        if layer is None:
            b_spec = pl.BlockSpec((tk, tn), lambda i, j, k: (k0 + k, j0 + j))
        else:
            b_spec = pl.BlockSpec((None, tk, tn), lambda i, j, k: (layer, k0 + k, j0 + j))
    in_specs = [pl.BlockSpec((tm, tk), lambda i, j, k: (i, k0 + k)), b_spec]
    args = [a, b]
    if res is not None:
        in_specs.append(pl.BlockSpec((tm, tn), lambda i, j, k: (i, j)))
        args.append(res)
    scratch = [] if nk == 1 else [pltpu.VMEM((tm, tn), F32)]
    return pl.pallas_call(
        functools.partial(_mm_kernel, nk=nk, act=act, has_res=res is not None, b_t=b_t),
        grid=(M // tm, N // tn, nk),
        in_specs=in_specs,
        out_specs=pl.BlockSpec((tm, tn), lambda i, j, k: (i, j)),
        out_shape=jax.ShapeDtypeStruct((M, N), out_dtype),
        scratch_shapes=scratch,
        compiler_params=_cparams(("parallel", "parallel", "arbitrary")),
        name=name,
    )(*args)


def _mm_blocks_kernel(a_ref, b_ref, o_ref, *, trans_b):
    a = a_ref[...].astype(BF16)
    b = b_ref[...].astype(BF16)
    if trans_b:
        r = lax.dot_general(a, b, NT, preferred_element_type=F32)
    else:
        r = jnp.dot(a, b, preferred_element_type=F32)
    o_ref[...] = r.astype(o_ref.dtype)


def _rmsnorm_kernel(x_ref, g_ref, o_ref):
    x = x_ref[...].astype(F32)
    y = x * lax.rsqrt(jnp.mean(x * x, axis=-1, keepdims=True) + NORM_EPS)
    o_ref[...] = (y * g_ref[...]).astype(o_ref.dtype)


def _rmsnorm(x, g, out_dtype, tm=512):
    M, D = x.shape
    tm = _pick(M, tm)
    return pl.pallas_call(
        _rmsnorm_kernel,
        grid=(M // tm,),
        in_specs=[pl.BlockSpec((tm, D), lambda i: (i, 0)),
                  pl.BlockSpec((1, D), lambda i: (0, 0))],
        out_specs=pl.BlockSpec((tm, D), lambda i: (i, 0)),
        out_shape=jax.ShapeDtypeStruct((M, D), out_dtype),
        compiler_params=_cparams(("parallel",)),
        name="rmsnorm",
    )(x, g.reshape(1, D).astype(F32))


def _mlstm_kernel(q_ref, k_ref, v_ref, og_ref, gates_ref, bias_ref, ng_ref, y_hbm_ref, o_ref,
                  c_ref, n_ref, m_ref, *, heads):
    H = heads
    L = q_ref.shape[0]
    dk = q_ref.shape[1] // H
    dv = v_ref.shape[1] // H

    @pl.when(pl.program_id(0) == 0)
    def _():
        c_ref[...] = jnp.zeros_like(c_ref)
        n_ref[...] = jnp.zeros_like(n_ref)
        m_ref[...] = jnp.zeros_like(m_ref)

    pre = gates_ref[...] + bias_ref[...]
    capped = GATE_CAP * jnp.tanh(pre / GATE_CAP)
    lsig = _log_sigmoid(capped)
    lane = lax.broadcasted_iota(jnp.int32, capped.shape, 1)
    row = lax.broadcasted_iota(jnp.int32, (L, L), 0)
    col = lax.broadcasted_iota(jnp.int32, (L, L), 1)
    eye = row == col
    lower = col <= row

    def gate_terms(h):
        li_col = jnp.sum(jnp.where(lane == h, capped, 0.0), axis=1, keepdims=True)
        lf_col = jnp.sum(jnp.where(lane == h + H, lsig, 0.0), axis=1, keepdims=True)
        lf_row = jnp.sum(jnp.where(eye, lf_col, 0.0), axis=0, keepdims=True)
        li_row = jnp.sum(jnp.where(eye, li_col, 0.0), axis=0, keepdims=True)
        b_col = jnp.sum(jnp.where(lower, lf_row, 0.0), axis=1, keepdims=True)
        b_row = jnp.sum(jnp.where(row <= col, lf_col, 0.0), axis=0, keepdims=True)
        g_row = li_row - b_row
        g_col = li_col - b_col
        m_prev = m_ref[h][:, 0:1]
        a_col = jnp.maximum(m_prev, jnp.max(jnp.where(lower, g_row, -jnp.inf), axis=1, keepdims=True))
        a_last = jnp.max(a_col, axis=0, keepdims=True)
        b_last = jnp.sum(lf_col, axis=0, keepdims=True)
        return dict(dmat=jnp.where(lower, jnp.exp(g_row - a_col), 0.0), g_inter=jnp.exp(m_prev - a_col),
                    floor=jnp.exp(-(b_col + a_col)), w_col=jnp.exp(g_col - a_last),
                    decay=jnp.exp(m_prev - a_last), m_new=b_last + a_last)

    hs = range(H)
    gt = [gate_terms(h) for h in hs]
    q = [q_ref[:, h * dk:(h + 1) * dk] * (dk ** -0.5) for h in hs]
    k = [k_ref[:, h * dk:(h + 1) * dk] for h in hs]
    vb = [v_ref[:, h * dv:(h + 1) * dv].astype(BF16) for h in hs]
    qb = [x.astype(BF16) for x in q]
    cst = [c_ref[h] for h in hs]
    s = _bdot(qb, [x.astype(BF16) for x in k], "nt")
    s = [s[h] * gt[h]["dmat"] for h in hs]
    qc = _bdot(qb, [c.astype(BF16) for c in cst], "nn")
    sv = _bdot([x.astype(BF16) for x in s], vb, "nn")
    kw = [k[h] * gt[h]["w_col"] for h in hs]
    upd = _bdot([x.astype(BF16) for x in kw], vb, "tn")
    for h in hs:
        g = gt[h]
        num = sv[h] + g["g_inter"] * qc[h]
        qn = jnp.sum(q[h] * n_ref[h], axis=1, keepdims=True)
        den = jnp.sum(s[h], axis=1, keepdims=True) + g["g_inter"] * qn
        hh = num / jnp.maximum(jnp.abs(den), g["floor"])
        c_ref[h] = g["decay"] * cst[h] + upd[h]
        n_ref[h] = g["decay"] * n_ref[h] + jnp.sum(kw[h], axis=0, keepdims=True)
        m_ref[h] = jnp.broadcast_to(g["m_new"], m_ref.shape[1:])
        y = hh * lax.rsqrt(jnp.mean(hh * hh, axis=-1, keepdims=True) + NORM_EPS)
        cs = slice(h * dv, (h + 1) * dv)
        o_ref[:, cs] = (y * ng_ref[:, cs] * _sigmoid(og_ref[:, cs])).astype(o_ref.dtype)


def _mlstm(proj, gates, bias_row, norm_g, dk, dv, y):
    T = proj.shape[0]
    H = MLSTM_HEADS
    L = MLSTM_CHUNK
    wk, wv = H * dk, H * dv
    assert (2 * wk) % wv == 0
    v0 = (2 * wk) // wv
    return pl.pallas_call(
        functools.partial(_mlstm_kernel, heads=H),
        grid=(T // L,),
        in_specs=[pl.BlockSpec((L, wk), lambda c: (c, 0)),
                  pl.BlockSpec((L, wk), lambda c: (c, 1)),
                  pl.BlockSpec((L, wv), lambda c: (c, v0)),
                  pl.BlockSpec((L, wv), lambda c: (c, v0 + 1)),
                  pl.BlockSpec((L, LANES), lambda c: (c, 0)),
                  pl.BlockSpec((1, LANES), lambda c: (0, 0)),
                  pl.BlockSpec((1, wv), lambda c: (0, 0)),
                  pl.BlockSpec(memory_space=pl.ANY)],
        out_specs=pl.BlockSpec((L, wv), lambda c: (c, 0)),
        out_shape=jax.ShapeDtypeStruct(y.shape, y.dtype),
        input_output_aliases={7: 0},
        scratch_shapes=[pltpu.VMEM((H, dk, dv), F32), pltpu.VMEM((H, 1, dk), F32), pltpu.VMEM((H, 1, LANES), F32)],
        compiler_params=_cparams(("arbitrary",)),
        name="mlstm",
    )(proj, proj, proj, proj, gates, bias_row, norm_g.reshape(1, wv).astype(F32), y)


def _hgrn_kernel(q_ref, f_ref, i_ref, g_ref, lb_ref, ng_ref, y_hbm_ref, o_ref, st_ref, *, lb_row, chunk):
    TB = q_ref.shape[0]
    dk = HGRN_HEAD_DIM
    GW = st_ref.shape[1]
    NG = q_ref.shape[1] // GW
    HG = GW // dk
    C = chunk

    @pl.when(pl.program_id(1) == 0)
    def _():
        st_ref[...] = jnp.zeros_like(st_ref)

    logits = lb_ref[...].astype(F32)
    e = jnp.exp(logits - jnp.max(logits, axis=0, keepdims=True))
    p = e / jnp.sum(e, axis=0, keepdims=True)
    lb_all = jnp.sum(p[0:lb_row + 1], axis=0, keepdims=True)
    ng_all = ng_ref[...]
    srow = lax.broadcasted_iota(jnp.int32, (C, GW), 0)
    srow_h = srow[:, :dk]
    same = _same_head(GW, dk)

    def elementwise(g, ins):
        qb, fb, vb = ins
        lb = lb_all[:, g * GW:(g + 1) * GW]
        f = lb + (1.0 - lb) * _sigmoid(fb)
        kk = 1.0 - f
        qh = qb * _sigmoid(qb)
        A = jnp.log2(f)
        sh = 1
        while sh < C:
            A = A + jnp.where(srow >= sh, pltpu.roll(A, sh, axis=0), 0.0)
            sh *= 2
        a_last = A[C - 1:C]
        intra = []
        for h in range(HG):
            hs = slice(h * dk, (h + 1) * dk)
            Ah, kh, qhh, vh = A[:, hs], kk[:, hs], qh[:, hs], vb[:, hs]
            rows = []
            for t in range(C):
                rel = jnp.where(srow_h <= t, jnp.exp2(Ah[t:t + 1] - Ah), 0.0)
                sc = jnp.sum(rel * kh * qhh[t:t + 1], axis=1, keepdims=True)
                rows.append(jnp.sum(sc * vh, axis=0, keepdims=True))
            intra.append(jnp.concatenate(rows, axis=0))
        qe = (qh * jnp.exp2(A)).astype(BF16)
        ke = (kk * jnp.exp2(a_last - A)).astype(BF16)
        return qe, ke, vb.astype(BF16), jnp.concatenate(intra, axis=1), jnp.exp2(a_last)

    def body(ci, carry):
        rs = pl.ds(pl.multiple_of(ci * C, C), C)
        G = range(NG)
        gsl = [slice(g * GW, (g + 1) * GW) for g in G]
        st = [st_ref[g] for g in G]
        ew = [elementwise(g, tuple(ref[rs, gsl[g]] for ref in (q_ref, f_ref, i_ref))) for g in G]
        o_inter = _bdot([e[0] for e in ew], [s.astype(BF16) for s in st], "nt")
        upd = _bdot([e[2] for e in ew], [e[1] for e in ew], "tn")
        for g in G:
            st_ref[g] = st[g] * ew[g][4] + (upd[g] if HG == 1 else jnp.where(same, upd[g], 0.0))
            o = o_inter[g] + ew[g][3]
            ys = []
            for h in range(HG):
                oh = o[:, h * dk:(h + 1) * dk]
                ys.append(oh * lax.rsqrt(jnp.mean(oh * oh, axis=-1, keepdims=True) + NORM_EPS))
            gb = g_ref[rs, gsl[g]]
            o_ref[rs, gsl[g]] = (jnp.concatenate(ys, axis=1) * ng_all[:, gsl[g]]
                                 * (gb * _sigmoid(gb))).astype(o_ref.dtype)
        return carry

    lax.fori_loop(0, TB // C, body, 0)


def _hgrn(proj, lb_logits, lb_row, norm_g, y, tb=256):
    T = proj.shape[0]
    W = proj.shape[1] // 4
    sw = _pick(W, STEP_LANES)
    gw = HGRN_HEAD_DIM
    H = W // sw
    tb = _pick(T, tb)
    R = lb_logits.shape[0]
    assert (y.shape[1] - W) % sw == 0
    off = (y.shape[1] - W) // sw
    return pl.pallas_call(
        functools.partial(_hgrn_kernel, lb_row=lb_row, chunk=HGRN_CHUNK),
        grid=(H, T // tb),
        in_specs=[pl.BlockSpec((tb, sw), lambda h, c: (c, h)),
                  pl.BlockSpec((tb, sw), lambda h, c: (c, H + h)),
                  pl.BlockSpec((tb, sw), lambda h, c: (c, 2 * H + h)),
                  pl.BlockSpec((tb, sw), lambda h, c: (c, 3 * H + h)),
                  pl.BlockSpec((R, sw), lambda h, c: (0, h)),
                  pl.BlockSpec((1, sw), lambda h, c: (0, h)),
                  pl.BlockSpec(memory_space=pl.ANY)],
        out_specs=pl.BlockSpec((tb, sw), lambda h, c: (c, off + h)),
        out_shape=jax.ShapeDtypeStruct(y.shape, y.dtype),
        input_output_aliases={6: 0},
        scratch_shapes=[pltpu.VMEM((sw // gw, gw, gw), F32)],
        compiler_params=_cparams(("parallel", "arbitrary")),
        name="hgrn2",
    )(proj, proj, proj, proj, lb_logits.astype(F32), norm_g.reshape(1, W).astype(F32), y)


def _xattn_scores_kernel(h_ref, g_ref, w_ref, o_ref, *, scale, width):
    x = h_ref[...]
    xn = x * lax.rsqrt(jnp.mean(x * x, axis=-1, keepdims=True) + NORM_EPS) * g_ref[...]
    r = jnp.dot(xn.astype(BF16), w_ref[...], preferred_element_type=F32)
    for j in range(r.shape[1] // width):
        cs = slice(j * width, (j + 1) * width)
        s = r[:, cs] * scale
        e = jnp.exp(s - jnp.max(s, axis=-1, keepdims=True))
        o_ref[:, cs] = (e / jnp.sum(e, axis=-1, keepdims=True)).astype(o_ref.dtype)


def _xattn_scores(h, g, wqk, scale, width, tm=512):
    T, D = h.shape
    N = wqk.shape[1]
    tm = _pick(T, tm)
    return pl.pallas_call(
        functools.partial(_xattn_scores_kernel, scale=scale, width=width),
        grid=(T // tm,),
        in_specs=[pl.BlockSpec((tm, D), lambda i: (i, 0)),
                  pl.BlockSpec((1, D), lambda i: (0, 0)),
                  pl.BlockSpec((D, N), lambda i: (0, 0))],
        out_specs=pl.BlockSpec((tm, N), lambda i: (i, 0)),
        out_shape=jax.ShapeDtypeStruct((T, N), BF16),
        compiler_params=_cparams(("parallel",)),
        name="xattn_scores",
    )(h, g.reshape(1, D).astype(F32), wqk)


def _xattn_out_kernel(p_ref, w_ref, res_ref, g_ref, h_ref, hn_ref):
    h = res_ref[...] + jnp.dot(p_ref[...], w_ref[...], preferred_element_type=F32)
    h_ref[...] = h
    hn_ref[...] = (h * lax.rsqrt(jnp.mean(h * h, axis=-1, keepdims=True) + NORM_EPS) * g_ref[...]).astype(hn_ref.dtype)


def _xattn_out(p, vwo, res, g, tm=256):
    T, K = p.shape
    D = vwo.shape[1]
    tm = _pick(T, tm)
    row = pl.BlockSpec((tm, D), lambda i: (i, 0))
    return pl.pallas_call(
        _xattn_out_kernel,
        grid=(T // tm,),
        in_specs=[pl.BlockSpec((tm, K), lambda i: (i, 0)),
                  pl.BlockSpec((K, D), lambda i: (0, 0)),
                  row,
                  pl.BlockSpec((1, D), lambda i: (0, 0))],
        out_specs=[row, row],
        out_shape=[jax.ShapeDtypeStruct((T, D), F32), jax.ShapeDtypeStruct((T, D), BF16)],
        compiler_params=_cparams(("parallel",)),
        name="xattn_out",
    )(p, vwo, res, g.reshape(1, D).astype(F32))


def _xattn_weights(w_q, w_o, layer, mem_kv):
    D = w_q.shape[1]
    M = mem_kv.shape[0]
    H = XATTN_HEADS
    d = D // H
    tm = _pick(D, 1024)
    wqk = pl.pallas_call(
        functools.partial(_mm_blocks_kernel, trans_b=True),
        grid=(D // tm, H),
        in_specs=[pl.BlockSpec((None, tm, d), lambda i, h: (layer, i, h)),
                  pl.BlockSpec((M, d), lambda i, h: (0, h))],
        out_specs=pl.BlockSpec((tm, M), lambda i, h: (i, h)),
        out_shape=jax.ShapeDtypeStruct((D, H * M), BF16),
        compiler_params=_cparams(("parallel", "parallel")),
        name="xattn_wqk",
    )(w_q, mem_kv)
    tn = _pick(D, 1024)
    vwo = pl.pallas_call(
        functools.partial(_mm_blocks_kernel, trans_b=False),
        grid=(H, D // tn),
        in_specs=[pl.BlockSpec((M, d), lambda h, j: (0, H + h)),
                  pl.BlockSpec((None, d, tn), lambda h, j: (layer, h, j))],
        out_specs=pl.BlockSpec((M, tn), lambda h, j: (h, j)),
        out_shape=jax.ShapeDtypeStruct((H * M, D), BF16),
        compiler_params=_cparams(("parallel", "parallel")),
        name="xattn_vwo",
    )(mem_kv, w_o)
    return wqk, vwo


def _shift_kernel(h_ref, prev_ref, g_ref, mu_ref, w1_ref, a1_ref, g1_ref,
                  sr_ref, sk_ref, sv_ref, lw_ref, la_ref, lg_ref):
    def norm(h):
        return h * lax.rsqrt(jnp.mean(h * h, axis=-1, keepdims=True) + NORM_EPS) * g_ref[...]

    x = norm(h_ref[...])
    n = prev_ref.shape[0]
    last = jnp.where(pl.program_id(0) == 0, 0.0, norm(prev_ref[...])[n - 1:n, :])
    row = lax.broadcasted_iota(jnp.int32, x.shape, 0)
    xx = jnp.where(row == 0, last, pltpu.roll(x, 1, axis=0)) - x

    def mix(j):
        return (x + xx * mu_ref[j:j + 1, :]).astype(BF16)

    def low_rank(j, w_ref):
        return jnp.dot(mix(j), w_ref[...].astype(BF16), preferred_element_type=F32)

    sr_ref[...] = mix(0)
    sk_ref[...] = mix(2)
    sv_ref[...] = mix(3)
    lw_ref[...] = jnp.tanh(low_rank(1, w1_ref)).astype(lw_ref.dtype)
    la_ref[...] = low_rank(4, a1_ref).astype(la_ref.dtype)
    lg_ref[...] = _sigmoid(low_rank(5, g1_ref)).astype(lg_ref.dtype)


def _rwkv_shift(h, g, mu, w1, a1, g1, layer, tm=256):
    T, D = h.shape
    J = mu.shape[0]
    tm = _pick(T, tm)
    pr = 8
    spec = pl.BlockSpec((tm, D), lambda i: (i, 0))
    lows = [w.shape[2] for w in (w1, a1, g1)]
    return pl.pallas_call(
        _shift_kernel,
        grid=(T // tm,),
        in_specs=[spec,
                  pl.BlockSpec((pr, D), lambda i: (jnp.maximum(i * (tm // pr) - 1, 0), 0)),
                  pl.BlockSpec((1, D), lambda i: (0, 0)),
                  pl.BlockSpec((J, D), lambda i: (0, 0))]
                 + [pl.BlockSpec((None, D, n), lambda i: (layer, 0, 0)) for n in lows],
        out_specs=[spec] * 3 + [pl.BlockSpec((tm, n), lambda i: (i, 0)) for n in lows],
        out_shape=[jax.ShapeDtypeStruct((T, D), BF16)] * 3 + [jax.ShapeDtypeStruct((T, n), BF16) for n in lows],
        compiler_params=_cparams(("parallel",)),
        name="rwkv_shift",
    )(h, h, g.reshape(1, D).astype(F32), mu.astype(F32), w1, a1, g1)


def _split2(x):
    x1 = x.astype(BF16)
    return x1, (x - x1.astype(F32)).astype(BF16)


def _dot_exact_rhs(x, m):
    n = x.shape[0]
    d = jnp.dot(jnp.concatenate(_split2(x), axis=0), m, preferred_element_type=F32)
    return d[:n] + d[n:]


def _head_ones(n, head, scale, dtype):
    return jnp.where(_same_head(n, head), scale, 0.0).astype(dtype)


def _rwkv_prep_kernel(r_ref, k_ref, v_ref, lw_ref, la_ref, par_ref,
                      at_ref, rt_ref, bk_ref, tc_ref, rb_ref, akv_ref, rkv_ref, bon_ref, gl_ref, *, chunk):
    TB = r_ref.shape[0]
    L = chunk
    N = RWKV_HEAD
    GW = 4 * L
    HG = GW // N
    NG = r_ref.shape[1] // GW
    assert L == N and GW == MXU_DIM
    par = par_ref[...]
    seg1 = _head_ones(GW, N, 1.0, BF16)
    hmask = _head_masks(L, GW, N)
    lane = lax.broadcasted_iota(jnp.int32, (L, GW), 1)
    rowi = lax.broadcasted_iota(jnp.int32, (L, GW), 0)
    half = GW // 2
    lane_h = lax.broadcasted_iota(jnp.int32, (L, half), 1)
    row_h = lax.broadcasted_iota(jnp.int32, (L, half), 0)
    low = lane_h < N
    src = jnp.where(low, lane_h, lane_h - N)
    strict = src < row_h
    incl = src <= row_h
    eye_half = jnp.where(src == row_h, 1.0, 0.0)
    eye_cat = jnp.concatenate([eye_half, eye_half], axis=1)

    def elementwise(g, ins):
        gs = slice(g * GW, (g + 1) * GW)
        w0, a0, k_k, k_a, r_k = (par[i:i + 1, gs] for i in range(5))
        r, k, v, lw, la = ins
        logw = -(RWKV_DECAY_SCALE * LOG2E) * _sigmoid(w0 + lw)
        a_sig = _sigmoid(a0 + la)
        kkr = k * k_k
        k2 = k * (1.0 + (a_sig - 1.0) * k_a)
        cs = logw
        sh = 1
        while sh < L:
            cs = cs + jnp.where(rowi >= sh, pltpu.roll(cs, sh, axis=0), 0.0)
            sh *= 2
        return r, v, logw, a_sig, kkr, k2, cs, r * k2 * r_k

    def pre_dots(ew, ss, bsum):
        r, v, logw, a_sig, kkr, k2, cs, _ = ew
        kk = kkr / jnp.maximum(jnp.sqrt(ss), 1e-12)
        ginv = jnp.exp2(-cs)
        atb = (-kk * jnp.exp2(cs - logw)).astype(BF16)
        rtb = (r * jnp.exp2(cs)).astype(BF16)
        bt = (kk * a_sig * ginv).astype(BF16)
        kt = (k2 * ginv).astype(BF16)
        zero = jnp.zeros_like(atb)
        lhs = jnp.concatenate([jnp.where(m, x, zero) for m in hmask for x in (atb, rtb)], axis=0)
        return atb, rtb, bt, kt, bsum * v, jnp.exp2(cs[L - 1:L]), lhs, jnp.concatenate([bt, kt, kt, bt], axis=0)

    def cat(parts, even_sel, odd_sel, mask):
        cols = [jnp.where(mask, jnp.where(low, even_sel(parts[2 * c]), odd_sel(parts[2 * c + 1])), 0.0)
                for c in range(HG // 2)]
        return jnp.concatenate(cols, axis=1)

    lo = lambda x: x[:, :half]
    hi = lambda x: x[:, half:]

    def body(ci, carry):
        r0 = pl.multiple_of(ci * L, L)
        sl = pl.ds(r0, L)
        G = range(NG)
        ews = [elementwise(g, tuple(ref[sl, g * GW:(g + 1) * GW]
                                    for ref in (r_ref, k_ref, v_ref, lw_ref, la_ref))) for g in G]
        sums = _dot_exact_rhs(jnp.concatenate([x for ew in ews for x in (ew[4] * ew[4], ew[7])], axis=0), seg1)
        pre = [pre_dots(ews[g], sums[2 * g * L:(2 * g + 1) * L], sums[(2 * g + 1) * L:(2 * g + 2) * L]) for g in G]
        P = _bdot([p[6] for p in pre], [p[7] for p in pre], "nt")
        pat = [[P[g][2 * h * L:(2 * h + 1) * L] for h in range(HG)] for g in G]
        prt = [[P[g][(2 * h + 1) * L:(2 * h + 2) * L] for h in range(HG)] for g in G]
        ncat = [cat(pat[g], lo, hi, strict) for g in G]
        nb = [n.astype(BF16) for n in ncat]
        pw = _bdot(nb, [_stack_heads(n, hmask) for n in nb], "nn")
        tinv = [eye_cat + n for n in ncat]
        power = 2
        while power < L:
            pwb = [x.astype(BF16) for x in pw]
            pst = [_stack_heads(x, hmask) for x in pwb]
            if 2 * power < L:
                d = _bdot([jnp.concatenate([tinv[g].astype(BF16), pwb[g]], axis=0) for g in G], pst, "nn")
                tinv = [tinv[g] + d[g][:L] for g in G]
                pw = [d[g][L:] for g in G]
            else:
                d = _bdot([t.astype(BF16) for t in tinv], pst, "nn")
                tinv = [tinv[g] + d[g] for g in G]
            power *= 2
        akrk = [jnp.concatenate([cat(pat[g], hi, lo, strict), cat(prt[g], hi, lo, incl)], axis=0).astype(BF16)
                for g in G]
        kv = _bdot(akrk, [_stack_heads(ews[g][1].astype(BF16), hmask) for g in G], "nn")
        for g in G:
            gs = slice(g * GW, (g + 1) * GW)
            atb, rtb, bt, kt, bon, gl = pre[g][:6]
            at_ref[sl, gs] = atb
            rt_ref[sl, gs] = rtb
            bk_ref[pl.ds(pl.multiple_of(2 * r0, 2 * L), L), gs] = bt
            bk_ref[pl.ds(pl.multiple_of(2 * r0 + L, L), L), gs] = kt
            tc_ref[sl, gs] = tinv[g].astype(BF16)
            rb_ref[sl, gs] = cat(prt[g], lo, hi, incl).astype(BF16)
            akv_ref[sl, gs] = kv[g][:L]
            rkv_ref[sl, gs] = kv[g][L:]
            bon_ref[sl, gs] = bon
            gl_ref[ci, :, gs] = gl
        return carry

    lax.fori_loop(0, TB // L, body, 0)


def _rwkv_state_kernel(at_ref, rt_ref, bk_ref, tc_ref, rb_ref, akv_ref, rkv_ref, gl_ref, v_ref,
                       y_ref, ht_ref, *, chunk):
    TB = v_ref.shape[0]
    L = chunk
    N = RWKV_HEAD
    GW = ht_ref.shape[1]
    NG = v_ref.shape[1] // GW

    @pl.when(pl.program_id(1) == 0)
    def _():
        ht_ref[...] = jnp.zeros_like(ht_ref)

    hmask = _head_masks(L, GW, N)
    same = _same_head(GW, N)

    def body(ci, carry):
        r0 = pl.multiple_of(ci * L, L)
        sl = pl.ds(r0, L)
        sl2 = pl.ds(pl.multiple_of(2 * r0, 2 * L), 2 * L)
        G = range(NG)
        gsl = [slice(g * GW, (g + 1) * GW) for g in G]
        ht = [ht_ref[g] for g in G]
        xr = _bdot([jnp.concatenate([at_ref[sl, gs], rt_ref[sl, gs]], axis=0) for gs in gsl],
                   [h.astype(BF16) for h in ht], "nt")
        xb = [(xr[g][:L] + akv_ref[sl, gsl[g]]).astype(BF16) for g in G]
        u = _bdot([tc_ref[sl, gs] for gs in gsl], [_stack_heads(x, hmask) for x in xb], "nn")
        ub = [x.astype(BF16) for x in u]
        yd = _bdot([rb_ref[sl, gs] for gs in gsl], [_stack_heads(x, hmask) for x in ub], "nn")
        dht = _bdot([jnp.concatenate([ub[g], v_ref[sl, gsl[g]].astype(BF16)], axis=0) for g in G],
                    [bk_ref[sl2, gs] for gs in gsl], "tn")
        for g in G:
            ht_ref[g] = (ht[g] + jnp.where(same, dht[g], 0.0)) * gl_ref[ci, :, gsl[g]]
            y_ref[sl, gsl[g]] = xr[g][L:] + yd[g] + rkv_ref[sl, gsl[g]]
        return carry

    lax.fori_loop(0, TB // L, body, 0)


def _rwkv_out_kernel(y_ref, bon_ref, g_ref, ln_ref, o_ref):
    N = RWKV_HEAD
    GW = MXU_DIM
    seg_mean = _head_ones(GW, N, 1.0 / N, BF16)
    for g in range(y_ref.shape[1] // GW):
        gs = slice(g * GW, (g + 1) * GW)
        y = y_ref[:, gs]
        yc = y - _dot_exact_rhs(y, seg_mean)
        var = _dot_exact_rhs(yc * yc, seg_mean)
        yn = yc * lax.rsqrt(var + RWKV_LNX_EPS) * ln_ref[0:1, gs] + ln_ref[1:2, gs]
        o_ref[:, gs] = ((yn + bon_ref[:, gs]) * g_ref[:, gs]).astype(o_ref.dtype)


def _rwkv_core(r, k, v, lw, la, g, w0, a0, k_k, k_a, r_k, lnx_w, lnx_b, tb=RWKV_CHUNK):
    T, D = r.shape
    L = RWKV_CHUNK
    gw = MXU_DIM
    W = _pick(D, STEP_LANES)
    assert W % gw == 0
    P = D // W
    tb = _pick(T, tb)
    nc = tb // L
    par = jnp.stack([w0, a0, k_k, k_a, r_k.reshape(D)]).astype(F32)
    par = jnp.concatenate([par, jnp.zeros((3, D), F32)], axis=0)
    spec = pl.BlockSpec((tb, W), lambda p, c: (c, p))
    spec2 = pl.BlockSpec((2 * tb, W), lambda p, c: (c, p))
    gspec = pl.BlockSpec((nc, 1, W), lambda p, c: (c, 0, p))
    bf = jax.ShapeDtypeStruct((T, D), BF16)
    f32 = jax.ShapeDtypeStruct((T, D), F32)
    at, rt, bk, tc, rb, akv, rkv, bon, gl = pl.pallas_call(
        functools.partial(_rwkv_prep_kernel, chunk=L),
        grid=(P, T // tb),
        in_specs=[spec] * 5 + [pl.BlockSpec((8, W), lambda p, c: (0, p))],
        out_specs=[spec, spec, spec2, spec, spec, spec, spec, spec, gspec],
        out_shape=[bf, bf, jax.ShapeDtypeStruct((2 * T, D), BF16), bf, bf, f32, f32, f32,
                   jax.ShapeDtypeStruct((T // L, 1, D), F32)],
        compiler_params=_cparams(("parallel", "parallel")),
        name="rwkv_prep",
    )(r, k, v, lw, la, par)
    y = pl.pallas_call(
        functools.partial(_rwkv_state_kernel, chunk=L),
        grid=(P, T // tb),
        in_specs=[spec, spec, spec2, spec, spec, spec, spec, gspec, spec],
        out_specs=spec,
        out_shape=f32,
        scratch_shapes=[pltpu.VMEM((W // gw, gw, gw), F32)],
        compiler_params=_cparams(("parallel", "arbitrary")),
        name="rwkv_state",
    )(at, rt, bk, tc, rb, akv, rkv, gl, v)
    ln = jnp.stack([lnx_w, lnx_b]).astype(F32)
    ln = jnp.concatenate([ln, jnp.zeros((6, D), F32)], axis=0)
    return pl.pallas_call(
        _rwkv_out_kernel,
        grid=(P, T // tb),
        in_specs=[spec, spec, spec, pl.BlockSpec((8, W), lambda p, c: (0, p))],
        out_specs=spec,
        out_shape=bf,
        compiler_params=_cparams(("parallel", "parallel")),
        name="rwkv_out",
    )(y, bon, g, ln)


def _mixer_even(h, norm_g, w_in, j, b_i, b_f, mlstm_g, lb_logits, hgrn_g, w_out):
    D = h.shape[1]
    H = MLSTM_HEADS
    mix_a = D // 2
    dv = mix_a // H
    dk = dv // 2
    na = 2 * H * dk + 2 * mix_a
    hn = _rmsnorm(h, norm_g, BF16)
    w_t = jnp.swapaxes(w_in, 1, 2)
    nb = w_in.shape[2] - na - 2 * H
    proj_a = _mm(hn, w_t, layer=j, cols=(0, na), b_t=True, name="mm_mlstm_in")
    gates = _mm(hn, w_t, layer=j, cols=(na, LANES), b_t=True, name="mm_gates")
    proj_b = _mm(hn, w_t, layer=j, cols=(na + 2 * H, nb), b_t=True, name="mm_hgrn_in")
    bias_row = jnp.zeros((1, LANES), F32).at[0, :H].set(b_i.astype(F32)).at[0, H:2 * H].set(b_f.astype(F32))
    y = jnp.zeros((h.shape[0], D), BF16)
    y = _mlstm(proj_a, gates, bias_row, mlstm_g, dk, dv, y)
    y = _hgrn(proj_b, lb_logits, j, hgrn_g, y)
    return _mm(y, w_out, layer=j, res=h, name="mm_mix_out")


def _mixer_odd(h, norm_g, j, mu, w0, w1, w2, a0, a1, a2, g1, g2, k_k, k_a, r_k, w_r, w_k, w_v, w_o, lnx_w, lnx_b):
    s_r, s_k, s_v, lw1, la1, lg1 = _rwkv_shift(h, norm_g, mu[j], w1, a1, g1, j)
    r = _mm(s_r, w_r, layer=j, name="mm_rwkv_r")
    k = _mm(s_k, w_k, layer=j, name="mm_rwkv_k")
    v = _mm(s_v, w_v, layer=j, name="mm_rwkv_v")
    wide = dict(tm=512, tn=h.shape[1])
    lw = _mm(lw1, w2, layer=j, name="mm_lora_w2", **wide)
    la = _mm(la1, a2, layer=j, name="mm_lora_a2", **wide)
    g = _mm(lg1, g2, layer=j, name="mm_lora_g2", **wide)
    y = _rwkv_core(r, k, v, lw, la, g, w0[j], a0[j], k_k[j], k_a[j], r_k[j], lnx_w[j], lnx_b[j])
    return _mm(y, w_o, layer=j, res=h, name="mm_rwkv_out")


def kernel(x, mem, norm_mix_g, norm_xattn_g, norm_mlp_g, final_norm_g, mem_norm_g, ab_w_in, mlstm_b_i, mlstm_b_f, mlstm_norm_g, hgrn_lb_logits, hgrn_norm_g, ab_w_out, rwkv_mu, rwkv_w0, rwkv_w1, rwkv_w2, rwkv_a0, rwkv_a1, rwkv_a2, rwkv_g1, rwkv_g2, rwkv_k_k, rwkv_k_a, rwkv_r_k, rwkv_w_r, rwkv_w_k, rwkv_w_v, rwkv_w_o, rwkv_lnx_w, rwkv_lnx_b, xattn_w_q, xattn_w_o, mem_w_kv, mlp_w_up, mlp_w_down):
    B, T, D = x.shape
    depth = norm_mix_g.shape[0]
    M = mem.shape[1]
    xscale = (D // XATTN_HEADS) ** -0.5
    outs = []
    for b in range(B):
        mem_kv = _mm(_rmsnorm(mem[b], mem_norm_g, BF16), mem_w_kv, out_dtype=BF16, name="mm_mem_kv")
        h = x[b]
        for layer in range(depth):
            j = layer // 2
            if layer % 2 == 0:
                h = _mixer_even(h, norm_mix_g[layer], ab_w_in, j, mlstm_b_i[j], mlstm_b_f[j], mlstm_norm_g[j],
                                hgrn_lb_logits, hgrn_norm_g[j], ab_w_out)
            else:
                h = _mixer_odd(h, norm_mix_g[layer], j, rwkv_mu, rwkv_w0, rwkv_w1, rwkv_w2, rwkv_a0, rwkv_a1,
                               rwkv_a2, rwkv_g1, rwkv_g2, rwkv_k_k, rwkv_k_a, rwkv_r_k,
                               rwkv_w_r, rwkv_w_k, rwkv_w_v, rwkv_w_o, rwkv_lnx_w, rwkv_lnx_b)
            wqk, vwo = _xattn_weights(xattn_w_q, xattn_w_o, layer, mem_kv)
            p = _xattn_scores(h, norm_xattn_g[layer], wqk, xscale, M)
            h, hn = _xattn_out(p, vwo, h, norm_mlp_g[layer])
            up = _mm(hn, mlp_w_up, layer=layer, act="relu2", out_dtype=BF16, name="mm_mlp_up")
            for k0 in range(0, up.shape[1], D):
                h = _mm(up, mlp_w_down, layer=layer, res=h, krange=(k0, D), name="mm_mlp_down")
        outs.append(_rmsnorm(h, final_norm_g, F32))
    return jnp.stack(outs)
```
